```python
import math
import jax
import jax.numpy as jnp
from jax import lax
import numpy as np

D_MODEL = 1024
BATCH = 4
SEQ = 8192
DEPTH = 2
DEC_BATCH = 32
DEC_SEQ = 64
PAST_LEN = 1024

CHUNK = 64
QBLOCK = 128
HEAD_DIM = 64
N_FOX_HEADS = 6
N_MLA_HEADS = 6
N_DIFF_HEADS = 4
MLA_Q_RANK = 384
MLA_KV_RANK = 256
MLA_NOPE_DIM = 64
MLA_ROPE_DIM = 32
MLA_V_DIM = 64
DIFF_QK_DIM = 32
DIFF_V_DIM = 64
FOX_WIDTH = N_FOX_HEADS * HEAD_DIM
MLA_WIDTH = N_MLA_HEADS * MLA_V_DIM
DIFF_WIDTH = N_DIFF_HEADS * DIFF_V_DIM
D_MIX = FOX_WIDTH + MLA_WIDTH + DIFF_WIDTH
D_FF = 4 * D_MODEL
CONV_WIDTH = 3
PLE_DIM = 256
ROPE_THETA = 10000.0
RMS_EPS = 1e-6
NEG_INF = -1e30

OFF_FOX_Q = 0
OFF_FOX_K = OFF_FOX_Q + FOX_WIDTH
OFF_FOX_V = OFF_FOX_K + FOX_WIDTH
OFF_FOX_F = OFF_FOX_V + FOX_WIDTH
OFF_MLA_CQ = OFF_FOX_F + N_FOX_HEADS
OFF_MLA_CKV = OFF_MLA_CQ + MLA_Q_RANK
OFF_MLA_KR = OFF_MLA_CKV + MLA_KV_RANK
OFF_DIFF_Q = OFF_MLA_KR + MLA_ROPE_DIM
OFF_DIFF_K = OFF_DIFF_Q + N_DIFF_HEADS * 2 * DIFF_QK_DIM
OFF_DIFF_V = OFF_DIFF_K + N_DIFF_HEADS * 2 * DIFF_QK_DIM
IN_WIDTH = OFF_DIFF_V + DIFF_WIDTH

kernel_name = "hybrid_fox_mla_diff_streaming_step"


def _normal(k, shape, scale):
    return scale * jax.random.normal(k, shape, jnp.float32)


def rmsnorm(x, g):
    x32 = x.astype(jnp.float32)
    y = x32 * lax.rsqrt(jnp.mean(x32 * x32, axis=-1, keepdims=True) + RMS_EPS)
    return (y * g.astype(jnp.float32)).astype(x.dtype)


def rope(x, pos):
    half = MLA_ROPE_DIM // 2
    inv_freq = ROPE_THETA ** (-jnp.arange(half, dtype=jnp.float32) / half)
    ang = pos.astype(jnp.float32)[:, None] * inv_freq[None, :]
    shape = (1, pos.shape[0]) + (1,) * (x.ndim - 3) + (half,)
    cos = jnp.cos(ang).reshape(shape)
    sin = jnp.sin(ang).reshape(shape)
    x32 = x.astype(jnp.float32)
    x1, x2 = x32[..., :half], x32[..., half:]
    return jnp.concatenate([x1 * cos - x2 * sin, x1 * sin + x2 * cos], axis=-1).astype(x.dtype)


def alibi_slopes(n):
    return jnp.asarray(2.0 ** (-8.0 * np.arange(1, n + 1) / n), dtype=jnp.float32)


def masked_softmax(s, mask):
    return jax.nn.softmax(jnp.where(mask, s.astype(jnp.float32), NEG_INF), axis=-1)


def token_projections(xn, pos, prm):
    B, T, _ = xn.shape
    z = xn @ prm["w_in"]
    fox_q = z[..., OFF_FOX_Q:OFF_FOX_K].reshape(B, T, N_FOX_HEADS, HEAD_DIM)
    fox_k = z[..., OFF_FOX_K:OFF_FOX_V].reshape(B, T, N_FOX_HEADS, HEAD_DIM)
    fox_v = z[..., OFF_FOX_V:OFF_FOX_F].reshape(B, T, N_FOX_HEADS, HEAD_DIM)
    fox_logf = jax.nn.log_sigmoid((z[..., OFF_FOX_F:OFF_MLA_CQ] + prm["b_forget"]).astype(jnp.float32))
    c_q = rmsnorm(z[..., OFF_MLA_CQ:OFF_MLA_CKV], prm["mla_q_norm"])
    q_mla = (c_q @ prm["w_mla_uq"]).reshape(B, T, N_MLA_HEADS, MLA_NOPE_DIM + MLA_ROPE_DIM)
    mla_ckv = rmsnorm(z[..., OFF_MLA_CKV:OFF_MLA_KR], prm["mla_kv_norm"])
    mla_krope = rope(z[..., OFF_MLA_KR:OFF_DIFF_Q], pos)
    diff_q = z[..., OFF_DIFF_Q:OFF_DIFF_K].reshape(B, T, N_DIFF_HEADS, 2 * DIFF_QK_DIM)
    diff_k = z[..., OFF_DIFF_K:OFF_DIFF_V].reshape(B, T, N_DIFF_HEADS, 2 * DIFF_QK_DIM)
    diff_v = z[..., OFF_DIFF_V:IN_WIDTH].reshape(B, T, N_DIFF_HEADS, DIFF_V_DIM)
    query = dict(fox_q=fox_q, mla_q_nope=q_mla[..., :MLA_NOPE_DIM],
                 mla_q_rope=rope(q_mla[..., MLA_NOPE_DIM:], pos), diff_q=diff_q)
    rows = (fox_k, fox_v, fox_logf, mla_ckv, mla_krope, diff_k, diff_v)
    return query, rows


def key_side(rows, prm):
    fox_k, fox_v, fox_logf, mla_ckv, mla_krope, diff_k, diff_v = rows
    B, Tk = fox_k.shape[:2]
    return dict(
        fox_k=fox_k, fox_v=fox_v,
        fox_c=jnp.cumsum(fox_logf.astype(jnp.float32), axis=1),
        mla_k_nope=(mla_ckv @ prm["w_mla_uk"]).reshape(B, Tk, N_MLA_HEADS, MLA_NOPE_DIM),
        mla_k_rope=mla_krope,
        mla_v=(mla_ckv @ prm["w_mla_uv"]).reshape(B, Tk, N_MLA_HEADS, MLA_V_DIM),
        diff_k=diff_k, diff_v=diff_v)


def attend_block(q, k, qpos, kpos, lam, lam_init, diff_subln):
    B, Tq = q["fox_q"].shape[:2]
    causal = kpos[None, :] <= qpos[:, None]
    chunk_causal = (kpos // CHUNK)[None, :] <= (qpos // CHUNK)[:, None]
    decay = jnp.transpose(q["fox_c"], (0, 2, 1))[:, :, :, None] - jnp.transpose(k["fox_c"], (0, 2, 1))[:, :, None, :]
    s_a = jnp.einsum("bqhd,bkhd->bhqk", q["fox_q"], k["fox_k"]).astype(jnp.float32) * (HEAD_DIM ** -0.5) + decay
    p_a = masked_softmax(s_a, causal).astype(k["fox_v"].dtype)
    o_a = jnp.einsum("bhqk,bkhd->bqhd", p_a, k["fox_v"])
    s_b = (jnp.einsum("bqhd,bkhd->bhqk", q["mla_q_nope"], k["mla_k_nope"])
           + jnp.einsum("bqhr,bkr->bhqk", q["mla_q_rope"], k["mla_k_rope"])).astype(jnp.float32)
    s_b = s_b * ((MLA_NOPE_DIM + MLA_ROPE_DIM) ** -0.5)
    p_b = masked_softmax(s_b, chunk_causal).astype(k["mla_v"].dtype)
    o_b = jnp.einsum("bhqk,bkhd->bqhd", p_b, k["mla_v"])
    dist = jnp.abs(qpos[:, None] - kpos[None, :]).astype(jnp.float32)
    alibi = -alibi_slopes(N_DIFF_HEADS)[:, None, None] * dist[None]
    q1, q2 = q["diff_q"][..., :DIFF_QK_DIM], q["diff_q"][..., DIFF_QK_DIM:]
    k1, k2 = k["diff_k"][..., :DIFF_QK_DIM], k["diff_k"][..., DIFF_QK_DIM:]
    s1 = jnp.einsum("bqhd,bkhd->bhqk", q1, k1).astype(jnp.float32) * (DIFF_QK_DIM ** -0.5) + alibi
    s2 = jnp.einsum("bqhd,bkhd->bhqk", q2, k2).astype(jnp.float32) * (DIFF_QK_DIM ** -0.5) + alibi
    p_c = masked_softmax(s1, chunk_causal) - lam * masked_softmax(s2, chunk_causal)
    o_c = jnp.einsum("bhqk,bkhd->bqhd", p_c.astype(k["diff_v"].dtype), k["diff_v"])
    o_c = rmsnorm(o_c, diff_subln) * (1.0 - lam_init)
    return jnp.concatenate([o_a.reshape(B, Tq, FOX_WIDTH), o_b.reshape(B, Tq, MLA_WIDTH),
                            o_c.reshape(B, Tq, DIFF_WIDTH)], axis=-1)


def sweep_query_blocks(query, keys, kpos, lam, lam_init, diff_subln):
    B, S = query["fox_q"].shape[:2]

    def one_block(start):
        qb = {name: lax.dynamic_slice_in_dim(a, start, QBLOCK, axis=1) for name, a in query.items()}
        qpos = start + jnp.arange(QBLOCK)
        return attend_block(qb, keys, qpos, kpos, lam, lam_init, diff_subln)

    out = lax.map(one_block, jnp.arange(S // QBLOCK) * QBLOCK)
    return jnp.transpose(out, (1, 0, 2, 3)).reshape(B, S, D_MIX)


def conv_ffn(xn, conv_left, prm):
    u = xn @ prm["w_ffn_up"]
    gate, val = u[..., :D_FF], u[..., D_FF:]
    T = gate.shape[1]
    g_ext = jnp.concatenate([conv_left.astype(gate.dtype), gate], axis=1)
    w = prm["ffn_conv_w"]
    conv = sum(w[j] * g_ext[:, j:j + T] for j in range(CONV_WIDTH)) + prm["ffn_conv_b"]
    out = (jax.nn.gelu(conv, approximate=True) * val) @ prm["w_ffn_down"]
    return out, g_ext[:, -(CONV_WIDTH - 1):]


def trunk_layer(h, p, pos, cache_rows, conv_left, prm, lam_init):
    T = h.shape[1]
    query, new_rows = token_projections(rmsnorm(h, prm["norm_mix_pre"]), pos, prm)
    if cache_rows is None:
        key_rows = new_rows
    else:
        key_rows = tuple(jnp.concatenate([c.astype(n.dtype), n], axis=1) for c, n in zip(cache_rows, new_rows))
    keys = key_side(key_rows, prm)
    query["fox_c"] = keys["fox_c"][:, -T:]
    kpos = jnp.arange(keys["fox_k"].shape[1])
    lam = (jnp.exp(jnp.sum(prm["lq1"].astype(jnp.float32) * prm["lk1"].astype(jnp.float32)))
           - jnp.exp(jnp.sum(prm["lq2"].astype(jnp.float32) * prm["lk2"].astype(jnp.float32))) + lam_init)
    if cache_rows is None:
        mix = sweep_query_blocks(query, keys, kpos, lam, lam_init, prm["diff_subln"])
    else:
        mix = attend_block(query, keys, pos, kpos, lam, lam_init, prm["diff_subln"])
    h = h + rmsnorm(mix @ prm["w_out"], prm["norm_mix_post"])
    ffn_out, conv_state = conv_ffn(rmsnorm(h, prm["norm_ffn_pre"]), conv_left, prm)
    h = h + rmsnorm(ffn_out, prm["norm_ffn_post"])
    gate = jax.nn.sigmoid(rmsnorm(h, prm["norm_ple_pre"]) @ prm["w_ple_gate"])
    h = h + rmsnorm((p @ prm["w_ple_proj"]) * gate, prm["norm_ple_post"])
    return h, new_rows, conv_state


def stack_layers(per_layer):
    return tuple(jnp.stack(a, axis=0) for a in zip(*per_layer))


def setup_inputs(seed: int = 0) -> dict:
    key = jax.random.key(seed)
    ks = iter(jax.random.split(key, 48))
    L = DEPTH
    gain = lambda shape: 1.0 + _normal(next(ks), shape, 0.05)
    return {
        "x_prompt": _normal(next(ks), (BATCH, SEQ, D_MODEL), 1.0),
        "x_sample": _normal(next(ks), (DEC_BATCH, DEC_SEQ, D_MODEL), 1.0),
        "cache_fox_k": _normal(next(ks), (L, DEC_BATCH, PAST_LEN, N_FOX_HEADS, HEAD_DIM), 1.0),
        "cache_fox_v": _normal(next(ks), (L, DEC_BATCH, PAST_LEN, N_FOX_HEADS, HEAD_DIM), 1.0),
        "cache_fox_logf": jax.nn.log_sigmoid(3.0 + _normal(next(ks), (L, DEC_BATCH, PAST_LEN, N_FOX_HEADS), 1.0)),
        "cache_mla_ckv": _normal(next(ks), (L, DEC_BATCH, PAST_LEN, MLA_KV_RANK), 1.0),
        "cache_mla_krope": _normal(next(ks), (L, DEC_BATCH, PAST_LEN, MLA_ROPE_DIM), 1.0),
        "cache_diff_k": _normal(next(ks), (L, DEC_BATCH, PAST_LEN, N_DIFF_HEADS, 2 * DIFF_QK_DIM), 1.0),
        "cache_diff_v": _normal(next(ks), (L, DEC_BATCH, PAST_LEN, N_DIFF_HEADS, DIFF_V_DIM), 1.0),
        "state_ffn_conv": _normal(next(ks), (L, DEC_BATCH, CONV_WIDTH - 1, D_FF), 1.0),
        "p_prompt": _normal(next(ks), (L, BATCH, SEQ, PLE_DIM), 1.0),
        "p_sample": _normal(next(ks), (L, DEC_BATCH, DEC_SEQ, PLE_DIM), 1.0),
        "w_in": _normal(next(ks), (L, D_MODEL, IN_WIDTH), D_MODEL ** -0.5),
        "b_forget": 3.0 + _normal(next(ks), (L, N_FOX_HEADS), 0.1),
        "mla_q_norm": gain((L, MLA_Q_RANK)),
        "w_mla_uq": _normal(next(ks), (L, MLA_Q_RANK, N_MLA_HEADS * (MLA_NOPE_DIM + MLA_ROPE_DIM)), MLA_Q_RANK ** -0.5),
        "mla_kv_norm": gain((L, MLA_KV_RANK)),
        "w_mla_uk": _normal(next(ks), (L, MLA_KV_RANK, N_MLA_HEADS * MLA_NOPE_DIM), MLA_KV_RANK ** -0.5),
        "w_mla_uv": _normal(next(ks), (L, MLA_KV_RANK, N_MLA_HEADS * MLA_V_DIM), MLA_KV_RANK ** -0.5),
        "diff_lambda_q1": _normal(next(ks), (L, DIFF_QK_DIM), 0.1),
        "diff_lambda_k1": _normal(next(ks), (L, DIFF_QK_DIM), 0.1),
        "diff_lambda_q2": _normal(next(ks), (L, DIFF_QK_DIM), 0.1),
        "diff_lambda_k2": _normal(next(ks), (L, DIFF_QK_DIM), 0.1),
        "diff_subln": gain((L, DIFF_V_DIM)),
        "w_out": _normal(next(ks), (L, D_MIX, D_MODEL), D_MIX ** -0.5),
        "norm_mix_pre": gain((L, D_MODEL)),
        "norm_mix_post": gain((L, D_MODEL)),
        "norm_ffn_pre": gain((L, D_MODEL)),
        "norm_ffn_post": gain((L, D_MODEL)),
        "norm_ple_pre": gain((L, D_MODEL)),
        "norm_ple_post": gain((L, D_MODEL)),
        "w_ffn_up": _normal(next(ks), (L, D_MODEL, 2 * D_FF), D_MODEL ** -0.5),
        "ffn_conv_w": _normal(next(ks), (L, CONV_WIDTH, D_FF), CONV_WIDTH ** -0.5),
        "ffn_conv_b": _normal(next(ks), (L, D_FF), 0.02),
        "w_ffn_down": _normal(next(ks), (L, D_FF, D_MODEL), D_FF ** -0.5),
        "w_ple_gate": _normal(next(ks), (L, D_MODEL, D_MODEL), D_MODEL ** -0.5),
        "w_ple_proj": _normal(next(ks), (L, PLE_DIM, D_MODEL), PLE_DIM ** -0.5),
    }


def reference(x_prompt, x_sample, cache_fox_k, cache_fox_v, cache_fox_logf, cache_mla_ckv,
              cache_mla_krope, cache_diff_k, cache_diff_v, state_ffn_conv, p_prompt, p_sample,
              w_in, b_forget, mla_q_norm, w_mla_uq, mla_kv_norm, w_mla_uk, w_mla_uv,
              diff_lambda_q1, diff_lambda_k1, diff_lambda_q2, diff_lambda_k2, diff_subln, w_out,
              norm_mix_pre, norm_mix_post, norm_ffn_pre, norm_ffn_post, norm_ple_pre, norm_ple_post,
              w_ffn_up, ffn_conv_w, ffn_conv_b, w_ffn_down, w_ple_gate, w_ple_proj):
    B, S, _ = x_prompt.shape
    P = cache_fox_k.shape[2]
    T = x_sample.shape[1]
    pos_p = jnp.arange(S)
    pos_s = P + jnp.arange(T)
    hp, hs = x_prompt, x_sample
    rows_prompt, rows_sample, conv_prompt, conv_sample = [], [], [], []
    for l in range(DEPTH):
        prm = dict(w_in=w_in[l], b_forget=b_forget[l], mla_q_norm=mla_q_norm[l], w_mla_uq=w_mla_uq[l],
                   mla_kv_norm=mla_kv_norm[l], w_mla_uk=w_mla_uk[l], w_mla_uv=w_mla_uv[l],
                   lq1=diff_lambda_q1[l], lk1=diff_lambda_k1[l], lq2=diff_lambda_q2[l], lk2=diff_lambda_k2[l],
                   diff_subln=diff_subln[l], w_out=w_out[l],
                   norm_mix_pre=norm_mix_pre[l], norm_mix_post=norm_mix_post[l],
                   norm_ffn_pre=norm_ffn_pre[l], norm_ffn_post=norm_ffn_post[l],
                   norm_ple_pre=norm_ple_pre[l], norm_ple_post=norm_ple_post[l],
                   w_ffn_up=w_ffn_up[l], ffn_conv_w=ffn_conv_w[l], ffn_conv_b=ffn_conv_b[l],
                   w_ffn_down=w_ffn_down[l], w_ple_gate=w_ple_gate[l], w_ple_proj=w_ple_proj[l])
        lam_init = 0.8 - 0.6 * math.exp(-0.3 * l)
        zeros_left = jnp.zeros((B, CONV_WIDTH - 1, D_FF), hp.dtype)
        hp, rp, cp = trunk_layer(hp, p_prompt[l], pos_p, None, zeros_left, prm, lam_init)
        cache_l = (cache_fox_k[l], cache_fox_v[l], cache_fox_logf[l], cache_mla_ckv[l],
                   cache_mla_krope[l], cache_diff_k[l], cache_diff_v[l])
        hs, rs, cs = trunk_layer(hs, p_sample[l], pos_s, cache_l, state_ffn_conv[l], prm, lam_init)
        rows_prompt.append(rp)
        rows_sample.append(rs)
        conv_prompt.append(cp)
        conv_sample.append(cs)
    fox_k_p, fox_v_p, fox_logf_p, mla_ckv_p, mla_krope_p, diff_k_p, diff_v_p = stack_layers(rows_prompt)
    fox_k_s, fox_v_s, fox_logf_s, mla_ckv_s, mla_krope_s, diff_k_s, diff_v_s = stack_layers(rows_sample)
    conv_p = jnp.stack(conv_prompt, axis=0)
    conv_s = jnp.stack(conv_sample, axis=0)
    return (hp, hs, fox_k_p, fox_v_p, fox_logf_p, mla_ckv_p, mla_krope_p, diff_k_p, diff_v_p, conv_p,
            fox_k_s, fox_v_s, fox_logf_s, mla_ckv_s, mla_krope_s, diff_k_s, diff_v_s, conv_s)
```

```python
import functools
import math

import jax
import jax.numpy as jnp
import numpy as np
from jax import lax
from jax.experimental import pallas as pl
from jax.experimental.pallas import tpu as pltpu

F32 = jnp.float32
BF16 = jnp.bfloat16

D_MODEL = 1024
HEAD_DIM = 64
N_FOX_HEADS = 6
N_MLA_HEADS = 6
N_DIFF_HEADS = 4
MLA_Q_RANK = 384
MLA_KV_RANK = 256
MLA_NOPE_DIM = 64
MLA_ROPE_DIM = 32
MLA_V_DIM = 64
DIFF_QK_DIM = 32
DIFF_V_DIM = 64
FOX_WIDTH = N_FOX_HEADS * HEAD_DIM
MLA_WIDTH = N_MLA_HEADS * MLA_V_DIM
DIFF_WIDTH = N_DIFF_HEADS * DIFF_V_DIM
D_FF = 4 * D_MODEL
CONV_WIDTH = 3
PLE_DIM = 256
CHUNK = 64
ROPE_THETA = 10000.0
RMS_EPS = 1e-6
NEG_INF = -1e30
LOG2E = math.log2(math.e)

OFF_FOX_Q = 0
OFF_FOX_F = 3 * FOX_WIDTH
OFF_MLA_CQ = OFF_FOX_F + N_FOX_HEADS
OFF_MLA_CKV = OFF_MLA_CQ + MLA_Q_RANK
OFF_MLA_KR = OFF_MLA_CKV + MLA_KV_RANK
OFF_DIFF_Q = OFF_MLA_KR + MLA_ROPE_DIM
IN_WIDTH = OFF_DIFF_Q + 2 * N_DIFF_HEADS * 2 * DIFF_QK_DIM + DIFF_WIDTH

LANE = 128
V7X_VMEM_BYTES = 64 * 1024 * 1024
VMEM_LIMIT = (V7X_VMEM_BYTES * 7) // 8

C_FOX = 0
C_DIFF = C_FOX + 3 * FOX_WIDTH
C_CQ = C_DIFF + 3 * DIFF_WIDTH
C_CKV = C_CQ + MLA_Q_RANK
C_KR = C_CKV + MLA_KV_RANK
C_F = C_KR + 2 * LANE
W_ALL = C_F + LANE
N_MLA_PAIRS = N_MLA_HEADS // 2
MLA_PAIR_W = 2 * LANE
MLA_QK_W = N_MLA_PAIRS * MLA_PAIR_W

FOX_QSCALE = HEAD_DIM ** -0.5 * LOG2E
MLA_QSCALE = (MLA_NOPE_DIM + MLA_ROPE_DIM) ** -0.5 * LOG2E
DIFF_QSCALE = DIFF_QK_DIM ** -0.5 * LOG2E


def _cparams(sem):
    return pltpu.CompilerParams(dimension_semantics=sem, vmem_limit_bytes=VMEM_LIMIT)


def _rms(x, g):
    return x * lax.rsqrt(jnp.mean(x * x, axis=-1, keepdims=True) + RMS_EPS) * g


def _dot(a, b):
    return jnp.dot(a, b, preferred_element_type=F32)


def _dot_nt(a, b):
    return lax.dot_general(a, b, (((1,), (1,)), ((), ())), preferred_element_type=F32)


def _proj_kernel(h_ref, gpre_ref, w_ref, bf_ref, gq_ref, gkv_ref, wuq_ref, wukv_ref, cos_ref, sin_ref,
                 foxk_ref, foxv_ref, logf_ref, ckv_ref, krope_ref, diffk_ref, diffv_ref,
                 foxbf_ref, diffbf_ref, mlaq_ref, mlak_ref, mlav_ref):
    xn = _rms(h_ref[...], gpre_ref[...]).astype(BF16)
    cos = cos_ref[...]
    sin = sin_ref[...]

    z = _dot(xn, w_ref[:, C_FOX:C_FOX + 3 * FOX_WIDTH])
    foxbf_ref[:, :FOX_WIDTH] = (z[:, :FOX_WIDTH] * FOX_QSCALE).astype(BF16)
    foxbf_ref[:, FOX_WIDTH:] = z[:, FOX_WIDTH:].astype(BF16)
    foxk_ref[...] = z[:, FOX_WIDTH:2 * FOX_WIDTH]
    foxv_ref[...] = z[:, 2 * FOX_WIDTH:3 * FOX_WIDTH]

    z = _dot(xn, w_ref[:, C_DIFF:C_DIFF + 3 * DIFF_WIDTH])
    diffbf_ref[:, :DIFF_WIDTH] = (z[:, :DIFF_WIDTH] * DIFF_QSCALE).astype(BF16)
    diffbf_ref[:, DIFF_WIDTH:] = z[:, DIFF_WIDTH:].astype(BF16)
    diffk_ref[...] = z[:, DIFF_WIDTH:2 * DIFF_WIDTH]
    diffv_ref[...] = z[:, 2 * DIFF_WIDTH:3 * DIFF_WIDTH]

    cq = _rms(_dot(xn, w_ref[:, C_CQ:C_CQ + MLA_Q_RANK]), gq_ref[...]).astype(BF16)
    q2 = _dot(cq, wuq_ref[...])
    for p in range(N_MLA_PAIRS):
        lo = p * MLA_PAIR_W
        roped = q2[:, lo:lo + MLA_PAIR_W] * cos + q2[:, MLA_QK_W + lo:MLA_QK_W + lo + MLA_PAIR_W] * sin
        mlaq_ref[:, lo:lo + MLA_PAIR_W] = (roped * MLA_QSCALE).astype(BF16)

    ckv = _rms(_dot(xn, w_ref[:, C_CKV:C_CKV + MLA_KV_RANK]), gkv_ref[...])
    ckv_ref[...] = ckv
    kv = _dot(ckv.astype(BF16), wukv_ref[...])
    mlav_ref[...] = kv[:, MLA_WIDTH:2 * MLA_WIDTH].astype(BF16)
    zkr = _dot(xn, w_ref[:, C_KR:C_KR + 2 * LANE])
    krp = zkr[:, :LANE] * cos[:, LANE:] + zkr[:, LANE:] * sin[:, LANE:]
    krope_ref[...] = krp[:, :MLA_ROPE_DIM]
    krp16 = krp.astype(BF16)
    for p in range(N_MLA_PAIRS):
        lo = p * MLA_PAIR_W
        mlak_ref[:, lo:lo + LANE] = kv[:, p * LANE:(p + 1) * LANE].astype(BF16)
        mlak_ref[:, lo + LANE:lo + 2 * LANE] = krp16

    zf = _dot(xn, w_ref[:, C_F:C_F + LANE]) + bf_ref[...]
    logf = -(jnp.maximum(-zf, 0.0) + jnp.log1p(jnp.exp(-jnp.abs(zf))))
    logf_ref[...] = logf[:, :N_FOX_HEADS]


def _proj_call(h, lw, cos, sin, tm):
    n = h.shape[0]
    ntab = cos.shape[0] // tm
    row = lambda i: (i, 0)
    full = lambda i: (0, 0)
    tab = lambda i: (i % ntab, 0)
    in_specs = [
        pl.BlockSpec((tm, D_MODEL), row),
        pl.BlockSpec((1, D_MODEL), full),
        pl.BlockSpec((D_MODEL, W_ALL), full),
        pl.BlockSpec((1, LANE), full),
        pl.BlockSpec((1, MLA_Q_RANK), full),
        pl.BlockSpec((1, MLA_KV_RANK), full),
        pl.BlockSpec((MLA_Q_RANK, 2 * MLA_QK_W), full),
        pl.BlockSpec((MLA_KV_RANK, 2 * MLA_WIDTH), full),
        pl.BlockSpec((tm, MLA_PAIR_W), tab),
        pl.BlockSpec((tm, MLA_PAIR_W), tab),
    ]
    widths = [(FOX_WIDTH, F32), (FOX_WIDTH, F32), (N_FOX_HEADS, F32), (MLA_KV_RANK, F32), (MLA_ROPE_DIM, F32),
              (DIFF_WIDTH, F32), (DIFF_WIDTH, F32),
              (3 * FOX_WIDTH, BF16), (3 * DIFF_WIDTH, BF16), (MLA_QK_W, BF16), (MLA_QK_W, BF16), (MLA_WIDTH, BF16)]
    out_shape = [jax.ShapeDtypeStruct((n, w), dt) for w, dt in widths]
    out_specs = [pl.BlockSpec((tm, w), row) for w, _ in widths]
    return pl.pallas_call(
        _proj_kernel, grid=(n // tm,), in_specs=in_specs, out_specs=out_specs, out_shape=out_shape,
        compiler_params=_cparams(("parallel",)), name="proj",
    )(h, lw["g_mix_pre"], lw["w_all"], lw["b_f"], lw["g_q"], lw["g_kv"], lw["w_uq2"], lw["w_ukv"], cos, sin)


def _kvup_kernel(ckv_ref, wukv_ref, kn_ref, v_ref):
    kv = _dot(ckv_ref[...].astype(BF16), wukv_ref[...])
    kn_ref[...] = kv[:, :MLA_WIDTH].astype(BF16)
    v_ref[...] = kv[:, MLA_WIDTH:].astype(BF16)


def _kvup_call(ckv, w_ukv, tm):
    n = ckv.shape[0]
    row = lambda i: (i, 0)
    return pl.pallas_call(
        _kvup_kernel, grid=(n // tm,),
        in_specs=[pl.BlockSpec((tm, MLA_KV_RANK), row), pl.BlockSpec((MLA_KV_RANK, 2 * MLA_WIDTH), lambda i: (0, 0))],
        out_specs=[pl.BlockSpec((tm, MLA_WIDTH), row), pl.BlockSpec((tm, MLA_WIDTH), row)],
        out_shape=[jax.ShapeDtypeStruct((n, MLA_WIDTH), BF16)] * 2,
        compiler_params=_cparams(("parallel",)), name="kvup",
    )(ckv, w_ukv)


def _cumsum_kernel(x_ref, o_ref, carry_sc, *, nch):
    lane = lax.broadcasted_iota(jnp.int32, (8, LANE), 1)

    @pl.when(pl.program_id(1) == 0)
    def _():
        carry_sc[...] = jnp.zeros_like(carry_sc)

    for j in range(nch):
        x = x_ref[0, j]
        d = 1
        while d < LANE:
            x = x + jnp.where(lane >= d, pltpu.roll(x, d, 1), 0.0)
            d *= 2
        c = x + carry_sc[...]
        carry_sc[...] = jnp.broadcast_to(c[:, LANE - 1:LANE], (8, LANE))
        for h in range(N_FOX_HEADS):
            o_ref[0, h, j * LANE:(j + 1) * LANE, :] = jnp.broadcast_to(c[h:h + 1, :], (LANE, LANE)).T * (-LOG2E)


def _cumsum_call(logf):
    b, t, hh = logf.shape
    nch = t // LANE
    per_step = math.gcd(nch, 8)
    x = jnp.pad(logf, ((0, 0), (0, 0), (0, 8 - hh))).reshape(b, nch, LANE, 8).transpose(0, 1, 3, 2)
    return pl.pallas_call(
        functools.partial(_cumsum_kernel, nch=per_step), grid=(b, nch // per_step),
        in_specs=[pl.BlockSpec((1, per_step, 8, LANE), lambda i, j: (i, j, 0, 0))],
        out_specs=pl.BlockSpec((1, N_FOX_HEADS, per_step * LANE, LANE), lambda i, j: (i, 0, j, 0)),
        out_shape=jax.ShapeDtypeStruct((b, N_FOX_HEADS, t, LANE), F32),
        scratch_shapes=[pltpu.VMEM((8, LANE), F32)],
        compiler_params=_cparams(("parallel", "arbitrary")), name="cumsum",
    )(x)


def _online_update(s, state, vt):
    m, l, acc = state
    m_new = jnp.maximum(m, jnp.max(s, axis=0, keepdims=True))
    alpha = jnp.exp2(m - m_new)
    p = jnp.exp2(s - m_new)
    l = alpha * l + jnp.sum(p, axis=0, keepdims=True)
    acc = alpha * acc + _dot(vt, p.astype(BF16))
    return m_new, l, acc


def _init_state(tq, dv):
    return (jnp.full((1, tq), NEG_INF, F32), jnp.zeros((1, tq), F32), jnp.zeros((dv, tq), F32))


def _key_query_iota(tk, tq):
    return (lax.broadcasted_iota(jnp.int32, (tk, tq), 0), lax.broadcasted_iota(jnp.int32, (tk, tq), 1))


def _store_pair(o_ref, o0, o1, row_scale=None):
    out = jnp.concatenate([o0, o1], axis=0).T
    if row_scale is not None:
        out = out * row_scale
    o_ref[0] = out.astype(BF16)


def _fox_kernel(q_ref, k_ref, vt_ref, cb_ref, o_ref, *, tq, tkd, qoff, blocked):
    i = pl.program_id(2)
    q = q_ref[0]
    lane = lax.broadcasted_iota(jnp.int32, (1, LANE), 1)
    qs = (jnp.where(lane < HEAD_DIM, q, jnp.zeros_like(q)), jnp.where(lane >= HEAD_DIM, q, jnp.zeros_like(q)))

    def scores(e, k, k0, nk):
        return _dot_nt(k, qs[e]) + jnp.tile(cb_ref[0, e, pl.ds(k0, nk), :], (1, tq // LANE))

    def off_step(j, states):
        k0 = pl.multiple_of(j * tq, tq)
        k = k_ref[0, pl.ds(k0, tq), :]
        return tuple(_online_update(scores(e, k, k0, tq), states[e],
                                    vt_ref[0, 0, j, e * HEAD_DIM:(e + 1) * HEAD_DIM, :]) for e in range(2))

    states = (_init_state(tq, HEAD_DIM), _init_state(tq, HEAD_DIM))
    if blocked:
        states = lax.fori_loop(0, i, off_step, states)
        k0 = pl.multiple_of(i * tq, tq)
        jd = i
    else:
        k0 = 0
        jd = 0
    k = k_ref[0, pl.ds(k0, tkd), :]
    kk, qq = _key_query_iota(tkd, tq)
    mask = (kk - qq) <= qoff
    outs = []
    for e in range(2):
        s = jnp.where(mask, scores(e, k, k0, tkd), NEG_INF)
        _, l, acc = _online_update(s, states[e], vt_ref[0, 0, jd, e * HEAD_DIM:(e + 1) * HEAD_DIM, :])
        outs.append(acc / l)
    _store_pair(o_ref, outs[0], outs[1])


def _mla_kernel(q_ref, k_ref, vt_ref, o_ref, *, tq, tkd, qoff, blocked):
    i = pl.program_id(2)
    q = q_ref[0]
    lane2 = lax.broadcasted_iota(jnp.int32, (1, MLA_PAIR_W), 1)
    sel = []
    for e in range(2):
        nope = (lane2 >= e * MLA_NOPE_DIM) & (lane2 < (e + 1) * MLA_NOPE_DIM)
        rope = (lane2 >= LANE + e * MLA_ROPE_DIM) & (lane2 < LANE + (e + 1) * MLA_ROPE_DIM)
        sel.append(jnp.where(nope | rope, q, jnp.zeros_like(q)))

    def off_step(j, states):
        k0 = pl.multiple_of(j * tq, tq)
        k = k_ref[0, pl.ds(k0, tq), :]
        return tuple(_online_update(_dot_nt(k, sel[e]), states[e],
                                    vt_ref[0, 0, j, e * MLA_V_DIM:(e + 1) * MLA_V_DIM, :]) for e in range(2))

    states = (_init_state(tq, MLA_V_DIM), _init_state(tq, MLA_V_DIM))
    if blocked:
        states = lax.fori_loop(0, i, off_step, states)
        k0 = pl.multiple_of(i * tq, tq)
        jd = i
    else:
        k0 = 0
        jd = 0
    k = k_ref[0, pl.ds(k0, tkd), :]
    kk, qq = _key_query_iota(tkd, tq)
    mask = (kk // CHUNK) <= (qq // CHUNK) + qoff // CHUNK
    outs = []
    for e in range(2):
        s = jnp.where(mask, _dot_nt(k, sel[e]), NEG_INF)
        _, l, acc = _online_update(s, states[e], vt_ref[0, 0, jd, e * MLA_V_DIM:(e + 1) * MLA_V_DIM, :])
        outs.append(acc / l)
    _store_pair(o_ref, outs[0], outs[1])


def _diff_kernel(q_ref, k_ref, vt_ref, slope_ref, lam_ref, subln_ref, o_ref, *, tq, tkd, qoff, blocked, lam_init):
    i = pl.program_id(2)
    q = q_ref[0]
    lane = lax.broadcasted_iota(jnp.int32, (1, LANE), 1)
    sel = [[jnp.where((lane >= e * 2 * DIFF_QK_DIM + t * DIFF_QK_DIM) & (lane < e * 2 * DIFF_QK_DIM + (t + 1) * DIFF_QK_DIM),
                      q, jnp.zeros_like(q)) for t in range(2)] for e in range(2)]
    slopes = [slope_ref[0, e:e + 1, 0:1] * LOG2E for e in range(2)]
    rel = qoff + i * tq
    vts = lambda j, e: vt_ref[0, 0, j, e * DIFF_V_DIM:(e + 1) * DIFF_V_DIM, :]

    def off_step(j, states):
        k0 = pl.multiple_of(j * tq, tq)
        k = k_ref[0, pl.ds(k0, tq), :]
        kpos = (j * tq - rel + lax.broadcasted_iota(jnp.int32, (tq, 1), 0)).astype(F32)
        return tuple(tuple(_online_update(_dot_nt(k, sel[e][t]) + slopes[e] * kpos, states[e][t], vts(j, e))
                           for t in range(2)) for e in range(2))

    states = tuple(tuple(_init_state(tq, DIFF_V_DIM) for _ in range(2)) for _ in range(2))
    if blocked:
        states = lax.fori_loop(0, i, off_step, states)
        k0 = pl.multiple_of(i * tq, tq)
        jd = i
    else:
        k0 = 0
        jd = 0
    k = k_ref[0, pl.ds(k0, tkd), :]
    kk, qq = _key_query_iota(tkd, tq)
    mask = (kk // CHUNK) <= (qq // CHUNK) + qoff // CHUNK
    dist = (qq - jnp.abs(qq + qoff - kk)).astype(F32)

    lq = lam_ref[...]
    lam = (jnp.exp(jnp.sum(lq[0:1] * lq[1:2], axis=-1, keepdims=True))
           - jnp.exp(jnp.sum(lq[2:3] * lq[3:4], axis=-1, keepdims=True)) + lam_init)
    outs = []
    for e in range(2):
        res = []
        for t in range(2):
            s = jnp.where(mask, _dot_nt(k, sel[e][t]) + slopes[e] * dist, NEG_INF)
            _, l, acc = _online_update(s, states[e][t], vts(jd, e))
            res.append(acc / l)
        o = res[0] - lam * res[1]
        ms = jnp.mean(o * o, axis=0, keepdims=True)
        outs.append(o * lax.rsqrt(ms + RMS_EPS))
    _store_pair(o_ref, outs[0], outs[1], subln_ref[...] * (1.0 - lam_init))


def _attn_call(kernel, name, q, q_cb, k, k_cb, vt, extras, extra_specs, *, npairs, qw, tq, tkd, qoff, blocked, **kw):
    b, t_q = q.shape[0], q.shape[1]
    t_k = k.shape[1]
    in_specs = [
        pl.BlockSpec((1, tq, qw), lambda bb, hp, i: (bb, i, q_cb + hp)),
        pl.BlockSpec((1, t_k, qw), lambda bb, hp, i: (bb, 0, k_cb + hp)),
        pl.BlockSpec((1, 1) + vt.shape[2:], lambda bb, hp, i: (bb, hp, 0, 0, 0)),
    ] + extra_specs
    return pl.pallas_call(
        functools.partial(kernel, tq=tq, tkd=tkd, qoff=qoff, blocked=blocked, **kw),
        grid=(b, npairs, t_q // tq), in_specs=in_specs,
        out_specs=pl.BlockSpec((1, tq, LANE), lambda bb, hp, i: (bb, i, hp)),
        out_shape=jax.ShapeDtypeStruct((b, t_q, npairs * LANE), BF16),
        compiler_params=_cparams(("parallel", "parallel", "arbitrary")), name=name,
    )(q, k, vt, *extras)


def _values_t(v, tkb):
    b, t, w = v.shape
    return v.reshape(b, t // tkb, tkb, w // LANE, LANE).transpose(0, 3, 1, 4, 2)


def _wout_kernel(oa_ref, ob_ref, oc_ref, h_ref, w_ref, g_ref, o_ref):
    y = (_dot(oa_ref[...], w_ref[0:FOX_WIDTH, :])
         + _dot(ob_ref[...], w_ref[FOX_WIDTH:FOX_WIDTH + MLA_WIDTH, :])
         + _dot(oc_ref[...], w_ref[FOX_WIDTH + MLA_WIDTH:, :]))
    o_ref[...] = h_ref[...] + _rms(y, g_ref[...])


def _wout_call(oa, ob, oc, h, w_out, g, tm):
    n = h.shape[0]
    row = lambda i: (i, 0)
    full = lambda i: (0, 0)
    return pl.pallas_call(
        _wout_kernel, grid=(n // tm,),
        in_specs=[pl.BlockSpec((tm, FOX_WIDTH), row), pl.BlockSpec((tm, MLA_WIDTH), row),
                  pl.BlockSpec((tm, DIFF_WIDTH), row), pl.BlockSpec((tm, D_MODEL), row),
                  pl.BlockSpec((D_MODEL, D_MODEL), full), pl.BlockSpec((1, D_MODEL), full)],
        out_specs=pl.BlockSpec((tm, D_MODEL), row),
        out_shape=jax.ShapeDtypeStruct((n, D_MODEL), F32),
        compiler_params=_cparams(("parallel",)), name="wout",
    )(oa, ob, oc, h, w_out, g)


def _ffn_kernel(h_ref, gpre_ref, wg_ref, wv_ref, cw_ref, cb_ref, wd_ref, gpost_ref, left_ref,
                o_ref, st_ref, xn_sc, acc_sc, carry_sc, *, tm, tf, nsb, seq_blocks):
    i = pl.program_id(0)
    f = pl.program_id(1)
    nf = pl.num_programs(1)
    tb = tm // nsb

    @pl.when(f == 0)
    def _():
        xn_sc[...] = _rms(h_ref[...], gpre_ref[...]).astype(BF16)
        acc_sc[...] = jnp.zeros_like(acc_sc)

    xn = xn_sc[...]
    gate = _dot(xn, wg_ref[...])
    val = _dot(xn, wv_ref[...])

    left = left_ref[...]
    if seq_blocks > 1:
        left = jnp.where(i % seq_blocks == 0, left, carry_sc[f, 0:CONV_WIDTH - 1, :][None])
        carry_sc[f, 0:CONV_WIDTH - 1, :] = gate[tm - (CONV_WIDTH - 1):, :]
    st_ref[f, pl.ds((i // seq_blocks) * nsb, nsb)] = gate.reshape(nsb, tb, tf)[:, tb - (CONV_WIDTH - 1):, :]

    def spread(rows):
        return jnp.broadcast_to(rows, (nsb, tb, tf)).reshape(tm, tf)

    rin = lax.broadcasted_iota(jnp.int32, (tm, 1), 0) & (tb - 1)
    l0 = spread(left[:, 0:1, :])
    l1 = spread(left[:, 1:2, :])
    g1 = jnp.where(rin == 0, l1, pltpu.roll(gate, 1, 0))
    g2 = jnp.where(rin == 0, l0, jnp.where(rin == 1, l1, pltpu.roll(gate, 2, 0)))
    cw = cw_ref[...]
    conv = cw[0:1] * g2 + cw[1:2] * g1 + cw[2:3] * gate + cb_ref[...]
    gelu = 0.5 * conv * (1.0 + jnp.tanh(math.sqrt(2.0 / math.pi) * (conv + 0.044715 * (conv * conv * conv))))
    acc_sc[...] += _dot((gelu * val).astype(BF16), wd_ref[...])

    @pl.when(f == nf - 1)
    def _():
        o_ref[...] = h_ref[...] + _rms(acc_sc[...], gpost_ref[...])


def _ffn_call(h, lw, left, seq_len, tm, tf):
    n = h.shape[0]
    nseq = left.shape[0]
    nf = D_FF // tf
    if seq_len >= tm:
        nsb, seq_blocks = 1, seq_len // tm
    else:
        nsb, seq_blocks = tm // seq_len, 1
    assert (tm // nsb) & (tm // nsb - 1) == 0
    row = lambda i, f: (i, 0)
    full = lambda i, f: (0, 0)
    out, state = pl.pallas_call(
        functools.partial(_ffn_kernel, tm=tm, tf=tf, nsb=nsb, seq_blocks=seq_blocks),
        grid=(n // tm, nf),
        in_specs=[pl.BlockSpec((tm, D_MODEL), row), pl.BlockSpec((1, D_MODEL), full),
                  pl.BlockSpec((D_MODEL, tf), lambda i, f: (0, f)),
                  pl.BlockSpec((D_MODEL, tf), lambda i, f: (0, nf + f)),
                  pl.BlockSpec((CONV_WIDTH, tf), lambda i, f: (0, f)),
                  pl.BlockSpec((1, tf), lambda i, f: (0, f)),
                  pl.BlockSpec((tf, D_MODEL), lambda i, f: (f, 0)),
                  pl.BlockSpec((1, D_MODEL), full),
                  pl.BlockSpec((nsb, CONV_WIDTH - 1, tf), lambda i, f: (i // seq_blocks, 0, f))],
        out_specs=[pl.BlockSpec((tm, D_MODEL), row),
                   pl.BlockSpec((nf, nseq, CONV_WIDTH - 1, tf), lambda i, f: (0, 0, 0, 0))],
        out_shape=[jax.ShapeDtypeStruct((n, D_MODEL), F32),
                   jax.ShapeDtypeStruct((nf, nseq, CONV_WIDTH - 1, tf), F32)],
        scratch_shapes=[pltpu.VMEM((tm, D_MODEL), BF16), pltpu.VMEM((tm, D_MODEL), F32),
                        pltpu.VMEM((nf, 8, tf), F32)],
        compiler_params=_cparams(("arbitrary", "arbitrary")), name="ffn",
    )(h, lw["g_ffn_pre"], lw["w_up"], lw["w_up"], lw["conv_w"], lw["conv_b"], lw["w_down"], lw["g_ffn_post"], left)
    return out, state.transpose(1, 2, 0, 3).reshape(nseq, CONV_WIDTH - 1, D_FF)


def _ple_kernel(h_ref, p_ref, gpre_ref, wg_ref, wp_ref, gpost_ref, o_ref):
    h = h_ref[...]
    gate = jax.nn.sigmoid(_dot(_rms(h, gpre_ref[...]).astype(BF16), wg_ref[...]))
    proj = _dot(p_ref[...].astype(BF16), wp_ref[...])
    o_ref[...] = h + _rms(proj * gate, gpost_ref[...])


def _ple_call(h, p, lw, tm):
    n = h.shape[0]
    row = lambda i: (i, 0)
    full = lambda i: (0, 0)
    return pl.pallas_call(
        _ple_kernel, grid=(n // tm,),
        in_specs=[pl.BlockSpec((tm, D_MODEL), row), pl.BlockSpec((tm, PLE_DIM), row),
                  pl.BlockSpec((1, D_MODEL), full), pl.BlockSpec((D_MODEL, D_MODEL), full),
                  pl.BlockSpec((PLE_DIM, D_MODEL), full), pl.BlockSpec((1, D_MODEL), full)],
        out_specs=pl.BlockSpec((tm, D_MODEL), row),
        out_shape=jax.ShapeDtypeStruct((n, D_MODEL), F32),
        compiler_params=_cparams(("parallel",)), name="ple",
    )(h, p, lw["g_ple_pre"], lw["w_ple_gate"], lw["w_ple_proj"], lw["g_ple_post"])


def _swap_halves(w):
    half = MLA_ROPE_DIM // 2
    return jnp.concatenate([w[..., half:], w[..., :half]], axis=-1)


def _pack_layer(w_in, b_forget, mla_q_norm, w_mla_uq, mla_kv_norm, w_mla_uk, w_mla_uv, lams, diff_subln, w_out,
                norm_mix_pre, norm_mix_post, norm_ffn_pre, norm_ffn_post, norm_ple_pre, norm_ple_post,
                w_ffn_up, ffn_conv_w, ffn_conv_b, w_ffn_down, w_ple_gate, w_ple_proj):
    zeros = lambda r, c: jnp.zeros((r, c), F32)
    w_kr = w_in[:, OFF_MLA_KR:OFF_DIFF_Q]
    w_all = jnp.concatenate([
        w_in[:, OFF_FOX_Q:OFF_FOX_F], w_in[:, OFF_DIFF_Q:IN_WIDTH],
        w_in[:, OFF_MLA_CQ:OFF_MLA_CKV], w_in[:, OFF_MLA_CKV:OFF_MLA_KR],
        w_kr, w_kr, zeros(D_MODEL, LANE - 2 * MLA_ROPE_DIM),
        _swap_halves(w_kr), _swap_halves(w_kr), zeros(D_MODEL, LANE - 2 * MLA_ROPE_DIM),
        w_in[:, OFF_FOX_F:OFF_MLA_CQ], zeros(D_MODEL, LANE - N_FOX_HEADS)], axis=1).astype(BF16)
    wq = w_mla_uq.reshape(MLA_Q_RANK, N_MLA_HEADS, MLA_NOPE_DIM + MLA_ROPE_DIM)
    plain, swapped = [], []
    pad = zeros(MLA_Q_RANK, MLA_PAIR_W - 2 * (MLA_NOPE_DIM + MLA_ROPE_DIM))
    for p in range(N_MLA_PAIRS):
        a, b = 2 * p, 2 * p + 1
        plain += [wq[:, a, :MLA_NOPE_DIM], wq[:, b, :MLA_NOPE_DIM], wq[:, a, MLA_NOPE_DIM:], wq[:, b, MLA_NOPE_DIM:], pad]
        swapped += [zeros(MLA_Q_RANK, 2 * MLA_NOPE_DIM), _swap_halves(wq[:, a, MLA_NOPE_DIM:]),
                    _swap_halves(wq[:, b, MLA_NOPE_DIM:]), pad]
    return dict(
        w_all=w_all,
        b_f=jnp.pad(b_forget, (0, LANE - N_FOX_HEADS)).reshape(1, LANE),
        g_q=mla_q_norm.reshape(1, -1), g_kv=mla_kv_norm.reshape(1, -1),
        w_uq2=jnp.concatenate(plain + swapped, axis=1).astype(BF16),
        w_ukv=jnp.concatenate([w_mla_uk, w_mla_uv], axis=1).astype(BF16),
        lams=lams, subln=jnp.tile(diff_subln, 2).reshape(1, LANE),
        w_out=w_out.astype(BF16),
        g_mix_pre=norm_mix_pre.reshape(1, -1), g_mix_post=norm_mix_post.reshape(1, -1),
        g_ffn_pre=norm_ffn_pre.reshape(1, -1), g_ffn_post=norm_ffn_post.reshape(1, -1),
        g_ple_pre=norm_ple_pre.reshape(1, -1), g_ple_post=norm_ple_post.reshape(1, -1),
        w_up=w_ffn_up.astype(BF16), conv_w=ffn_conv_w, conv_b=ffn_conv_b.reshape(1, -1),
        w_down=w_ffn_down.astype(BF16), w_ple_gate=w_ple_gate.astype(BF16), w_ple_proj=w_ple_proj.astype(BF16))


def _rope_tables(pos):
    half = MLA_ROPE_DIM // 2
    inv_freq = ROPE_THETA ** (-jnp.arange(half, dtype=F32) / half)
    ang = pos.astype(F32)[:, None] * inv_freq[None, :]
    cos, sin = jnp.cos(ang), jnp.sin(ang)
    t = pos.shape[0]
    pad = jnp.zeros((t, MLA_PAIR_W - LANE - 2 * MLA_ROPE_DIM), F32)
    cos_t = jnp.concatenate([jnp.ones((t, LANE), F32), cos, cos, cos, cos, pad], axis=1)
    sin_t = jnp.concatenate([jnp.zeros((t, LANE), F32), -sin, sin, -sin, sin, pad], axis=1)
    return cos_t, sin_t


def _alibi_slopes():
    s = 2.0 ** (-8.0 * np.arange(1, N_DIFF_HEADS + 1) / N_DIFF_HEADS)
    return jnp.asarray(np.broadcast_to(s.reshape(N_DIFF_HEADS // 2, 2, 1), (N_DIFF_HEADS // 2, 2, LANE)), dtype=F32)


def _tile_rows(n, pref):
    t = min(n, pref)
    assert n % t == 0
    return t


def _layer(h, p, lw, cos, sin, cache, conv_left, lam_init, *, batch, seq, past, tq):
    n = h.shape[0]
    tm = _tile_rows(n, 512)
    (fox_k, fox_v, logf, ckv, krope, diff_k, diff_v,
     fox_bf, diff_bf, mla_q, mla_k, mla_v) = _proj_call(h, lw, cos, sin, tm)
    rows = (fox_k, fox_v, logf, ckv, krope, diff_k, diff_v)
    b3 = lambda a: a.reshape(batch, seq, a.shape[-1])
    diff_extras = [_alibi_slopes(), lw["lams"], lw["subln"]]
    diff_specs = [pl.BlockSpec((1, 2, LANE), lambda bb, hp, i: (hp, 0, 0)),
                  pl.BlockSpec((4, DIFF_QK_DIM), lambda bb, hp, i: (0, 0)),
                  pl.BlockSpec((1, LANE), lambda bb, hp, i: (0, 0))]
    fb, db = b3(fox_bf), b3(diff_bf)
    if cache is None:
        cb = _cumsum_call(b3(logf))
        cb_spec = pl.BlockSpec((1, 2, seq, LANE), lambda bb, hp, i: (bb, hp, 0, 0))
        common = dict(tq=tq, tkd=tq, qoff=0, blocked=True)
        o_a = _attn_call(_fox_kernel, "fox_attn", fb, 0, fb, 3, _values_t(fb[..., 2 * FOX_WIDTH:], tq),
                         [cb], [cb_spec], npairs=N_FOX_HEADS // 2, qw=LANE, **common)
        o_b = _attn_call(_mla_kernel, "mla_attn", b3(mla_q), 0, b3(mla_k), 0, _values_t(b3(mla_v), tq), [], [],
                         npairs=N_MLA_PAIRS, qw=MLA_PAIR_W, **common)
        o_c = _attn_call(_diff_kernel, "diff_attn", db, 0, db, 2, _values_t(db[..., 2 * DIFF_WIDTH:], tq),
                         diff_extras, diff_specs, npairs=N_DIFF_HEADS // 2, qw=LANE, lam_init=lam_init, **common)
    else:
        c_fox_k, c_fox_v, c_logf, c_ckv, c_krope, c_diff_k, c_diff_v = cache
        tk = -(-(past + seq) // LANE) * LANE
        padk = lambda a: jnp.pad(a, ((0, 0), (0, tk - past - seq), (0, 0)))
        cat = lambda old, new: padk(jnp.concatenate([old.astype(new.dtype), new], axis=1))
        flat = lambda a: a.reshape(batch, past, -1)
        padq = lambda a: jnp.pad(a, ((0, 0), (0, tq - seq), (0, 0)))
        kf = cat(flat(c_fox_k), fb[..., FOX_WIDTH:2 * FOX_WIDTH])
        vf = cat(flat(c_fox_v), fb[..., 2 * FOX_WIDTH:])
        kd = cat(flat(c_diff_k), db[..., DIFF_WIDTH:2 * DIFF_WIDTH])
        vd = cat(flat(c_diff_v), db[..., 2 * DIFF_WIDTH:])
        cb = _cumsum_call(cat(c_logf, b3(logf)))
        kn_c, v_c = _kvup_call(c_ckv.reshape(batch * past, MLA_KV_RANK), lw["w_ukv"], _tile_rows(batch * past, 1024))
        kr_c = c_krope.astype(BF16)
        krp_c = jnp.concatenate([kr_c, kr_c, jnp.zeros((batch, past, LANE - 2 * MLA_ROPE_DIM), BF16)], axis=-1)
        km_c = jnp.concatenate([kn_c.reshape(batch, past, N_MLA_PAIRS, LANE),
                                jnp.broadcast_to(krp_c[:, :, None, :], (batch, past, N_MLA_PAIRS, LANE))],
                               axis=-1).reshape(batch, past, MLA_QK_W)
        km = cat(km_c, b3(mla_k))
        vm = cat(v_c.reshape(batch, past, MLA_WIDTH), b3(mla_v))
        cb_spec = pl.BlockSpec((1, 2, tk, LANE), lambda bb, hp, i: (bb, hp, 0, 0))
        common = dict(tq=tq, tkd=tk, qoff=past, blocked=False)
        o_a = _attn_call(_fox_kernel, "fox_attn_s", padq(fb[..., :FOX_WIDTH]), 0, kf, 0, _values_t(vf, tk),
                         [cb], [cb_spec], npairs=N_FOX_HEADS // 2, qw=LANE, **common)
        o_b = _attn_call(_mla_kernel, "mla_attn_s", padq(b3(mla_q)), 0, km, 0, _values_t(vm, tk), [], [],
                         npairs=N_MLA_PAIRS, qw=MLA_PAIR_W, **common)
        o_c = _attn_call(_diff_kernel, "diff_attn_s", padq(db[..., :DIFF_WIDTH]), 0, kd, 0, _values_t(vd, tk),
                         diff_extras, diff_specs, npairs=N_DIFF_HEADS // 2, qw=LANE, lam_init=lam_init, **common)
        o_a, o_b, o_c = o_a[:, :seq], o_b[:, :seq], o_c[:, :seq]
    f2 = lambda a: a.reshape(n, a.shape[-1])
    h = _wout_call(f2(o_a), f2(o_b), f2(o_c), h, lw["w_out"], lw["g_mix_post"], tm)
    h, conv_state = _ffn_call(h, lw, conv_left, seq, tm, 512)
    h = _ple_call(h, p, lw, tm)
    return h, rows, conv_state


def kernel(x_prompt, x_sample, cache_fox_k, cache_fox_v, cache_fox_logf, cache_mla_ckv, cache_mla_krope, cache_diff_k, cache_diff_v, state_ffn_conv, p_prompt, p_sample, w_in, b_forget, mla_q_norm, w_mla_uq, mla_kv_norm, w_mla_uk, w_mla_uv, diff_lambda_q1, diff_lambda_k1, diff_lambda_q2, diff_lambda_k2, diff_subln, w_out, norm_mix_pre, norm_mix_post, norm_ffn_pre, norm_ffn_post, norm_ple_pre, norm_ple_post, w_ffn_up, ffn_conv_w, ffn_conv_b, w_ffn_down, w_ple_gate, w_ple_proj):
    bp, sp, _ = x_prompt.shape
    bs, ts, _ = x_sample.shape
    depth, _, past = cache_fox_k.shape[:3]
    assert past % CHUNK == 0 and ts <= CHUNK and sp % LANE == 0
    tq = _tile_rows(sp, 512)
    tms = _tile_rows(bs * ts, 512)

    cos_p, sin_p = _rope_tables(jnp.arange(sp))
    cos_s, sin_s = _rope_tables(past + jnp.arange(ts))
    cos_s, sin_s = jnp.tile(cos_s, (tms // ts, 1)), jnp.tile(sin_s, (tms // ts, 1))

    hp = x_prompt.reshape(bp * sp, D_MODEL)
    hs = x_sample.reshape(bs * ts, D_MODEL)
    rows_p, rows_s, conv_p, conv_s = [], [], [], []
    for l in range(depth):
        lams = jnp.stack([diff_lambda_q1[l], diff_lambda_k1[l], diff_lambda_q2[l], diff_lambda_k2[l]], axis=0)
        lw = _pack_layer(w_in[l], b_forget[l], mla_q_norm[l], w_mla_uq[l], mla_kv_norm[l], w_mla_uk[l], w_mla_uv[l],
                         lams, diff_subln[l], w_out[l], norm_mix_pre[l], norm_mix_post[l], norm_ffn_pre[l],
                         norm_ffn_post[l], norm_ple_pre[l], norm_ple_post[l], w_ffn_up[l], ffn_conv_w[l],
                         ffn_conv_b[l], w_ffn_down[l], w_ple_gate[l], w_ple_proj[l])
        lam_init = 0.8 - 0.6 * math.exp(-0.3 * l)
        hp, rp, cp = _layer(hp, p_prompt[l].reshape(bp * sp, PLE_DIM), lw, cos_p, sin_p, None,
                            jnp.zeros((bp, CONV_WIDTH - 1, D_FF), F32), lam_init,
                            batch=bp, seq=sp, past=0, tq=tq)
        cache_l = (cache_fox_k[l], cache_fox_v[l], cache_fox_logf[l], cache_mla_ckv[l], cache_mla_krope[l],
                   cache_diff_k[l], cache_diff_v[l])
        hs, rs, cs = _layer(hs, p_sample[l].reshape(bs * ts, PLE_DIM), lw, cos_s, sin_s, cache_l,
                            state_ffn_conv[l], lam_init, batch=bs, seq=ts, past=past, tq=LANE)
        rows_p.append(rp)
        rows_s.append(rs)
        conv_p.append(cp)
        conv_s.append(cs)

    def stack_rows(per_layer, batch, seq):
        fk, fv, lf, ckv, kr, dk, dv = (jnp.stack(a, axis=0) for a in zip(*per_layer))
        lead = (depth, batch, seq)
        return (fk.reshape(lead + (N_FOX_HEADS, HEAD_DIM)), fv.reshape(lead + (N_FOX_HEADS, HEAD_DIM)),
                lf.reshape(lead + (N_FOX_HEADS,)), ckv.reshape(lead + (MLA_KV_RANK,)),
                kr.reshape(lead + (MLA_ROPE_DIM,)), dk.reshape(lead + (N_DIFF_HEADS, 2 * DIFF_QK_DIM)),
                dv.reshape(lead + (N_DIFF_HEADS, DIFF_V_DIM)))

    out_p = stack_rows(rows_p, bp, sp)
    out_s = stack_rows(rows_s, bs, ts)
    return ((hp.reshape(bp, sp, D_MODEL), hs.reshape(bs, ts, D_MODEL)) + out_p + (jnp.stack(conv_p, axis=0),)
            + out_s + (jnp.stack(conv_s, axis=0),))
```

```python
import functools
import math

import jax
import jax.numpy as jnp
import numpy as np
from jax import lax
from jax.experimental import pallas as pl
from jax.experimental.pallas import tpu as pltpu

F32 = jnp.float32
BF16 = jnp.bfloat16

D_MODEL = 1024
HEAD_DIM = 64
N_FOX_HEADS = 6
N_MLA_HEADS = 6
N_DIFF_HEADS = 4
MLA_Q_RANK = 384
MLA_KV_RANK = 256
MLA_NOPE_DIM = 64
MLA_ROPE_DIM = 32
MLA_V_DIM = 64
DIFF_QK_DIM = 32
DIFF_V_DIM = 64
FOX_WIDTH = N_FOX_HEADS * HEAD_DIM
MLA_WIDTH = N_MLA_HEADS * MLA_V_DIM
DIFF_WIDTH = N_DIFF_HEADS * DIFF_V_DIM
D_FF = 4 * D_MODEL
CONV_WIDTH = 3
PLE_DIM = 256
CHUNK = 64
ROPE_THETA = 10000.0
RMS_EPS = 1e-6
NEG_INF = -1e30
LOG2E = math.log2(math.e)

OFF_FOX_Q = 0
OFF_FOX_F = 3 * FOX_WIDTH
OFF_MLA_CQ = OFF_FOX_F + N_FOX_HEADS
OFF_MLA_CKV = OFF_MLA_CQ + MLA_Q_RANK
OFF_MLA_KR = OFF_MLA_CKV + MLA_KV_RANK
OFF_DIFF_Q = OFF_MLA_KR + MLA_ROPE_DIM
IN_WIDTH = OFF_DIFF_Q + 2 * N_DIFF_HEADS * 2 * DIFF_QK_DIM + DIFF_WIDTH

LANE = 128
V7X_VMEM_BYTES = 64 * 1024 * 1024
VMEM_LIMIT = (V7X_VMEM_BYTES * 7) // 8

C_FOX = 0
C_DIFF = C_FOX + 3 * FOX_WIDTH
C_CQ = C_DIFF + 3 * DIFF_WIDTH
C_CKV = C_CQ + MLA_Q_RANK
C_KR = C_CKV + MLA_KV_RANK
C_F = C_KR + 2 * LANE
W_ALL = C_F + LANE
N_MLA_PAIRS = N_MLA_HEADS // 2
MLA_PAIR_W = 2 * LANE
MLA_QK_W = N_MLA_PAIRS * MLA_PAIR_W

FOX_QSCALE = HEAD_DIM ** -0.5 * LOG2E
MLA_QSCALE = (MLA_NOPE_DIM + MLA_ROPE_DIM) ** -0.5 * LOG2E
DIFF_QSCALE = DIFF_QK_DIM ** -0.5 * LOG2E


def _cparams(sem):
    return pltpu.CompilerParams(dimension_semantics=sem, vmem_limit_bytes=VMEM_LIMIT)


def _rms(x, g):
    return x * lax.rsqrt(jnp.mean(x * x, axis=-1, keepdims=True) + RMS_EPS) * g


def _dot(a, b):
    return jnp.dot(a, b, preferred_element_type=F32)


def _dot_nt(a, b):
    return lax.dot_general(a, b, (((1,), (1,)), ((), ())), preferred_element_type=F32)


def _proj_kernel(h_ref, gpre_ref, w_ref, bf_ref, gq_ref, gkv_ref, wuq_ref, wukv_ref, cos_ref, sin_ref,
                 foxk_ref, foxv_ref, logf_ref, ckv_ref, krope_ref, diffk_ref, diffv_ref,
                 foxbf_ref, diffbf_ref, mlaq_ref, mlak_ref, mlav_ref):
    xn = _rms(h_ref[...], gpre_ref[...]).astype(BF16)
    cos = cos_ref[...]
    sin = sin_ref[...]

    z = _dot(xn, w_ref[:, C_FOX:C_FOX + 3 * FOX_WIDTH])
    foxbf_ref[:, :FOX_WIDTH] = (z[:, :FOX_WIDTH] * FOX_QSCALE).astype(BF16)
    foxbf_ref[:, FOX_WIDTH:] = z[:, FOX_WIDTH:].astype(BF16)
    foxk_ref[...] = z[:, FOX_WIDTH:2 * FOX_WIDTH]
    foxv_ref[...] = z[:, 2 * FOX_WIDTH:3 * FOX_WIDTH]

    z = _dot(xn, w_ref[:, C_DIFF:C_DIFF + 3 * DIFF_WIDTH])
    diffbf_ref[:, :DIFF_WIDTH] = (z[:, :DIFF_WIDTH] * DIFF_QSCALE).astype(BF16)
    diffbf_ref[:, DIFF_WIDTH:] = z[:, DIFF_WIDTH:].astype(BF16)
    diffk_ref[...] = z[:, DIFF_WIDTH:2 * DIFF_WIDTH]
    diffv_ref[...] = z[:, 2 * DIFF_WIDTH:3 * DIFF_WIDTH]

    cq = _rms(_dot(xn, w_ref[:, C_CQ:C_CQ + MLA_Q_RANK]), gq_ref[...]).astype(BF16)
    q2 = _dot(cq, wuq_ref[...])
    for p in range(N_MLA_PAIRS):
        lo = p * MLA_PAIR_W
        roped = q2[:, lo:lo + MLA_PAIR_W] * cos + q2[:, MLA_QK_W + lo:MLA_QK_W + lo + MLA_PAIR_W] * sin
        mlaq_ref[:, lo:lo + MLA_PAIR_W] = (roped * MLA_QSCALE).astype(BF16)

    ckv = _rms(_dot(xn, w_ref[:, C_CKV:C_CKV + MLA_KV_RANK]), gkv_ref[...])
    ckv_ref[...] = ckv
    kv = _dot(ckv.astype(BF16), wukv_ref[...])
    mlav_ref[...] = kv[:, MLA_WIDTH:2 * MLA_WIDTH].astype(BF16)
    zkr = _dot(xn, w_ref[:, C_KR:C_KR + 2 * LANE])
    krp = zkr[:, :LANE] * cos[:, LANE:] + zkr[:, LANE:] * sin[:, LANE:]
    krope_ref[...] = krp[:, :MLA_ROPE_DIM]
    krp16 = krp.astype(BF16)
    for p in range(N_MLA_PAIRS):
        lo = p * MLA_PAIR_W
        mlak_ref[:, lo:lo + LANE] = kv[:, p * LANE:(p + 1) * LANE].astype(BF16)
        mlak_ref[:, lo + LANE:lo + 2 * LANE] = krp16

    zf = _dot(xn, w_ref[:, C_F:C_F + LANE]) + bf_ref[...]
    logf = -(jnp.maximum(-zf, 0.0) + jnp.log1p(jnp.exp(-jnp.abs(zf))))
    logf_ref[...] = logf[:, :N_FOX_HEADS]


def _proj_call(h, lw, cos, sin, tm):
    n = h.shape[0]
    ntab = cos.shape[0] // tm
    row = lambda i: (i, 0)
    full = lambda i: (0, 0)
    tab = lambda i: (i % ntab, 0)
    in_specs = [
        pl.BlockSpec((tm, D_MODEL), row),
        pl.BlockSpec((1, D_MODEL), full),
        pl.BlockSpec((D_MODEL, W_ALL), full),
        pl.BlockSpec((1, LANE), full),
        pl.BlockSpec((1, MLA_Q_RANK), full),
        pl.BlockSpec((1, MLA_KV_RANK), full),
        pl.BlockSpec((MLA_Q_RANK, 2 * MLA_QK_W), full),
        pl.BlockSpec((MLA_KV_RANK, 2 * MLA_WIDTH), full),
        pl.BlockSpec((tm, MLA_PAIR_W), tab),
        pl.BlockSpec((tm, MLA_PAIR_W), tab),
    ]
    widths = [(FOX_WIDTH, F32), (FOX_WIDTH, F32), (N_FOX_HEADS, F32), (MLA_KV_RANK, F32), (MLA_ROPE_DIM, F32),
              (DIFF_WIDTH, F32), (DIFF_WIDTH, F32),
              (3 * FOX_WIDTH, BF16), (3 * DIFF_WIDTH, BF16), (MLA_QK_W, BF16), (MLA_QK_W, BF16), (MLA_WIDTH, BF16)]
    out_shape = [jax.ShapeDtypeStruct((n, w), dt) for w, dt in widths]
    out_specs = [pl.BlockSpec((tm, w), row) for w, _ in widths]
    return pl.pallas_call(
        _proj_kernel, grid=(n // tm,), in_specs=in_specs, out_specs=out_specs, out_shape=out_shape,
        compiler_params=_cparams(("parallel",)), name="proj",
    )(h, lw["g_mix_pre"], lw["w_all"], lw["b_f"], lw["g_q"], lw["g_kv"], lw["w_uq2"], lw["w_ukv"], cos, sin)


def _kvup_kernel(ckv_ref, wukv_ref, kn_ref, v_ref):
    kv = _dot(ckv_ref[...].astype(BF16), wukv_ref[...])
    kn_ref[...] = kv[:, :MLA_WIDTH].astype(BF16)
    v_ref[...] = kv[:, MLA_WIDTH:].astype(BF16)


def _kvup_call(ckv, w_ukv, tm):
    n = ckv.shape[0]
    row = lambda i: (i, 0)
    return pl.pallas_call(
        _kvup_kernel, grid=(n // tm,),
        in_specs=[pl.BlockSpec((tm, MLA_KV_RANK), row), pl.BlockSpec((MLA_KV_RANK, 2 * MLA_WIDTH), lambda i: (0, 0))],
        out_specs=[pl.BlockSpec((tm, MLA_WIDTH), row), pl.BlockSpec((tm, MLA_WIDTH), row)],
        out_shape=[jax.ShapeDtypeStruct((n, MLA_WIDTH), BF16)] * 2,
        compiler_params=_cparams(("parallel",)), name="kvup",
    )(ckv, w_ukv)


def _cumsum_kernel(x_ref, o_ref, carry_sc, *, nch):
    lane = lax.broadcasted_iota(jnp.int32, (8, LANE), 1)

    @pl.when(pl.program_id(1) == 0)
    def _():
        carry_sc[...] = jnp.zeros_like(carry_sc)

    for j in range(nch):
        x = x_ref[0, j]
        d = 1
        while d < LANE:
            x = x + jnp.where(lane >= d, pltpu.roll(x, d, 1), 0.0)
            d *= 2
        c = x + carry_sc[...]
        carry_sc[...] = jnp.broadcast_to(c[:, LANE - 1:LANE], (8, LANE))
        for h in range(N_FOX_HEADS):
            o_ref[0, h, j * LANE:(j + 1) * LANE, :] = jnp.broadcast_to(c[h:h + 1, :], (LANE, LANE)).T * (-LOG2E)


def _cumsum_call(logf):
    b, t, hh = logf.shape
    nch = t // LANE
    per_step = math.gcd(nch, 8)
    x = jnp.pad(logf, ((0, 0), (0, 0), (0, 8 - hh))).reshape(b, nch, LANE, 8).transpose(0, 1, 3, 2)
    return pl.pallas_call(
        functools.partial(_cumsum_kernel, nch=per_step), grid=(b, nch // per_step),
        in_specs=[pl.BlockSpec((1, per_step, 8, LANE), lambda i, j: (i, j, 0, 0))],
        out_specs=pl.BlockSpec((1, N_FOX_HEADS, per_step * LANE, LANE), lambda i, j: (i, 0, j, 0)),
        out_shape=jax.ShapeDtypeStruct((b, N_FOX_HEADS, t, LANE), F32),
        scratch_shapes=[pltpu.VMEM((8, LANE), F32)],
        compiler_params=_cparams(("parallel", "arbitrary")), name="cumsum",
    )(x)


ROW_CHUNK = 32
assert CHUNK % ROW_CHUNK == 0


def _softmax_block(s_ref, p_ref, m_ref, l_ref, acc_ref, c, vt, logits_fn, tk, tq):
    nrc = tk // ROW_CHUNK
    sub = ROW_CHUNK // 8
    rows = lambda r: slice(r * ROW_CHUNK, (r + 1) * ROW_CHUNK)
    mrun = jnp.full((8, tq), NEG_INF, F32)
    for r in range(nrc):
        x = logits_fn(r * ROW_CHUNK, s_ref[c, rows(r), :])
        mrun = jnp.maximum(mrun, jnp.max(x.reshape(sub, 8, tq), axis=0))
    m_old = m_ref[c]
    m_new = jnp.maximum(m_old, jnp.max(mrun, axis=0, keepdims=True))
    alpha = jnp.exp2(m_old - m_new)
    lrun = jnp.zeros((8, tq), F32)
    for r in range(nrc):
        p = jnp.exp2(logits_fn(r * ROW_CHUNK, s_ref[c, rows(r), :]) - m_new)
        lrun = lrun + jnp.sum(p.reshape(sub, 8, tq), axis=0)
        p_ref[c, rows(r), :] = p.astype(BF16)
    m_ref[c] = m_new
    l_ref[c] = alpha * l_ref[c] + jnp.sum(lrun, axis=0, keepdims=True)
    acc_ref[c] = alpha * acc_ref[c] + _dot(vt, p_ref[c])


def _init_states(m_ref, l_ref, acc_ref):
    m_ref[...] = jnp.full(m_ref.shape, NEG_INF, F32)
    l_ref[...] = jnp.zeros(l_ref.shape, F32)
    acc_ref[...] = jnp.zeros(acc_ref.shape, F32)


def _chunk_mask_row(r0, tq, qoff):
    qq = lax.broadcasted_iota(jnp.int32, (1, tq), 1)
    return (r0 // CHUNK) <= (qq // CHUNK) + qoff // CHUNK


def _key_query_iota(tk, tq):
    return (lax.broadcasted_iota(jnp.int32, (tk, tq), 0), lax.broadcasted_iota(jnp.int32, (tk, tq), 1))


def _store_pair(o_ref, o0, o1, row_scale=None):
    out = jnp.concatenate([o0, o1], axis=0).T
    if row_scale is not None:
        out = out * row_scale
    o_ref[0] = out.astype(BF16)


def _block_start(j, size):
    return j * size if isinstance(j, int) else pl.multiple_of(j * size, size)


def _attend(i, blocked, qk_fn, step_fn, s_a, s_b):
    if not blocked:
        qk_fn(s_a, 0)
        step_fn(s_a, 0, True)
        return
    odd = i % 2

    @pl.when(odd == 1)
    def _():
        qk_fn(s_b, 0)
        step_fn(s_b, 0, False)

    qk_fn(s_a, odd)

    def pair(t, carry):
        b0 = odd + 2 * t
        qk_fn(s_b, b0 + 1)
        step_fn(s_a, b0, False)
        qk_fn(s_a, b0 + 2)
        step_fn(s_b, b0 + 1, False)
        return carry

    lax.fori_loop(0, (i - odd) // 2, pair, 0)
    step_fn(s_a, i, True)


def _fox_kernel(q_ref, k_ref, vt_ref, cb_ref, o_ref, s_a, s_b, p_sc, m_sc, l_sc, acc_sc, *, tq, tkd, qoff, blocked):
    i = pl.program_id(2)
    _init_states(m_sc, l_sc, acc_sc)
    q = q_ref[0]
    lane = lax.broadcasted_iota(jnp.int32, (1, LANE), 1)
    qs = (jnp.where(lane < HEAD_DIM, q, jnp.zeros_like(q)), jnp.where(lane >= HEAD_DIM, q, jnp.zeros_like(q)))

    def qk_fn(buf, j):
        k = k_ref[0, pl.ds(_block_start(j, tq), tkd), :]
        for e in range(2):
            buf[e] = _dot_nt(k, qs[e])

    def step_fn(buf, j, diag):
        k0 = _block_start(j, tq)
        for e in range(2):
            def logits(r0, x, e=e):
                bias = cb_ref[0, e, pl.ds(pl.multiple_of(k0 + r0, ROW_CHUNK), ROW_CHUNK), :]
                x = x + jnp.tile(bias, (1, tq // LANE))
                if diag:
                    kk, qq = _key_query_iota(ROW_CHUNK, tq)
                    x = jnp.where(kk - qq <= qoff - r0, x, NEG_INF)
                return x
            _softmax_block(buf, p_sc, m_sc, l_sc, acc_sc, e, vt_ref[0, 0, j, e * HEAD_DIM:(e + 1) * HEAD_DIM, :],
                           logits, tkd, tq)

    _attend(i, blocked, qk_fn, step_fn, s_a, s_b)
    _store_pair(o_ref, *(acc_sc[e] / l_sc[e] for e in range(2)))


def _mla_kernel(q_ref, k_ref, vt_ref, o_ref, s_a, s_b, p_sc, m_sc, l_sc, acc_sc, *, tq, tkd, qoff, blocked):
    i = pl.program_id(2)
    _init_states(m_sc, l_sc, acc_sc)
    q = q_ref[0]
    lane2 = lax.broadcasted_iota(jnp.int32, (1, MLA_PAIR_W), 1)
    sel = []
    for e in range(2):
        nope = (lane2 >= e * MLA_NOPE_DIM) & (lane2 < (e + 1) * MLA_NOPE_DIM)
        rope = (lane2 >= LANE + e * MLA_ROPE_DIM) & (lane2 < LANE + (e + 1) * MLA_ROPE_DIM)
        sel.append(jnp.where(nope | rope, q, jnp.zeros_like(q)))

    def qk_fn(buf, j):
        k = k_ref[0, pl.ds(_block_start(j, tq), tkd), :]
        for e in range(2):
            buf[e] = _dot_nt(k, sel[e])

    def step_fn(buf, j, diag):
        def logits(r0, x):
            return jnp.where(_chunk_mask_row(r0, tq, qoff), x, NEG_INF) if diag else x
        for e in range(2):
            _softmax_block(buf, p_sc, m_sc, l_sc, acc_sc, e, vt_ref[0, 0, j, e * MLA_V_DIM:(e + 1) * MLA_V_DIM, :],
                           logits, tkd, tq)

    _attend(i, blocked, qk_fn, step_fn, s_a, s_b)
    _store_pair(o_ref, *(acc_sc[e] / l_sc[e] for e in range(2)))


def _diff_kernel(q_ref, k_ref, vt_ref, slope_ref, lam_ref, subln_ref, o_ref, s_a, s_b, p_sc, m_sc, l_sc, acc_sc,
                 *, tq, tkd, qoff, blocked, lam_init):
    i = pl.program_id(2)
    _init_states(m_sc, l_sc, acc_sc)
    q = q_ref[0]
    lane = lax.broadcasted_iota(jnp.int32, (1, LANE), 1)
    sel = [jnp.where((lane >= (2 * e + t) * DIFF_QK_DIM) & (lane < (2 * e + t + 1) * DIFF_QK_DIM), q, jnp.zeros_like(q))
           for e in range(2) for t in range(2)]
    slopes = [slope_ref[0, e:e + 1, 0:1] * LOG2E for e in range(2)]
    rel = qoff + i * tq

    def qk_fn(buf, j):
        k = k_ref[0, pl.ds(_block_start(j, tq), tkd), :]
        for c in range(4):
            buf[c] = _dot_nt(k, sel[c])

    def step_fn(buf, j, diag):
        for c in range(4):
            e = c // 2

            def logits(r0, x, e=e):
                if diag:
                    kk, qq = _key_query_iota(ROW_CHUNK, tq)
                    dist = (qq - jnp.abs(qq + (qoff - r0) - kk)).astype(F32)
                    return jnp.where(_chunk_mask_row(r0, tq, qoff), x + slopes[e] * dist, NEG_INF)
                kpos = (j * tq - rel + r0 + lax.broadcasted_iota(jnp.int32, (ROW_CHUNK, LANE), 0)).astype(F32)
                return x + jnp.tile(slopes[e] * kpos, (1, tq // LANE))
            _softmax_block(buf, p_sc, m_sc, l_sc, acc_sc, c, vt_ref[0, 0, j, e * DIFF_V_DIM:(e + 1) * DIFF_V_DIM, :],
                           logits, tkd, tq)

    _attend(i, blocked, qk_fn, step_fn, s_a, s_b)
    res = [acc_sc[c] / l_sc[c] for c in range(4)]
    lq = lam_ref[...]
    lam = (jnp.exp(jnp.sum(lq[0:1] * lq[1:2], axis=-1, keepdims=True))
           - jnp.exp(jnp.sum(lq[2:3] * lq[3:4], axis=-1, keepdims=True)) + lam_init)
    outs = []
    for e in range(2):
        o = res[2 * e] - lam * res[2 * e + 1]
        outs.append(o * lax.rsqrt(jnp.mean(o * o, axis=0, keepdims=True) + RMS_EPS))
    _store_pair(o_ref, outs[0], outs[1], subln_ref[...] * (1.0 - lam_init))


def _attn_call(kernel, name, q, q_cb, k, k_cb, vt, extras, extra_specs, *, npairs, qw, tq, tkd, qoff, blocked,
               chains, **kw):
    b, t_q = q.shape[0], q.shape[1]
    t_k = k.shape[1]
    dv = LANE // 2
    scratch = [pltpu.VMEM((chains, tkd, tq), F32), pltpu.VMEM((chains, tkd, tq), F32),
               pltpu.VMEM((chains, tkd, tq), BF16), pltpu.VMEM((chains, 1, tq), F32),
               pltpu.VMEM((chains, 1, tq), F32), pltpu.VMEM((chains, dv, tq), F32)]
    in_specs = [
        pl.BlockSpec((1, tq, qw), lambda bb, hp, i: (bb, i, q_cb + hp)),
        pl.BlockSpec((1, t_k, qw), lambda bb, hp, i: (bb, 0, k_cb + hp)),
        pl.BlockSpec((1, 1) + vt.shape[2:], lambda bb, hp, i: (bb, hp, 0, 0, 0)),
    ] + extra_specs
    return pl.pallas_call(
        functools.partial(kernel, tq=tq, tkd=tkd, qoff=qoff, blocked=blocked, **kw),
        grid=(b, npairs, t_q // tq), in_specs=in_specs,
        out_specs=pl.BlockSpec((1, tq, LANE), lambda bb, hp, i: (bb, i, hp)),
        out_shape=jax.ShapeDtypeStruct((b, t_q, npairs * LANE), BF16), scratch_shapes=scratch,
        compiler_params=_cparams(("parallel", "parallel", "arbitrary")), name=name,
    )(q, k, vt, *extras)


def _values_t(v, tkb):
    b, t, w = v.shape
    return v.reshape(b, t // tkb, tkb, w // LANE, LANE).transpose(0, 3, 1, 4, 2)


def _wout_kernel(oa_ref, ob_ref, oc_ref, h_ref, w_ref, g_ref, o_ref):
    y = (_dot(oa_ref[...], w_ref[0:FOX_WIDTH, :])
         + _dot(ob_ref[...], w_ref[FOX_WIDTH:FOX_WIDTH + MLA_WIDTH, :])
         + _dot(oc_ref[...], w_ref[FOX_WIDTH + MLA_WIDTH:, :]))
    o_ref[...] = h_ref[...] + _rms(y, g_ref[...])


def _wout_call(oa, ob, oc, h, w_out, g, tm):
    n = h.shape[0]
    row = lambda i: (i, 0)
    full = lambda i: (0, 0)
    return pl.pallas_call(
        _wout_kernel, grid=(n // tm,),
        in_specs=[pl.BlockSpec((tm, FOX_WIDTH), row), pl.BlockSpec((tm, MLA_WIDTH), row),
                  pl.BlockSpec((tm, DIFF_WIDTH), row), pl.BlockSpec((tm, D_MODEL), row),
                  pl.BlockSpec((D_MODEL, D_MODEL), full), pl.BlockSpec((1, D_MODEL), full)],
        out_specs=pl.BlockSpec((tm, D_MODEL), row),
        out_shape=jax.ShapeDtypeStruct((n, D_MODEL), F32),
        compiler_params=_cparams(("parallel",)), name="wout",
    )(oa, ob, oc, h, w_out, g)


def _ffn_kernel(h_ref, gpre_ref, wg_ref, wv_ref, cw_ref, cb_ref, wd_ref, gpost_ref, left_ref,
                o_ref, st_ref, xn_sc, acc_sc, carry_sc, *, tm, tf, nsb, seq_blocks):
    i = pl.program_id(0)
    f = pl.program_id(1)
    nf = pl.num_programs(1)
    tb = tm // nsb

    @pl.when(f == 0)
    def _():
        xn_sc[...] = _rms(h_ref[...], gpre_ref[...]).astype(BF16)
        acc_sc[...] = jnp.zeros_like(acc_sc)

    xn = xn_sc[...]
    gate = _dot(xn, wg_ref[...])
    val = _dot(xn, wv_ref[...])

    left = left_ref[...]
    if seq_blocks > 1:
        left = jnp.where(i % seq_blocks == 0, left, carry_sc[f, 0:CONV_WIDTH - 1, :][None])
        carry_sc[f, 0:CONV_WIDTH - 1, :] = gate[tm - (CONV_WIDTH - 1):, :]
    st_ref[f, pl.ds((i // seq_blocks) * nsb, nsb)] = gate.reshape(nsb, tb, tf)[:, tb - (CONV_WIDTH - 1):, :]

    def spread(rows):
        return jnp.broadcast_to(rows, (nsb, tb, tf)).reshape(tm, tf)

    rin = lax.broadcasted_iota(jnp.int32, (tm, 1), 0) & (tb - 1)
    l0 = spread(left[:, 0:1, :])
    l1 = spread(left[:, 1:2, :])
    g1 = jnp.where(rin == 0, l1, pltpu.roll(gate, 1, 0))
    g2 = jnp.where(rin == 0, l0, jnp.where(rin == 1, l1, pltpu.roll(gate, 2, 0)))
    cw = cw_ref[...]
    conv = cw[0:1] * g2 + cw[1:2] * g1 + cw[2:3] * gate + cb_ref[...]
    gelu = 0.5 * conv * (1.0 + jnp.tanh(math.sqrt(2.0 / math.pi) * (conv + 0.044715 * (conv * conv * conv))))
    acc_sc[...] += _dot((gelu * val).astype(BF16), wd_ref[...])

    @pl.when(f == nf - 1)
    def _():
        o_ref[...] = h_ref[...] + _rms(acc_sc[...], gpost_ref[...])


def _ffn_call(h, lw, left, seq_len, tm, tf):
    n = h.shape[0]
    nseq = left.shape[0]
    nf = D_FF // tf
    if seq_len >= tm:
        nsb, seq_blocks = 1, seq_len // tm
    else:
        nsb, seq_blocks = tm // seq_len, 1
    assert (tm // nsb) & (tm // nsb - 1) == 0
    row = lambda i, f: (i, 0)
    full = lambda i, f: (0, 0)
    out, state = pl.pallas_call(
        functools.partial(_ffn_kernel, tm=tm, tf=tf, nsb=nsb, seq_blocks=seq_blocks),
        grid=(n // tm, nf),
        in_specs=[pl.BlockSpec((tm, D_MODEL), row), pl.BlockSpec((1, D_MODEL), full),
                  pl.BlockSpec((D_MODEL, tf), lambda i, f: (0, f)),
                  pl.BlockSpec((D_MODEL, tf), lambda i, f: (0, nf + f)),
                  pl.BlockSpec((CONV_WIDTH, tf), lambda i, f: (0, f)),
                  pl.BlockSpec((1, tf), lambda i, f: (0, f)),
                  pl.BlockSpec((tf, D_MODEL), lambda i, f: (f, 0)),
                  pl.BlockSpec((1, D_MODEL), full),
                  pl.BlockSpec((nsb, CONV_WIDTH - 1, tf), lambda i, f: (i // seq_blocks, 0, f))],
        out_specs=[pl.BlockSpec((tm, D_MODEL), row),
                   pl.BlockSpec((nf, nseq, CONV_WIDTH - 1, tf), lambda i, f: (0, 0, 0, 0))],
        out_shape=[jax.ShapeDtypeStruct((n, D_MODEL), F32),
                   jax.ShapeDtypeStruct((nf, nseq, CONV_WIDTH - 1, tf), F32)],
        scratch_shapes=[pltpu.VMEM((tm, D_MODEL), BF16), pltpu.VMEM((tm, D_MODEL), F32),
                        pltpu.VMEM((nf, 8, tf), F32)],
        compiler_params=_cparams(("arbitrary", "arbitrary")), name="ffn",
    )(h, lw["g_ffn_pre"], lw["w_up"], lw["w_up"], lw["conv_w"], lw["conv_b"], lw["w_down"], lw["g_ffn_post"], left)
    return out, state.transpose(1, 2, 0, 3).reshape(nseq, CONV_WIDTH - 1, D_FF)


def _ple_kernel(h_ref, p_ref, gpre_ref, wg_ref, wp_ref, gpost_ref, o_ref):
    h = h_ref[...]
    gate = jax.nn.sigmoid(_dot(_rms(h, gpre_ref[...]).astype(BF16), wg_ref[...]))
    proj = _dot(p_ref[...].astype(BF16), wp_ref[...])
    o_ref[...] = h + _rms(proj * gate, gpost_ref[...])


def _ple_call(h, p, lw, tm):
    n = h.shape[0]
    row = lambda i: (i, 0)
    full = lambda i: (0, 0)
    return pl.pallas_call(
        _ple_kernel, grid=(n // tm,),
        in_specs=[pl.BlockSpec((tm, D_MODEL), row), pl.BlockSpec((tm, PLE_DIM), row),
                  pl.BlockSpec((1, D_MODEL), full), pl.BlockSpec((D_MODEL, D_MODEL), full),
                  pl.BlockSpec((PLE_DIM, D_MODEL), full), pl.BlockSpec((1, D_MODEL), full)],
        out_specs=pl.BlockSpec((tm, D_MODEL), row),
        out_shape=jax.ShapeDtypeStruct((n, D_MODEL), F32),
        compiler_params=_cparams(("parallel",)), name="ple",
    )(h, p, lw["g_ple_pre"], lw["w_ple_gate"], lw["w_ple_proj"], lw["g_ple_post"])


def _swap_halves(w):
    half = MLA_ROPE_DIM // 2
    return jnp.concatenate([w[..., half:], w[..., :half]], axis=-1)


def _pack_layer(w_in, b_forget, mla_q_norm, w_mla_uq, mla_kv_norm, w_mla_uk, w_mla_uv, lams, diff_subln, w_out,
                norm_mix_pre, norm_mix_post, norm_ffn_pre, norm_ffn_post, norm_ple_pre, norm_ple_post,
                w_ffn_up, ffn_conv_w, ffn_conv_b, w_ffn_down, w_ple_gate, w_ple_proj):
    zeros = lambda r, c: jnp.zeros((r, c), F32)
    w_kr = w_in[:, OFF_MLA_KR:OFF_DIFF_Q]
    w_all = jnp.concatenate([
        w_in[:, OFF_FOX_Q:OFF_FOX_F], w_in[:, OFF_DIFF_Q:IN_WIDTH],
        w_in[:, OFF_MLA_CQ:OFF_MLA_CKV], w_in[:, OFF_MLA_CKV:OFF_MLA_KR],
        w_kr, w_kr, zeros(D_MODEL, LANE - 2 * MLA_ROPE_DIM),
        _swap_halves(w_kr), _swap_halves(w_kr), zeros(D_MODEL, LANE - 2 * MLA_ROPE_DIM),
        w_in[:, OFF_FOX_F:OFF_MLA_CQ], zeros(D_MODEL, LANE - N_FOX_HEADS)], axis=1).astype(BF16)
    wq = w_mla_uq.reshape(MLA_Q_RANK, N_MLA_HEADS, MLA_NOPE_DIM + MLA_ROPE_DIM)
    plain, swapped = [], []
    pad = zeros(MLA_Q_RANK, MLA_PAIR_W - 2 * (MLA_NOPE_DIM + MLA_ROPE_DIM))
    for p in range(N_MLA_PAIRS):
        a, b = 2 * p, 2 * p + 1
        plain += [wq[:, a, :MLA_NOPE_DIM], wq[:, b, :MLA_NOPE_DIM], wq[:, a, MLA_NOPE_DIM:], wq[:, b, MLA_NOPE_DIM:], pad]
        swapped += [zeros(MLA_Q_RANK, 2 * MLA_NOPE_DIM), _swap_halves(wq[:, a, MLA_NOPE_DIM:]),
                    _swap_halves(wq[:, b, MLA_NOPE_DIM:]), pad]
    return dict(
        w_all=w_all,
        b_f=jnp.pad(b_forget, (0, LANE - N_FOX_HEADS)).reshape(1, LANE),
        g_q=mla_q_norm.reshape(1, -1), g_kv=mla_kv_norm.reshape(1, -1),
        w_uq2=jnp.concatenate(plain + swapped, axis=1).astype(BF16),
        w_ukv=jnp.concatenate([w_mla_uk, w_mla_uv], axis=1).astype(BF16),
        lams=lams, subln=jnp.tile(diff_subln, 2).reshape(1, LANE),
        w_out=w_out.astype(BF16),
        g_mix_pre=norm_mix_pre.reshape(1, -1), g_mix_post=norm_mix_post.reshape(1, -1),
        g_ffn_pre=norm_ffn_pre.reshape(1, -1), g_ffn_post=norm_ffn_post.reshape(1, -1),
        g_ple_pre=norm_ple_pre.reshape(1, -1), g_ple_post=norm_ple_post.reshape(1, -1),
        w_up=w_ffn_up.astype(BF16), conv_w=ffn_conv_w, conv_b=ffn_conv_b.reshape(1, -1),
        w_down=w_ffn_down.astype(BF16), w_ple_gate=w_ple_gate.astype(BF16), w_ple_proj=w_ple_proj.astype(BF16))


def _rope_tables(pos):
    half = MLA_ROPE_DIM // 2
    inv_freq = ROPE_THETA ** (-jnp.arange(half, dtype=F32) / half)
    ang = pos.astype(F32)[:, None] * inv_freq[None, :]
    cos, sin = jnp.cos(ang), jnp.sin(ang)
    t = pos.shape[0]
    pad = jnp.zeros((t, MLA_PAIR_W - LANE - 2 * MLA_ROPE_DIM), F32)
    cos_t = jnp.concatenate([jnp.ones((t, LANE), F32), cos, cos, cos, cos, pad], axis=1)
    sin_t = jnp.concatenate([jnp.zeros((t, LANE), F32), -sin, sin, -sin, sin, pad], axis=1)
    return cos_t, sin_t


def _alibi_slopes():
    s = 2.0 ** (-8.0 * np.arange(1, N_DIFF_HEADS + 1) / N_DIFF_HEADS)
    return jnp.asarray(np.broadcast_to(s.reshape(N_DIFF_HEADS // 2, 2, 1), (N_DIFF_HEADS // 2, 2, LANE)), dtype=F32)


def _tile_rows(n, pref):
    t = min(n, pref)
    assert n % t == 0
    return t


def _layer(h, p, lw, cos, sin, cache, conv_left, lam_init, *, batch, seq, past, tq):
    n = h.shape[0]
    tm = _tile_rows(n, 512)
    (fox_k, fox_v, logf, ckv, krope, diff_k, diff_v,
     fox_bf, diff_bf, mla_q, mla_k, mla_v) = _proj_call(h, lw, cos, sin, tm)
    rows = (fox_k, fox_v, logf, ckv, krope, diff_k, diff_v)
    b3 = lambda a: a.reshape(batch, seq, a.shape[-1])
    diff_extras = [_alibi_slopes(), lw["lams"], lw["subln"]]
    diff_specs = [pl.BlockSpec((1, 2, LANE), lambda bb, hp, i: (hp, 0, 0)),
                  pl.BlockSpec((4, DIFF_QK_DIM), lambda bb, hp, i: (0, 0)),
                  pl.BlockSpec((1, LANE), lambda bb, hp, i: (0, 0))]
    fb, db = b3(fox_bf), b3(diff_bf)
    if cache is None:
        cb = _cumsum_call(b3(logf))
        cb_spec = pl.BlockSpec((1, 2, seq, LANE), lambda bb, hp, i: (bb, hp, 0, 0))
        common = dict(tq=tq, tkd=tq, qoff=0, blocked=True)
        o_a = _attn_call(_fox_kernel, "fox_attn", fb, 0, fb, 3, _values_t(fb[..., 2 * FOX_WIDTH:], tq),
                         [cb], [cb_spec], npairs=N_FOX_HEADS // 2, qw=LANE, chains=2, **common)
        o_b = _attn_call(_mla_kernel, "mla_attn", b3(mla_q), 0, b3(mla_k), 0, _values_t(b3(mla_v), tq), [], [],
                         npairs=N_MLA_PAIRS, qw=MLA_PAIR_W, chains=2, **common)
        o_c = _attn_call(_diff_kernel, "diff_attn", db, 0, db, 2, _values_t(db[..., 2 * DIFF_WIDTH:], tq),
                         diff_extras, diff_specs, npairs=N_DIFF_HEADS // 2, qw=LANE, chains=4, lam_init=lam_init, **common)
    else:
        c_fox_k, c_fox_v, c_logf, c_ckv, c_krope, c_diff_k, c_diff_v = cache
        tk = -(-(past + seq) // LANE) * LANE
        padk = lambda a: jnp.pad(a, ((0, 0), (0, tk - past - seq), (0, 0)))
        cat = lambda old, new: padk(jnp.concatenate([old.astype(new.dtype), new], axis=1))
        flat = lambda a: a.reshape(batch, past, -1)
        padq = lambda a: jnp.pad(a, ((0, 0), (0, tq - seq), (0, 0)))
        kf = cat(flat(c_fox_k), fb[..., FOX_WIDTH:2 * FOX_WIDTH])
        vf = cat(flat(c_fox_v), fb[..., 2 * FOX_WIDTH:])
        kd = cat(flat(c_diff_k), db[..., DIFF_WIDTH:2 * DIFF_WIDTH])
        vd = cat(flat(c_diff_v), db[..., 2 * DIFF_WIDTH:])
        cb = _cumsum_call(cat(c_logf, b3(logf)))
        kn_c, v_c = _kvup_call(c_ckv.reshape(batch * past, MLA_KV_RANK), lw["w_ukv"], _tile_rows(batch * past, 1024))
        kr_c = c_krope.astype(BF16)
        krp_c = jnp.concatenate([kr_c, kr_c, jnp.zeros((batch, past, LANE - 2 * MLA_ROPE_DIM), BF16)], axis=-1)
        km_c = jnp.concatenate([kn_c.reshape(batch, past, N_MLA_PAIRS, LANE),
                                jnp.broadcast_to(krp_c[:, :, None, :], (batch, past, N_MLA_PAIRS, LANE))],
                               axis=-1).reshape(batch, past, MLA_QK_W)
        km = cat(km_c, b3(mla_k))
        vm = cat(v_c.reshape(batch, past, MLA_WIDTH), b3(mla_v))
        cb_spec = pl.BlockSpec((1, 2, tk, LANE), lambda bb, hp, i: (bb, hp, 0, 0))
        common = dict(tq=tq, tkd=tk, qoff=past, blocked=False)
        o_a = _attn_call(_fox_kernel, "fox_attn_s", padq(fb[..., :FOX_WIDTH]), 0, kf, 0, _values_t(vf, tk),
                         [cb], [cb_spec], npairs=N_FOX_HEADS // 2, qw=LANE, chains=2, **common)
        o_b = _attn_call(_mla_kernel, "mla_attn_s", padq(b3(mla_q)), 0, km, 0, _values_t(vm, tk), [], [],
                         npairs=N_MLA_PAIRS, qw=MLA_PAIR_W, chains=2, **common)
        o_c = _attn_call(_diff_kernel, "diff_attn_s", padq(db[..., :DIFF_WIDTH]), 0, kd, 0, _values_t(vd, tk),
                         diff_extras, diff_specs, npairs=N_DIFF_HEADS // 2, qw=LANE, chains=4, lam_init=lam_init, **common)
        o_a, o_b, o_c = o_a[:, :seq], o_b[:, :seq], o_c[:, :seq]
    f2 = lambda a: a.reshape(n, a.shape[-1])
    h = _wout_call(f2(o_a), f2(o_b), f2(o_c), h, lw["w_out"], lw["g_mix_post"], tm)
    h, conv_state = _ffn_call(h, lw, conv_left, seq, tm, 512)
    h = _ple_call(h, p, lw, tm)
    return h, rows, conv_state


def kernel(x_prompt, x_sample, cache_fox_k, cache_fox_v, cache_fox_logf, cache_mla_ckv, cache_mla_krope, cache_diff_k, cache_diff_v, state_ffn_conv, p_prompt, p_sample, w_in, b_forget, mla_q_norm, w_mla_uq, mla_kv_norm, w_mla_uk, w_mla_uv, diff_lambda_q1, diff_lambda_k1, diff_lambda_q2, diff_lambda_k2, diff_subln, w_out, norm_mix_pre, norm_mix_post, norm_ffn_pre, norm_ffn_post, norm_ple_pre, norm_ple_post, w_ffn_up, ffn_conv_w, ffn_conv_b, w_ffn_down, w_ple_gate, w_ple_proj):
    bp, sp, _ = x_prompt.shape
    bs, ts, _ = x_sample.shape
    depth, _, past = cache_fox_k.shape[:3]
    assert past % CHUNK == 0 and ts <= CHUNK and sp % LANE == 0
    tq = _tile_rows(sp, 512)
    tms = _tile_rows(bs * ts, 512)

    cos_p, sin_p = _rope_tables(jnp.arange(sp))
    cos_s, sin_s = _rope_tables(past + jnp.arange(ts))
    cos_s, sin_s = jnp.tile(cos_s, (tms // ts, 1)), jnp.tile(sin_s, (tms // ts, 1))

    hp = x_prompt.reshape(bp * sp, D_MODEL)
    hs = x_sample.reshape(bs * ts, D_MODEL)
    rows_p, rows_s, conv_p, conv_s = [], [], [], []
    for l in range(depth):
        lams = jnp.stack([diff_lambda_q1[l], diff_lambda_k1[l], diff_lambda_q2[l], diff_lambda_k2[l]], axis=0)
        lw = _pack_layer(w_in[l], b_forget[l], mla_q_norm[l], w_mla_uq[l], mla_kv_norm[l], w_mla_uk[l], w_mla_uv[l],
                         lams, diff_subln[l], w_out[l], norm_mix_pre[l], norm_mix_post[l], norm_ffn_pre[l],
                         norm_ffn_post[l], norm_ple_pre[l], norm_ple_post[l], w_ffn_up[l], ffn_conv_w[l],
                         ffn_conv_b[l], w_ffn_down[l], w_ple_gate[l], w_ple_proj[l])
        lam_init = 0.8 - 0.6 * math.exp(-0.3 * l)
        hp, rp, cp = _layer(hp, p_prompt[l].reshape(bp * sp, PLE_DIM), lw, cos_p, sin_p, None,
                            jnp.zeros((bp, CONV_WIDTH - 1, D_FF), F32), lam_init,
                            batch=bp, seq=sp, past=0, tq=tq)
        cache_l = (cache_fox_k[l], cache_fox_v[l], cache_fox_logf[l], cache_mla_ckv[l], cache_mla_krope[l],
                   cache_diff_k[l], cache_diff_v[l])
        hs, rs, cs = _layer(hs, p_sample[l].reshape(bs * ts, PLE_DIM), lw, cos_s, sin_s, cache_l,
                            state_ffn_conv[l], lam_init, batch=bs, seq=ts, past=past, tq=LANE)
        rows_p.append(rp)
        rows_s.append(rs)
        conv_p.append(cp)
        conv_s.append(cs)

    def stack_rows(per_layer, batch, seq):
        fk, fv, lf, ckv, kr, dk, dv = (jnp.stack(a, axis=0) for a in zip(*per_layer))
        lead = (depth, batch, seq)
        return (fk.reshape(lead + (N_FOX_HEADS, HEAD_DIM)), fv.reshape(lead + (N_FOX_HEADS, HEAD_DIM)),
                lf.reshape(lead + (N_FOX_HEADS,)), ckv.reshape(lead + (MLA_KV_RANK,)),
                kr.reshape(lead + (MLA_ROPE_DIM,)), dk.reshape(lead + (N_DIFF_HEADS, 2 * DIFF_QK_DIM)),
                dv.reshape(lead + (N_DIFF_HEADS, DIFF_V_DIM)))

    out_p = stack_rows(rows_p, bp, sp)
    out_s = stack_rows(rows_s, bs, ts)
    return ((hp.reshape(bp, sp, D_MODEL), hs.reshape(bs, ts, D_MODEL)) + out_p + (jnp.stack(conv_p, axis=0),)
            + out_s + (jnp.stack(conv_s, axis=0),))
```

```python
import functools
import math

import jax
import jax.numpy as jnp
import numpy as np
from jax import lax
from jax.experimental import pallas as pl
from jax.experimental.pallas import tpu as pltpu

F32 = jnp.float32
BF16 = jnp.bfloat16

D_MODEL = 1024
HEAD_DIM = 64
N_FOX_HEADS = 6
N_MLA_HEADS = 6
N_DIFF_HEADS = 4
MLA_Q_RANK = 384
MLA_KV_RANK = 256
MLA_NOPE_DIM = 64
MLA_ROPE_DIM = 32
MLA_V_DIM = 64
DIFF_QK_DIM = 32
DIFF_V_DIM = 64
FOX_WIDTH = N_FOX_HEADS * HEAD_DIM
MLA_WIDTH = N_MLA_HEADS * MLA_V_DIM
DIFF_WIDTH = N_DIFF_HEADS * DIFF_V_DIM
D_FF = 4 * D_MODEL
CONV_WIDTH = 3
PLE_DIM = 256
CHUNK = 64
ROPE_THETA = 10000.0
RMS_EPS = 1e-6
NEG_INF = -1e30
LOG2E = math.log2(math.e)

OFF_FOX_Q = 0
OFF_FOX_F = 3 * FOX_WIDTH
OFF_MLA_CQ = OFF_FOX_F + N_FOX_HEADS
OFF_MLA_CKV = OFF_MLA_CQ + MLA_Q_RANK
OFF_MLA_KR = OFF_MLA_CKV + MLA_KV_RANK
OFF_DIFF_Q = OFF_MLA_KR + MLA_ROPE_DIM
IN_WIDTH = OFF_DIFF_Q + 2 * N_DIFF_HEADS * 2 * DIFF_QK_DIM + DIFF_WIDTH

LANE = 128
V7X_VMEM_BYTES = 64 * 1024 * 1024
VMEM_LIMIT = (V7X_VMEM_BYTES * 7) // 8

C_FOX = 0
C_DIFF = C_FOX + 3 * FOX_WIDTH
C_CQ = C_DIFF + 3 * DIFF_WIDTH
C_CKV = C_CQ + MLA_Q_RANK
C_KR = C_CKV + MLA_KV_RANK
C_F = C_KR + 2 * LANE
W_ALL = C_F + LANE
N_MLA_PAIRS = N_MLA_HEADS // 2
MLA_PAIR_W = 2 * LANE
MLA_QK_W = N_MLA_PAIRS * MLA_PAIR_W

FOX_QSCALE = HEAD_DIM ** -0.5 * LOG2E
MLA_QSCALE = (MLA_NOPE_DIM + MLA_ROPE_DIM) ** -0.5 * LOG2E
DIFF_QSCALE = DIFF_QK_DIM ** -0.5 * LOG2E


def _cparams(sem):
    return pltpu.CompilerParams(dimension_semantics=sem, vmem_limit_bytes=VMEM_LIMIT)


def _rms(x, g):
    return x * lax.rsqrt(jnp.mean(x * x, axis=-1, keepdims=True) + RMS_EPS) * g


def _dot(a, b):
    return jnp.dot(a, b, preferred_element_type=F32)


def _dot_nt(a, b):
    return lax.dot_general(a, b, (((1,), (1,)), ((), ())), preferred_element_type=F32)


def _proj_kernel(h_ref, gpre_ref, w_ref, bf_ref, gq_ref, gkv_ref, wuq_ref, wukv_ref, cos_ref, sin_ref,
                 foxk_ref, foxv_ref, logf_ref, ckv_ref, krope_ref, diffk_ref, diffv_ref,
                 foxbf_ref, diffbf_ref, mlaq_ref, mlak_ref, mlav_ref):
    xn = _rms(h_ref[...], gpre_ref[...]).astype(BF16)
    cos = cos_ref[...]
    sin = sin_ref[...]

    z = _dot(xn, w_ref[:, C_FOX:C_FOX + 3 * FOX_WIDTH])
    foxbf_ref[:, :FOX_WIDTH] = (z[:, :FOX_WIDTH] * FOX_QSCALE).astype(BF16)
    foxbf_ref[:, FOX_WIDTH:] = z[:, FOX_WIDTH:].astype(BF16)
    foxk_ref[...] = z[:, FOX_WIDTH:2 * FOX_WIDTH]
    foxv_ref[...] = z[:, 2 * FOX_WIDTH:3 * FOX_WIDTH]

    z = _dot(xn, w_ref[:, C_DIFF:C_DIFF + 3 * DIFF_WIDTH])
    diffbf_ref[:, :DIFF_WIDTH] = (z[:, :DIFF_WIDTH] * DIFF_QSCALE).astype(BF16)
    diffbf_ref[:, DIFF_WIDTH:] = z[:, DIFF_WIDTH:].astype(BF16)
    diffk_ref[...] = z[:, DIFF_WIDTH:2 * DIFF_WIDTH]
    diffv_ref[...] = z[:, 2 * DIFF_WIDTH:3 * DIFF_WIDTH]

    cq = _rms(_dot(xn, w_ref[:, C_CQ:C_CQ + MLA_Q_RANK]), gq_ref[...]).astype(BF16)
    q2 = _dot(cq, wuq_ref[...])
    for p in range(N_MLA_PAIRS):
        lo = p * MLA_PAIR_W
        roped = q2[:, lo:lo + MLA_PAIR_W] * cos + q2[:, MLA_QK_W + lo:MLA_QK_W + lo + MLA_PAIR_W] * sin
        mlaq_ref[:, lo:lo + MLA_PAIR_W] = (roped * MLA_QSCALE).astype(BF16)

    ckv = _rms(_dot(xn, w_ref[:, C_CKV:C_CKV + MLA_KV_RANK]), gkv_ref[...])
    ckv_ref[...] = ckv
    kv = _dot(ckv.astype(BF16), wukv_ref[...])
    mlav_ref[...] = kv[:, MLA_WIDTH:2 * MLA_WIDTH].astype(BF16)
    zkr = _dot(xn, w_ref[:, C_KR:C_KR + 2 * LANE])
    krp = zkr[:, :LANE] * cos[:, LANE:] + zkr[:, LANE:] * sin[:, LANE:]
    krope_ref[...] = krp[:, :MLA_ROPE_DIM]
    krp16 = krp.astype(BF16)
    for p in range(N_MLA_PAIRS):
        lo = p * MLA_PAIR_W
        mlak_ref[:, lo:lo + LANE] = kv[:, p * LANE:(p + 1) * LANE].astype(BF16)
        mlak_ref[:, lo + LANE:lo + 2 * LANE] = krp16

    zf = _dot(xn, w_ref[:, C_F:C_F + LANE]) + bf_ref[...]
    logf = -(jnp.maximum(-zf, 0.0) + jnp.log1p(jnp.exp(-jnp.abs(zf))))
    logf_ref[...] = logf[:, :N_FOX_HEADS]


def _proj_call(h, lw, cos, sin, tm):
    n = h.shape[0]
    ntab = cos.shape[0] // tm
    row = lambda i: (i, 0)
    full = lambda i: (0, 0)
    tab = lambda i: (i % ntab, 0)
    in_specs = [
        pl.BlockSpec((tm, D_MODEL), row),
        pl.BlockSpec((1, D_MODEL), full),
        pl.BlockSpec((D_MODEL, W_ALL), full),
        pl.BlockSpec((1, LANE), full),
        pl.BlockSpec((1, MLA_Q_RANK), full),
        pl.BlockSpec((1, MLA_KV_RANK), full),
        pl.BlockSpec((MLA_Q_RANK, 2 * MLA_QK_W), full),
        pl.BlockSpec((MLA_KV_RANK, 2 * MLA_WIDTH), full),
        pl.BlockSpec((tm, MLA_PAIR_W), tab),
        pl.BlockSpec((tm, MLA_PAIR_W), tab),
    ]
    widths = [(FOX_WIDTH, F32), (FOX_WIDTH, F32), (N_FOX_HEADS, F32), (MLA_KV_RANK, F32), (MLA_ROPE_DIM, F32),
              (DIFF_WIDTH, F32), (DIFF_WIDTH, F32),
              (3 * FOX_WIDTH, BF16), (3 * DIFF_WIDTH, BF16), (MLA_QK_W, BF16), (MLA_QK_W, BF16), (MLA_WIDTH, BF16)]
    out_shape = [jax.ShapeDtypeStruct((n, w), dt) for w, dt in widths]
    out_specs = [pl.BlockSpec((tm, w), row) for w, _ in widths]
    return pl.pallas_call(
        _proj_kernel, grid=(n // tm,), in_specs=in_specs, out_specs=out_specs, out_shape=out_shape,
        compiler_params=_cparams(("parallel",)), name="proj",
    )(h, lw["g_mix_pre"], lw["w_all"], lw["b_f"], lw["g_q"], lw["g_kv"], lw["w_uq2"], lw["w_ukv"], cos, sin)


def _kvup_kernel(ckv_ref, wukv_ref, kn_ref, v_ref):
    kv = _dot(ckv_ref[...].astype(BF16), wukv_ref[...])
    kn_ref[...] = kv[:, :MLA_WIDTH].astype(BF16)
    v_ref[...] = kv[:, MLA_WIDTH:].astype(BF16)


def _kvup_call(ckv, w_ukv, tm):
    n = ckv.shape[0]
    row = lambda i: (i, 0)
    return pl.pallas_call(
        _kvup_kernel, grid=(n // tm,),
        in_specs=[pl.BlockSpec((tm, MLA_KV_RANK), row), pl.BlockSpec((MLA_KV_RANK, 2 * MLA_WIDTH), lambda i: (0, 0))],
        out_specs=[pl.BlockSpec((tm, MLA_WIDTH), row), pl.BlockSpec((tm, MLA_WIDTH), row)],
        out_shape=[jax.ShapeDtypeStruct((n, MLA_WIDTH), BF16)] * 2,
        compiler_params=_cparams(("parallel",)), name="kvup",
    )(ckv, w_ukv)


AUG_STRIDE = 8


def _keep_bf16_bits(x):
    bits = lax.bitcast_convert_type(x, jnp.uint32) & jnp.uint32(0xFFFF0000)
    return lax.bitcast_convert_type(bits, F32)


def _split3(x):
    x1 = _keep_bf16_bits(x)
    r = x - x1
    x2 = _keep_bf16_bits(r)
    return x1, x2, r - x2


def _place3(x1, x2, x3):
    lane = lax.broadcasted_iota(jnp.int32, (1, LANE), 1)
    return jnp.where(lane < AUG_STRIDE, x1,
                     jnp.where(lane < 2 * AUG_STRIDE, pltpu.roll(x2, AUG_STRIDE, 1),
                               pltpu.roll(x3, 2 * AUG_STRIDE, 1))).astype(BF16)


def _aug_selector(h, rows):
    lane = lax.broadcasted_iota(jnp.int32, (rows, LANE), 1)
    hit = (lane == h) | (lane == AUG_STRIDE + h) | (lane == 2 * AUG_STRIDE + h)
    return jnp.where(hit, 1.0, 0.0).astype(BF16)


def _forget_bias_kernel(x_ref, o_ref, carry_sc, *, tb):
    @pl.when(pl.program_id(1) == 0)
    def _():
        carry_sc[...] = jnp.zeros_like(carry_sc)

    r, c = _key_query_iota(tb, tb)
    tri = jnp.where(c <= r, 1.0, 0.0).astype(BF16)
    cs = carry_sc[0:1, :]
    for term in _split3(x_ref[0]):
        cs = cs + _dot(tri, term.astype(BF16))
    carry_sc[...] = jnp.broadcast_to(cs[tb - 1:tb, :], carry_sc.shape)
    o_ref[0] = _place3(*_split3(cs * (-LOG2E)))


def _forget_bias_call(logf):
    b, t, hh = logf.shape
    tb = LANE * math.gcd(t // LANE, 4)
    x = jnp.pad(logf, ((0, 0), (0, 0), (0, LANE - hh)))
    blk = pl.BlockSpec((1, tb, LANE), lambda i, j: (i, j, 0))
    return pl.pallas_call(
        functools.partial(_forget_bias_kernel, tb=tb), grid=(b, t // tb), in_specs=[blk], out_specs=blk,
        out_shape=jax.ShapeDtypeStruct((b, t, LANE), BF16),
        scratch_shapes=[pltpu.VMEM((8, LANE), F32)],
        compiler_params=_cparams(("parallel", "arbitrary")), name="forget_bias",
    )(x)


def _alibi_bias_block(t_k):
    slopes = 2.0 ** (-8.0 * np.arange(1, N_DIFF_HEADS + 1) / N_DIFF_HEADS)
    b = jnp.zeros((t_k, LANE), F32).at[:, :N_DIFF_HEADS].set(
        jnp.arange(t_k, dtype=F32)[:, None] * jnp.asarray(slopes * LOG2E, F32)[None, :])
    x1, x2, x3 = _split3(b)
    lane = jnp.arange(LANE)[None, :]
    placed = jnp.where(lane < AUG_STRIDE, x1, jnp.where(lane < 2 * AUG_STRIDE, jnp.roll(x2, AUG_STRIDE, 1),
                                                        jnp.roll(x3, 2 * AUG_STRIDE, 1)))
    return placed.astype(BF16)[None]


ROW_CHUNK = CHUNK
STRIP = 4 * LANE
ONES_ROWS = 16


def _softmax_block(s_ref, p_ref, m_ref, acc_ref, c, vt, mask, tk, tq):
    sub = ROW_CHUNK // 8
    alphas = []
    sw = min(STRIP, tq)
    for c0 in range(0, tq, sw):
        cols = slice(c0, c0 + sw)
        chunks = [(r0, "all" if mask is None else mask[0](r0, c0, sw)) for r0 in range(0, tk, ROW_CHUNK)]

        def logits(r0, vis):
            x = s_ref[c, r0:r0 + ROW_CHUNK, cols]
            return x if vis == "all" else mask[1](r0, c0, x)

        mrun = jnp.full((8, sw), NEG_INF, F32)
        for r0, vis in chunks:
            if vis != "none":
                mrun = jnp.maximum(mrun, jnp.max(logits(r0, vis).reshape(sub, 8, sw), axis=0))
        m_old = m_ref[c, :, cols]
        m_new = jnp.maximum(m_old, jnp.max(mrun, axis=0, keepdims=True))
        for r0, vis in chunks:
            if vis == "none":
                p_ref[c, r0:r0 + ROW_CHUNK, cols] = jnp.zeros((ROW_CHUNK, sw), BF16)
            else:
                p_ref[c, r0:r0 + ROW_CHUNK, cols] = jnp.exp2(logits(r0, vis) - m_new).astype(BF16)
        m_ref[c, :, cols] = m_new
        alphas.append(jnp.exp2(m_old - m_new))
    acc_ref[c] = jnp.concatenate(alphas, axis=1) * acc_ref[c] + _dot(vt, p_ref[c])


def _causal_mask(qoff):
    def visibility(r0, c0, sw):
        if r0 + ROW_CHUNK - 1 <= c0 + qoff:
            return "all"
        return "none" if r0 > c0 + sw - 1 + qoff else "some"

    def apply(r0, c0, x):
        kk, qq = _key_query_iota(*x.shape)
        return jnp.where(kk - qq <= c0 + qoff - r0, x, NEG_INF)
    return visibility, apply


def _chunk_visibility(r0, c0, sw, qoff):
    if r0 // CHUNK <= (c0 + qoff) // CHUNK:
        return "all"
    return "none" if r0 // CHUNK > (c0 + sw - 1 + qoff) // CHUNK else "some"


def _chunk_mask(r0, c0, sw, qoff):
    qq = lax.broadcasted_iota(jnp.int32, (1, sw), 1)
    return r0 // CHUNK <= (qq + (c0 + qoff)) // CHUNK


def _init_states(m_ref, acc_ref):
    m_ref[...] = jnp.full(m_ref.shape, NEG_INF, F32)
    acc_ref[...] = jnp.zeros(acc_ref.shape, F32)


def _normalized(acc_ref, c, dv):
    return acc_ref[c, :dv, :] / acc_ref[c, dv:dv + 1, :]


def _key_query_iota(tk, tq):
    return (lax.broadcasted_iota(jnp.int32, (tk, tq), 0), lax.broadcasted_iota(jnp.int32, (tk, tq), 1))


def _store_pair(o_ref, o0, o1, row_scale=None):
    out = jnp.concatenate([o0, o1], axis=0).T
    if row_scale is not None:
        out = out * row_scale
    o_ref[0] = out.astype(BF16)


def _block_start(j, size):
    return j * size if isinstance(j, int) else pl.multiple_of(j * size, size)


def _attend(i, blocked, qk_fn, step_fn, s_a, s_b):
    if not blocked:
        qk_fn(s_a, 0)
        step_fn(s_a, 0, True)
        return
    odd = i % 2

    @pl.when(odd == 1)
    def _():
        qk_fn(s_b, 0)
        step_fn(s_b, 0, False)

    qk_fn(s_a, odd)

    def pair(t, carry):
        b0 = odd + 2 * t
        qk_fn(s_b, b0 + 1)
        step_fn(s_a, b0, False)
        qk_fn(s_a, b0 + 2)
        step_fn(s_b, b0 + 1, False)
        return carry

    lax.fori_loop(0, (i - odd) // 2, pair, 0)
    step_fn(s_a, i, True)


def _vt_rows(vt_ref, j, e, dv):
    return vt_ref[0, 0, j, e * (dv + ONES_ROWS):(e + 1) * (dv + ONES_ROWS), :]


def _fox_kernel(q_ref, k_ref, aug_ref, vt_ref, o_ref, s_a, s_b, p_sc, m_sc, acc_sc, *, tq, tkd, qoff, blocked):
    hp, i = pl.program_id(1), pl.program_id(2)
    _init_states(m_sc, acc_sc)
    q = q_ref[0]
    lane = lax.broadcasted_iota(jnp.int32, (1, LANE), 1)
    qs = [jnp.concatenate([jnp.where((lane >= e * HEAD_DIM) & (lane < (e + 1) * HEAD_DIM), q, jnp.zeros_like(q)),
                           _aug_selector(2 * hp + e, tq)], axis=1) for e in range(2)]

    def qk_fn(buf, j):
        rows = pl.ds(_block_start(j, tq), tkd)
        k = jnp.concatenate([k_ref[0, rows, :], aug_ref[0, rows, :]], axis=1)
        for e in range(2):
            buf[e] = _dot_nt(k, qs[e])

    def step_fn(buf, j, diag):
        for e in range(2):
            _softmax_block(buf, p_sc, m_sc, acc_sc, e, _vt_rows(vt_ref, j, e, HEAD_DIM),
                           _causal_mask(qoff) if diag else None, tkd, tq)

    _attend(i, blocked, qk_fn, step_fn, s_a, s_b)
    _store_pair(o_ref, *(_normalized(acc_sc, e, HEAD_DIM) for e in range(2)))


def _mla_kernel(q_ref, k_ref, vt_ref, o_ref, s_a, s_b, p_sc, m_sc, acc_sc, *, tq, tkd, qoff, blocked):
    i = pl.program_id(2)
    _init_states(m_sc, acc_sc)
    q = q_ref[0]
    lane2 = lax.broadcasted_iota(jnp.int32, (1, MLA_PAIR_W), 1)
    sel = []
    for e in range(2):
        nope = (lane2 >= e * MLA_NOPE_DIM) & (lane2 < (e + 1) * MLA_NOPE_DIM)
        rope = (lane2 >= LANE + e * MLA_ROPE_DIM) & (lane2 < LANE + (e + 1) * MLA_ROPE_DIM)
        sel.append(jnp.where(nope | rope, q, jnp.zeros_like(q)))

    def qk_fn(buf, j):
        k = k_ref[0, pl.ds(_block_start(j, tq), tkd), :]
        for e in range(2):
            buf[e] = _dot_nt(k, sel[e])

    def step_fn(buf, j, diag):
        mask = (lambda r0, c0, sw: _chunk_visibility(r0, c0, sw, qoff),
                lambda r0, c0, x: jnp.where(_chunk_mask(r0, c0, x.shape[1], qoff), x, NEG_INF))
        for e in range(2):
            _softmax_block(buf, p_sc, m_sc, acc_sc, e, _vt_rows(vt_ref, j, e, MLA_V_DIM), mask if diag else None,
                           tkd, tq)

    _attend(i, blocked, qk_fn, step_fn, s_a, s_b)
    _store_pair(o_ref, *(_normalized(acc_sc, e, MLA_V_DIM) for e in range(2)))


def _diff_kernel(q_ref, k_ref, aug_ref, vt_ref, slope_ref, lam_ref, subln_ref, o_ref, s_a, s_b, p_sc, m_sc, acc_sc,
                 *, tq, tkd, qoff, blocked, lam_init):
    hp, i = pl.program_id(1), pl.program_id(2)
    _init_states(m_sc, acc_sc)
    q = q_ref[0]
    lane = lax.broadcasted_iota(jnp.int32, (1, LANE), 1)
    sel = [jnp.concatenate([jnp.where((lane >= (2 * e + t) * DIFF_QK_DIM) & (lane < (2 * e + t + 1) * DIFF_QK_DIM),
                                      q, jnp.zeros_like(q)), _aug_selector(2 * hp + e, tq)], axis=1)
           for e in range(2) for t in range(2)]
    slopes = [slope_ref[0, e:e + 1, 0:1] * LOG2E for e in range(2)]

    def qk_fn(buf, j):
        rows = pl.ds(_block_start(j, tq), tkd)
        k = jnp.concatenate([k_ref[0, rows, :], aug_ref[0, rows, :]], axis=1)
        for c in range(4):
            buf[c] = _dot_nt(k, sel[c])

    def step_fn(buf, j, diag):
        for c in range(4):
            e = c // 2

            def visibility(r0, c0, sw):
                vis = _chunk_visibility(r0, c0, sw, qoff)
                return "some" if vis == "all" and r0 + ROW_CHUNK - 1 > c0 + qoff else vis

            def apply(r0, c0, x, e=e):
                kk, qq = _key_query_iota(*x.shape)
                ahead = jnp.maximum(kk - qq + (r0 - c0 - qoff), 0).astype(F32)
                return jnp.where(_chunk_mask(r0, c0, x.shape[1], qoff), x - (2.0 * slopes[e]) * ahead, NEG_INF)
            _softmax_block(buf, p_sc, m_sc, acc_sc, c, _vt_rows(vt_ref, j, e, DIFF_V_DIM),
                           (visibility, apply) if diag else None, tkd, tq)

    _attend(i, blocked, qk_fn, step_fn, s_a, s_b)
    res = [_normalized(acc_sc, c, DIFF_V_DIM) for c in range(4)]
    lq = lam_ref[...]
    lam = (jnp.exp(jnp.sum(lq[0:1] * lq[1:2], axis=-1, keepdims=True))
           - jnp.exp(jnp.sum(lq[2:3] * lq[3:4], axis=-1, keepdims=True)) + lam_init)
    outs = []
    for e in range(2):
        o = res[2 * e] - lam * res[2 * e + 1]
        outs.append(o * lax.rsqrt(jnp.mean(o * o, axis=0, keepdims=True) + RMS_EPS))
    _store_pair(o_ref, outs[0], outs[1], subln_ref[...] * (1.0 - lam_init))


def _attn_call(kernel, name, q, q_cb, k, k_cb, aug, vt, extras, extra_specs, *, npairs, qw, tq, tkd, qoff, blocked,
               chains, **kw):
    b, t_q = q.shape[0], q.shape[1]
    t_k = k.shape[1]
    acc_rows = LANE // 2 + ONES_ROWS
    scratch = [pltpu.VMEM((chains, tkd, tq), F32), pltpu.VMEM((chains, tkd, tq), F32),
               pltpu.VMEM((chains, tkd, tq), BF16), pltpu.VMEM((chains, 1, tq), F32),
               pltpu.VMEM((chains, acc_rows, tq), F32)]
    in_specs = [pl.BlockSpec((1, tq, qw), lambda bb, hp, i: (bb, i, q_cb + hp)),
                pl.BlockSpec((1, t_k, qw), lambda bb, hp, i: (bb, 0, k_cb + hp))]
    args = [q, k]
    if aug is not None:
        per_batch = aug.shape[0] > 1
        in_specs.append(pl.BlockSpec((1, t_k, LANE), lambda bb, hp, i: (bb if per_batch else 0, 0, 0)))
        args.append(aug)
    in_specs.append(pl.BlockSpec((1, 1) + vt.shape[2:], lambda bb, hp, i: (bb, hp, 0, 0, 0)))
    return pl.pallas_call(
        functools.partial(kernel, tq=tq, tkd=tkd, qoff=qoff, blocked=blocked, **kw),
        grid=(b, npairs, t_q // tq), in_specs=in_specs + extra_specs,
        out_specs=pl.BlockSpec((1, tq, LANE), lambda bb, hp, i: (bb, i, hp)),
        out_shape=jax.ShapeDtypeStruct((b, t_q, npairs * LANE), BF16), scratch_shapes=scratch,
        compiler_params=_cparams(("parallel", "parallel", "arbitrary")), name=name,
    )(*args, vt, *extras)


def _values_t(v, tkb):
    b, t, w = v.shape
    dv = LANE // 2
    vt = v.reshape(b, t // tkb, tkb, w // LANE, 2, dv).transpose(0, 3, 1, 4, 5, 2)
    ones = jnp.ones(vt.shape[:4] + (ONES_ROWS, tkb), v.dtype)
    return jnp.concatenate([vt, ones], axis=4).reshape(b, w // LANE, t // tkb, 2 * (dv + ONES_ROWS), tkb)


def _wout_kernel(oa_ref, ob_ref, oc_ref, h_ref, w_ref, g_ref, o_ref):
    y = (_dot(oa_ref[...], w_ref[0:FOX_WIDTH, :])
         + _dot(ob_ref[...], w_ref[FOX_WIDTH:FOX_WIDTH + MLA_WIDTH, :])
         + _dot(oc_ref[...], w_ref[FOX_WIDTH + MLA_WIDTH:, :]))
    o_ref[...] = h_ref[...] + _rms(y, g_ref[...])


def _wout_call(oa, ob, oc, h, w_out, g, tm):
    n = h.shape[0]
    row = lambda i: (i, 0)
    full = lambda i: (0, 0)
    return pl.pallas_call(
        _wout_kernel, grid=(n // tm,),
        in_specs=[pl.BlockSpec((tm, FOX_WIDTH), row), pl.BlockSpec((tm, MLA_WIDTH), row),
                  pl.BlockSpec((tm, DIFF_WIDTH), row), pl.BlockSpec((tm, D_MODEL), row),
                  pl.BlockSpec((D_MODEL, D_MODEL), full), pl.BlockSpec((1, D_MODEL), full)],
        out_specs=pl.BlockSpec((tm, D_MODEL), row),
        out_shape=jax.ShapeDtypeStruct((n, D_MODEL), F32),
        compiler_params=_cparams(("parallel",)), name="wout",
    )(oa, ob, oc, h, w_out, g)


def _ffn_kernel(h_ref, gpre_ref, wg_ref, wv_ref, cw_ref, cb_ref, wd_ref, gpost_ref, left_ref,
                o_ref, st_ref, xn_sc, acc_sc, carry_sc, *, tm, tf, nsb, seq_blocks):
    i = pl.program_id(0)
    f = pl.program_id(1)
    nf = pl.num_programs(1)
    tb = tm // nsb

    @pl.when(f == 0)
    def _():
        xn_sc[...] = _rms(h_ref[...], gpre_ref[...]).astype(BF16)
        acc_sc[...] = jnp.zeros_like(acc_sc)

    xn = xn_sc[...]
    gate = _dot(xn, wg_ref[...])
    val = _dot(xn, wv_ref[...])

    left = left_ref[...]
    if seq_blocks > 1:
        left = jnp.where(i % seq_blocks == 0, left, carry_sc[f, 0:CONV_WIDTH - 1, :][None])
        carry_sc[f, 0:CONV_WIDTH - 1, :] = gate[tm - (CONV_WIDTH - 1):, :]
    st_ref[f, pl.ds((i // seq_blocks) * nsb, nsb)] = gate.reshape(nsb, tb, tf)[:, tb - (CONV_WIDTH - 1):, :]

    def spread(rows):
        return jnp.broadcast_to(rows, (nsb, tb, tf)).reshape(tm, tf)

    rin = lax.broadcasted_iota(jnp.int32, (tm, 1), 0) & (tb - 1)
    l0 = spread(left[:, 0:1, :])
    l1 = spread(left[:, 1:2, :])
    g1 = jnp.where(rin == 0, l1, pltpu.roll(gate, 1, 0))
    g2 = jnp.where(rin == 0, l0, jnp.where(rin == 1, l1, pltpu.roll(gate, 2, 0)))
    cw = cw_ref[...]
    conv = cw[0:1] * g2 + cw[1:2] * g1 + cw[2:3] * gate + cb_ref[...]
    gelu = 0.5 * conv * (1.0 + jnp.tanh(math.sqrt(2.0 / math.pi) * (conv + 0.044715 * (conv * conv * conv))))
    acc_sc[...] += _dot((gelu * val).astype(BF16), wd_ref[...])

    @pl.when(f == nf - 1)
    def _():
        o_ref[...] = h_ref[...] + _rms(acc_sc[...], gpost_ref[...])


def _ffn_call(h, lw, left, seq_len, tm, tf):
    n = h.shape[0]
    nseq = left.shape[0]
    nf = D_FF // tf
    if seq_len >= tm:
        nsb, seq_blocks = 1, seq_len // tm
    else:
        nsb, seq_blocks = tm // seq_len, 1
    assert (tm // nsb) & (tm // nsb - 1) == 0
    row = lambda i, f: (i, 0)
    full = lambda i, f: (0, 0)
    out, state = pl.pallas_call(
        functools.partial(_ffn_kernel, tm=tm, tf=tf, nsb=nsb, seq_blocks=seq_blocks),
        grid=(n // tm, nf),
        in_specs=[pl.BlockSpec((tm, D_MODEL), row), pl.BlockSpec((1, D_MODEL), full),
                  pl.BlockSpec((D_MODEL, tf), lambda i, f: (0, f)),
                  pl.BlockSpec((D_MODEL, tf), lambda i, f: (0, nf + f)),
                  pl.BlockSpec((CONV_WIDTH, tf), lambda i, f: (0, f)),
                  pl.BlockSpec((1, tf), lambda i, f: (0, f)),
                  pl.BlockSpec((tf, D_MODEL), lambda i, f: (f, 0)),
                  pl.BlockSpec((1, D_MODEL), full),
                  pl.BlockSpec((nsb, CONV_WIDTH - 1, tf), lambda i, f: (i // seq_blocks, 0, f))],
        out_specs=[pl.BlockSpec((tm, D_MODEL), row),
                   pl.BlockSpec((nf, nseq, CONV_WIDTH - 1, tf), lambda i, f: (0, 0, 0, 0))],
        out_shape=[jax.ShapeDtypeStruct((n, D_MODEL), F32),
                   jax.ShapeDtypeStruct((nf, nseq, CONV_WIDTH - 1, tf), F32)],
        scratch_shapes=[pltpu.VMEM((tm, D_MODEL), BF16), pltpu.VMEM((tm, D_MODEL), F32),
                        pltpu.VMEM((nf, 8, tf), F32)],
        compiler_params=_cparams(("arbitrary", "arbitrary")), name="ffn",
    )(h, lw["g_ffn_pre"], lw["w_up"], lw["w_up"], lw["conv_w"], lw["conv_b"], lw["w_down"], lw["g_ffn_post"], left)
    return out, state.transpose(1, 2, 0, 3).reshape(nseq, CONV_WIDTH - 1, D_FF)


def _ple_kernel(h_ref, p_ref, gpre_ref, wg_ref, wp_ref, gpost_ref, o_ref):
    h = h_ref[...]
    gate = jax.nn.sigmoid(_dot(_rms(h, gpre_ref[...]).astype(BF16), wg_ref[...]))
    proj = _dot(p_ref[...].astype(BF16), wp_ref[...])
    o_ref[...] = h + _rms(proj * gate, gpost_ref[...])


def _ple_call(h, p, lw, tm):
    n = h.shape[0]
    row = lambda i: (i, 0)
    full = lambda i: (0, 0)
    return pl.pallas_call(
        _ple_kernel, grid=(n // tm,),
        in_specs=[pl.BlockSpec((tm, D_MODEL), row), pl.BlockSpec((tm, PLE_DIM), row),
                  pl.BlockSpec((1, D_MODEL), full), pl.BlockSpec((D_MODEL, D_MODEL), full),
                  pl.BlockSpec((PLE_DIM, D_MODEL), full), pl.BlockSpec((1, D_MODEL), full)],
        out_specs=pl.BlockSpec((tm, D_MODEL), row),
        out_shape=jax.ShapeDtypeStruct((n, D_MODEL), F32),
        compiler_params=_cparams(("parallel",)), name="ple",
    )(h, p, lw["g_ple_pre"], lw["w_ple_gate"], lw["w_ple_proj"], lw["g_ple_post"])


def _swap_halves(w):
    half = MLA_ROPE_DIM // 2
    return jnp.concatenate([w[..., half:], w[..., :half]], axis=-1)


def _pack_layer(w_in, b_forget, mla_q_norm, w_mla_uq, mla_kv_norm, w_mla_uk, w_mla_uv, lams, diff_subln, w_out,
                norm_mix_pre, norm_mix_post, norm_ffn_pre, norm_ffn_post, norm_ple_pre, norm_ple_post,
                w_ffn_up, ffn_conv_w, ffn_conv_b, w_ffn_down, w_ple_gate, w_ple_proj):
    zeros = lambda r, c: jnp.zeros((r, c), F32)
    w_kr = w_in[:, OFF_MLA_KR:OFF_DIFF_Q]
    w_all = jnp.concatenate([
        w_in[:, OFF_FOX_Q:OFF_FOX_F], w_in[:, OFF_DIFF_Q:IN_WIDTH],
        w_in[:, OFF_MLA_CQ:OFF_MLA_CKV], w_in[:, OFF_MLA_CKV:OFF_MLA_KR],
        w_kr, w_kr, zeros(D_MODEL, LANE - 2 * MLA_ROPE_DIM),
        _swap_halves(w_kr), _swap_halves(w_kr), zeros(D_MODEL, LANE - 2 * MLA_ROPE_DIM),
        w_in[:, OFF_FOX_F:OFF_MLA_CQ], zeros(D_MODEL, LANE - N_FOX_HEADS)], axis=1).astype(BF16)
    wq = w_mla_uq.reshape(MLA_Q_RANK, N_MLA_HEADS, MLA_NOPE_DIM + MLA_ROPE_DIM)
    plain, swapped = [], []
    pad = zeros(MLA_Q_RANK, MLA_PAIR_W - 2 * (MLA_NOPE_DIM + MLA_ROPE_DIM))
    for p in range(N_MLA_PAIRS):
        a, b = 2 * p, 2 * p + 1
        plain += [wq[:, a, :MLA_NOPE_DIM], wq[:, b, :MLA_NOPE_DIM], wq[:, a, MLA_NOPE_DIM:], wq[:, b, MLA_NOPE_DIM:], pad]
        swapped += [zeros(MLA_Q_RANK, 2 * MLA_NOPE_DIM), _swap_halves(wq[:, a, MLA_NOPE_DIM:]),
                    _swap_halves(wq[:, b, MLA_NOPE_DIM:]), pad]
    return dict(
        w_all=w_all,
        b_f=jnp.pad(b_forget, (0, LANE - N_FOX_HEADS)).reshape(1, LANE),
        g_q=mla_q_norm.reshape(1, -1), g_kv=mla_kv_norm.reshape(1, -1),
        w_uq2=jnp.concatenate(plain + swapped, axis=1).astype(BF16),
        w_ukv=jnp.concatenate([w_mla_uk, w_mla_uv], axis=1).astype(BF16),
        lams=lams, subln=jnp.tile(diff_subln, 2).reshape(1, LANE),
        w_out=w_out.astype(BF16),
        g_mix_pre=norm_mix_pre.reshape(1, -1), g_mix_post=norm_mix_post.reshape(1, -1),
        g_ffn_pre=norm_ffn_pre.reshape(1, -1), g_ffn_post=norm_ffn_post.reshape(1, -1),
        g_ple_pre=norm_ple_pre.reshape(1, -1), g_ple_post=norm_ple_post.reshape(1, -1),
        w_up=w_ffn_up.astype(BF16), conv_w=ffn_conv_w, conv_b=ffn_conv_b.reshape(1, -1),
        w_down=w_ffn_down.astype(BF16), w_ple_gate=w_ple_gate.astype(BF16), w_ple_proj=w_ple_proj.astype(BF16))


def _rope_tables(pos):
    half = MLA_ROPE_DIM // 2
    inv_freq = ROPE_THETA ** (-jnp.arange(half, dtype=F32) / half)
    ang = pos.astype(F32)[:, None] * inv_freq[None, :]
    cos, sin = jnp.cos(ang), jnp.sin(ang)
    t = pos.shape[0]
    pad = jnp.zeros((t, MLA_PAIR_W - LANE - 2 * MLA_ROPE_DIM), F32)
    cos_t = jnp.concatenate([jnp.ones((t, LANE), F32), cos, cos, cos, cos, pad], axis=1)
    sin_t = jnp.concatenate([jnp.zeros((t, LANE), F32), -sin, sin, -sin, sin, pad], axis=1)
    return cos_t, sin_t


def _alibi_slopes():
    s = 2.0 ** (-8.0 * np.arange(1, N_DIFF_HEADS + 1) / N_DIFF_HEADS)
    return jnp.asarray(np.broadcast_to(s.reshape(N_DIFF_HEADS // 2, 2, 1), (N_DIFF_HEADS // 2, 2, LANE)), dtype=F32)


def _tile_rows(n, pref):
    t = min(n, pref)
    assert n % t == 0
    return t


def _layer(h, p, lw, cos, sin, cache, conv_left, lam_init, *, batch, seq, past, tq):
    n = h.shape[0]
    tm = _tile_rows(n, 512)
    (fox_k, fox_v, logf, ckv, krope, diff_k, diff_v,
     fox_bf, diff_bf, mla_q, mla_k, mla_v) = _proj_call(h, lw, cos, sin, tm)
    rows = (fox_k, fox_v, logf, ckv, krope, diff_k, diff_v)
    b3 = lambda a: a.reshape(batch, seq, a.shape[-1])
    diff_extras = [_alibi_slopes(), lw["lams"], lw["subln"]]
    diff_specs = [pl.BlockSpec((1, 2, LANE), lambda bb, hp, i: (hp, 0, 0)),
                  pl.BlockSpec((4, DIFF_QK_DIM), lambda bb, hp, i: (0, 0)),
                  pl.BlockSpec((1, LANE), lambda bb, hp, i: (0, 0))]
    fb, db = b3(fox_bf), b3(diff_bf)
    if cache is None:
        common = dict(tq=tq, tkd=tq, qoff=0, blocked=True)
        o_a = _attn_call(_fox_kernel, "fox_attn", fb, 0, fb, 3, _forget_bias_call(b3(logf)),
                         _values_t(fb[..., 2 * FOX_WIDTH:], tq), [], [],
                         npairs=N_FOX_HEADS // 2, qw=LANE, chains=2, **common)
        o_b = _attn_call(_mla_kernel, "mla_attn", b3(mla_q), 0, b3(mla_k), 0, None, _values_t(b3(mla_v), tq), [], [],
                         npairs=N_MLA_PAIRS, qw=MLA_PAIR_W, chains=2, **common)
        o_c = _attn_call(_diff_kernel, "diff_attn", db, 0, db, 2, _alibi_bias_block(seq),
                         _values_t(db[..., 2 * DIFF_WIDTH:], tq), diff_extras, diff_specs,
                         npairs=N_DIFF_HEADS // 2, qw=LANE, chains=4, lam_init=lam_init, **common)
    else:
        c_fox_k, c_fox_v, c_logf, c_ckv, c_krope, c_diff_k, c_diff_v = cache
        tk = -(-(past + seq) // LANE) * LANE
        padk = lambda a: jnp.pad(a, ((0, 0), (0, tk - past - seq), (0, 0)))
        cat = lambda old, new: padk(jnp.concatenate([old.astype(new.dtype), new], axis=1))
        flat = lambda a: a.reshape(batch, past, -1)
        padq = lambda a: jnp.pad(a, ((0, 0), (0, tq - seq), (0, 0)))
        kf = cat(flat(c_fox_k), fb[..., FOX_WIDTH:2 * FOX_WIDTH])
        vf = cat(flat(c_fox_v), fb[..., 2 * FOX_WIDTH:])
        kd = cat(flat(c_diff_k), db[..., DIFF_WIDTH:2 * DIFF_WIDTH])
        vd = cat(flat(c_diff_v), db[..., 2 * DIFF_WIDTH:])
        kn_c, v_c = _kvup_call(c_ckv.reshape(batch * past, MLA_KV_RANK), lw["w_ukv"], _tile_rows(batch * past, 1024))
        kr_c = c_krope.astype(BF16)
        krp_c = jnp.concatenate([kr_c, kr_c, jnp.zeros((batch, past, LANE - 2 * MLA_ROPE_DIM), BF16)], axis=-1)
        km_c = jnp.concatenate([kn_c.reshape(batch, past, N_MLA_PAIRS, LANE),
                                jnp.broadcast_to(krp_c[:, :, None, :], (batch, past, N_MLA_PAIRS, LANE))],
                               axis=-1).reshape(batch, past, MLA_QK_W)
        km = cat(km_c, b3(mla_k))
        vm = cat(v_c.reshape(batch, past, MLA_WIDTH), b3(mla_v))
        common = dict(tq=tq, tkd=tk, qoff=past, blocked=False)
        o_a = _attn_call(_fox_kernel, "fox_attn_s", padq(fb[..., :FOX_WIDTH]), 0, kf, 0,
                         _forget_bias_call(cat(c_logf, b3(logf))), _values_t(vf, tk), [], [],
                         npairs=N_FOX_HEADS // 2, qw=LANE, chains=2, **common)
        o_b = _attn_call(_mla_kernel, "mla_attn_s", padq(b3(mla_q)), 0, km, 0, None, _values_t(vm, tk), [], [],
                         npairs=N_MLA_PAIRS, qw=MLA_PAIR_W, chains=2, **common)
        o_c = _attn_call(_diff_kernel, "diff_attn_s", padq(db[..., :DIFF_WIDTH]), 0, kd, 0, _alibi_bias_block(tk),
                         _values_t(vd, tk), diff_extras, diff_specs,
                         npairs=N_DIFF_HEADS // 2, qw=LANE, chains=4, lam_init=lam_init, **common)
        o_a, o_b, o_c = o_a[:, :seq], o_b[:, :seq], o_c[:, :seq]
    f2 = lambda a: a.reshape(n, a.shape[-1])
    h = _wout_call(f2(o_a), f2(o_b), f2(o_c), h, lw["w_out"], lw["g_mix_post"], tm)
    h, conv_state = _ffn_call(h, lw, conv_left, seq, tm, 512)
    h = _ple_call(h, p, lw, tm)
    return h, rows, conv_state


def kernel(x_prompt, x_sample, cache_fox_k, cache_fox_v, cache_fox_logf, cache_mla_ckv, cache_mla_krope, cache_diff_k, cache_diff_v, state_ffn_conv, p_prompt, p_sample, w_in, b_forget, mla_q_norm, w_mla_uq, mla_kv_norm, w_mla_uk, w_mla_uv, diff_lambda_q1, diff_lambda_k1, diff_lambda_q2, diff_lambda_k2, diff_subln, w_out, norm_mix_pre, norm_mix_post, norm_ffn_pre, norm_ffn_post, norm_ple_pre, norm_ple_post, w_ffn_up, ffn_conv_w, ffn_conv_b, w_ffn_down, w_ple_gate, w_ple_proj):
    bp, sp, _ = x_prompt.shape
    bs, ts, _ = x_sample.shape
    depth, _, past = cache_fox_k.shape[:3]
    assert past % CHUNK == 0 and ts <= CHUNK and sp % LANE == 0
    tq = _tile_rows(sp, 512)
    tms = _tile_rows(bs * ts, 512)

    cos_p, sin_p = _rope_tables(jnp.arange(sp))
    cos_s, sin_s = _rope_tables(past + jnp.arange(ts))
    cos_s, sin_s = jnp.tile(cos_s, (tms // ts, 1)), jnp.tile(sin_s, (tms // ts, 1))

    hp = x_prompt.reshape(bp * sp, D_MODEL)
    hs = x_sample.reshape(bs * ts, D_MODEL)
    rows_p, rows_s, conv_p, conv_s = [], [], [], []
    for l in range(depth):
        lams = jnp.stack([diff_lambda_q1[l], diff_lambda_k1[l], diff_lambda_q2[l], diff_lambda_k2[l]], axis=0)
        lw = _pack_layer(w_in[l], b_forget[l], mla_q_norm[l], w_mla_uq[l], mla_kv_norm[l], w_mla_uk[l], w_mla_uv[l],
                         lams, diff_subln[l], w_out[l], norm_mix_pre[l], norm_mix_post[l], norm_ffn_pre[l],
                         norm_ffn_post[l], norm_ple_pre[l], norm_ple_post[l], w_ffn_up[l], ffn_conv_w[l],
                         ffn_conv_b[l], w_ffn_down[l], w_ple_gate[l], w_ple_proj[l])
        lam_init = 0.8 - 0.6 * math.exp(-0.3 * l)
        hp, rp, cp = _layer(hp, p_prompt[l].reshape(bp * sp, PLE_DIM), lw, cos_p, sin_p, None,
                            jnp.zeros((bp, CONV_WIDTH - 1, D_FF), F32), lam_init,
                            batch=bp, seq=sp, past=0, tq=tq)
        cache_l = (cache_fox_k[l], cache_fox_v[l], cache_fox_logf[l], cache_mla_ckv[l], cache_mla_krope[l],
                   cache_diff_k[l], cache_diff_v[l])
        hs, rs, cs = _layer(hs, p_sample[l].reshape(bs * ts, PLE_DIM), lw, cos_s, sin_s, cache_l,
                            state_ffn_conv[l], lam_init, batch=bs, seq=ts, past=past, tq=LANE)
        rows_p.append(rp)
        rows_s.append(rs)
        conv_p.append(cp)
        conv_s.append(cs)

    def stack_rows(per_layer, batch, seq):
        fk, fv, lf, ckv, kr, dk, dv = (jnp.stack(a, axis=0) for a in zip(*per_layer))
        lead = (depth, batch, seq)
        return (fk.reshape(lead + (N_FOX_HEADS, HEAD_DIM)), fv.reshape(lead + (N_FOX_HEADS, HEAD_DIM)),
                lf.reshape(lead + (N_FOX_HEADS,)), ckv.reshape(lead + (MLA_KV_RANK,)),
                kr.reshape(lead + (MLA_ROPE_DIM,)), dk.reshape(lead + (N_DIFF_HEADS, 2 * DIFF_QK_DIM)),
                dv.reshape(lead + (N_DIFF_HEADS, DIFF_V_DIM)))

    out_p = stack_rows(rows_p, bp, sp)
    out_s = stack_rows(rows_s, bs, ts)
    return ((hp.reshape(bp, sp, D_MODEL), hs.reshape(bs, ts, D_MODEL)) + out_p + (jnp.stack(conv_p, axis=0),)
            + out_s + (jnp.stack(conv_s, axis=0),))
```

```python
import functools
import math

import jax
import jax.numpy as jnp
import numpy as np
from jax import lax
from jax.experimental import pallas as pl
from jax.experimental.pallas import tpu as pltpu

F32 = jnp.float32
BF16 = jnp.bfloat16

D_MODEL = 1024
HEAD_DIM = 64
N_FOX_HEADS = 6
N_MLA_HEADS = 6
N_DIFF_HEADS = 4
MLA_Q_RANK = 384
MLA_KV_RANK = 256
MLA_NOPE_DIM = 64
MLA_ROPE_DIM = 32
MLA_V_DIM = 64
DIFF_QK_DIM = 32
DIFF_V_DIM = 64
FOX_WIDTH = N_FOX_HEADS * HEAD_DIM
MLA_WIDTH = N_MLA_HEADS * MLA_V_DIM
DIFF_WIDTH = N_DIFF_HEADS * DIFF_V_DIM
D_FF = 4 * D_MODEL
CONV_WIDTH = 3
PLE_DIM = 256
CHUNK = 64
ROPE_THETA = 10000.0
RMS_EPS = 1e-6
NEG_INF = -1e30
LOG2E = math.log2(math.e)

OFF_FOX_Q = 0
OFF_FOX_F = 3 * FOX_WIDTH
OFF_MLA_CQ = OFF_FOX_F + N_FOX_HEADS
OFF_MLA_CKV = OFF_MLA_CQ + MLA_Q_RANK
OFF_MLA_KR = OFF_MLA_CKV + MLA_KV_RANK
OFF_DIFF_Q = OFF_MLA_KR + MLA_ROPE_DIM
IN_WIDTH = OFF_DIFF_Q + 2 * N_DIFF_HEADS * 2 * DIFF_QK_DIM + DIFF_WIDTH

LANE = 128
V7X_VMEM_BYTES = 64 * 1024 * 1024
VMEM_LIMIT = (V7X_VMEM_BYTES * 7) // 8

C_FOX = 0
C_DIFF = C_FOX + 3 * FOX_WIDTH
C_CQ = C_DIFF + 3 * DIFF_WIDTH
C_CKV = C_CQ + MLA_Q_RANK
C_KR = C_CKV + MLA_KV_RANK
C_F = C_KR + 2 * LANE
W_ALL = C_F + LANE
N_MLA_PAIRS = N_MLA_HEADS // 2
MLA_PAIR_W = 2 * LANE
MLA_QK_W = N_MLA_PAIRS * MLA_PAIR_W

FOX_QSCALE = HEAD_DIM ** -0.5 * LOG2E
MLA_QSCALE = (MLA_NOPE_DIM + MLA_ROPE_DIM) ** -0.5 * LOG2E
DIFF_QSCALE = DIFF_QK_DIM ** -0.5 * LOG2E


def _cparams(sem):
    return pltpu.CompilerParams(dimension_semantics=sem, vmem_limit_bytes=VMEM_LIMIT)


def _rms(x, g):
    return x * lax.rsqrt(jnp.mean(x * x, axis=-1, keepdims=True) + RMS_EPS) * g


def _dot(a, b):
    return jnp.dot(a, b, preferred_element_type=F32)


def _dot_nt(a, b):
    return lax.dot_general(a, b, (((1,), (1,)), ((), ())), preferred_element_type=F32)


def _proj_kernel(h_ref, gpre_ref, w_ref, bf_ref, gq_ref, gkv_ref, wuq_ref, wukv_ref, cos_ref, sin_ref,
                 foxk_ref, foxv_ref, logf_ref, ckv_ref, krope_ref, diffk_ref, diffv_ref,
                 foxbf_ref, diffbf_ref, mlaq_ref, mlak_ref, mlav_ref):
    xn = _rms(h_ref[...], gpre_ref[...]).astype(BF16)
    cos = cos_ref[...]
    sin = sin_ref[...]

    z = _dot(xn, w_ref[:, C_FOX:C_FOX + 3 * FOX_WIDTH])
    foxbf_ref[:, :FOX_WIDTH] = (z[:, :FOX_WIDTH] * FOX_QSCALE).astype(BF16)
    foxbf_ref[:, FOX_WIDTH:] = z[:, FOX_WIDTH:].astype(BF16)
    foxk_ref[...] = z[:, FOX_WIDTH:2 * FOX_WIDTH]
    foxv_ref[...] = z[:, 2 * FOX_WIDTH:3 * FOX_WIDTH]

    z = _dot(xn, w_ref[:, C_DIFF:C_DIFF + 3 * DIFF_WIDTH])
    diffbf_ref[:, :DIFF_WIDTH] = (z[:, :DIFF_WIDTH] * DIFF_QSCALE).astype(BF16)
    diffbf_ref[:, DIFF_WIDTH:] = z[:, DIFF_WIDTH:].astype(BF16)
    diffk_ref[...] = z[:, DIFF_WIDTH:2 * DIFF_WIDTH]
    diffv_ref[...] = z[:, 2 * DIFF_WIDTH:3 * DIFF_WIDTH]

    cq = _rms(_dot(xn, w_ref[:, C_CQ:C_CQ + MLA_Q_RANK]), gq_ref[...]).astype(BF16)
    q2 = _dot(cq, wuq_ref[...])
    for p in range(N_MLA_PAIRS):
        lo = p * MLA_PAIR_W
        roped = q2[:, lo:lo + MLA_PAIR_W] * cos + q2[:, MLA_QK_W + lo:MLA_QK_W + lo + MLA_PAIR_W] * sin
        mlaq_ref[:, lo:lo + MLA_PAIR_W] = (roped * MLA_QSCALE).astype(BF16)

    ckv = _rms(_dot(xn, w_ref[:, C_CKV:C_CKV + MLA_KV_RANK]), gkv_ref[...])
    ckv_ref[...] = ckv
    kv = _dot(ckv.astype(BF16), wukv_ref[...])
    mlav_ref[...] = kv[:, MLA_WIDTH:2 * MLA_WIDTH].astype(BF16)
    zkr = _dot(xn, w_ref[:, C_KR:C_KR + 2 * LANE])
    krp = zkr[:, :LANE] * cos[:, LANE:] + zkr[:, LANE:] * sin[:, LANE:]
    krope_ref[...] = krp[:, :MLA_ROPE_DIM]
    krp16 = krp.astype(BF16)
    for p in range(N_MLA_PAIRS):
        lo = p * MLA_PAIR_W
        mlak_ref[:, lo:lo + LANE] = kv[:, p * LANE:(p + 1) * LANE].astype(BF16)
        mlak_ref[:, lo + LANE:lo + 2 * LANE] = krp16

    zf = _dot(xn, w_ref[:, C_F:C_F + LANE]) + bf_ref[...]
    logf = -(jnp.maximum(-zf, 0.0) + jnp.log1p(jnp.exp(-jnp.abs(zf))))
    logf_ref[...] = logf[:, :N_FOX_HEADS]


def _proj_call(h, lw, cos, sin, tm):
    n = h.shape[0]
    ntab = cos.shape[0] // tm
    row = lambda i: (i, 0)
    full = lambda i: (0, 0)
    tab = lambda i: (i % ntab, 0)
    in_specs = [
        pl.BlockSpec((tm, D_MODEL), row),
        pl.BlockSpec((1, D_MODEL), full),
        pl.BlockSpec((D_MODEL, W_ALL), full),
        pl.BlockSpec((1, LANE), full),
        pl.BlockSpec((1, MLA_Q_RANK), full),
        pl.BlockSpec((1, MLA_KV_RANK), full),
        pl.BlockSpec((MLA_Q_RANK, 2 * MLA_QK_W), full),
        pl.BlockSpec((MLA_KV_RANK, 2 * MLA_WIDTH), full),
        pl.BlockSpec((tm, MLA_PAIR_W), tab),
        pl.BlockSpec((tm, MLA_PAIR_W), tab),
    ]
    widths = [(FOX_WIDTH, F32), (FOX_WIDTH, F32), (N_FOX_HEADS, F32), (MLA_KV_RANK, F32), (MLA_ROPE_DIM, F32),
              (DIFF_WIDTH, F32), (DIFF_WIDTH, F32),
              (3 * FOX_WIDTH, BF16), (3 * DIFF_WIDTH, BF16), (MLA_QK_W, BF16), (MLA_QK_W, BF16), (MLA_WIDTH, BF16)]
    out_shape = [jax.ShapeDtypeStruct((n, w), dt) for w, dt in widths]
    out_specs = [pl.BlockSpec((tm, w), row) for w, _ in widths]
    return pl.pallas_call(
        _proj_kernel, grid=(n // tm,), in_specs=in_specs, out_specs=out_specs, out_shape=out_shape,
        compiler_params=_cparams(("parallel",)), name="proj",
    )(h, lw["g_mix_pre"], lw["w_all"], lw["b_f"], lw["g_q"], lw["g_kv"], lw["w_uq2"], lw["w_ukv"], cos, sin)


def _kvup_kernel(ckv_ref, wukv_ref, kn_ref, v_ref):
    kv = _dot(ckv_ref[...].astype(BF16), wukv_ref[...])
    kn_ref[...] = kv[:, :MLA_WIDTH].astype(BF16)
    v_ref[...] = kv[:, MLA_WIDTH:].astype(BF16)


def _kvup_call(ckv, w_ukv, tm):
    n = ckv.shape[0]
    row = lambda i: (i, 0)
    return pl.pallas_call(
        _kvup_kernel, grid=(n // tm,),
        in_specs=[pl.BlockSpec((tm, MLA_KV_RANK), row), pl.BlockSpec((MLA_KV_RANK, 2 * MLA_WIDTH), lambda i: (0, 0))],
        out_specs=[pl.BlockSpec((tm, MLA_WIDTH), row), pl.BlockSpec((tm, MLA_WIDTH), row)],
        out_shape=[jax.ShapeDtypeStruct((n, MLA_WIDTH), BF16)] * 2,
        compiler_params=_cparams(("parallel",)), name="kvup",
    )(ckv, w_ukv)


AUG_STRIDE = 8


def _keep_bf16_bits(x):
    bits = lax.bitcast_convert_type(x, jnp.uint32) & jnp.uint32(0xFFFF0000)
    return lax.bitcast_convert_type(bits, F32)


def _split3(x):
    x1 = _keep_bf16_bits(x)
    r = x - x1
    x2 = _keep_bf16_bits(r)
    return x1, x2, r - x2


def _place3(x1, x2, x3):
    lane = lax.broadcasted_iota(jnp.int32, (1, LANE), 1)
    return jnp.where(lane < AUG_STRIDE, x1,
                     jnp.where(lane < 2 * AUG_STRIDE, pltpu.roll(x2, AUG_STRIDE, 1),
                               pltpu.roll(x3, 2 * AUG_STRIDE, 1))).astype(BF16)


def _aug_selector(h, rows):
    lane = lax.broadcasted_iota(jnp.int32, (rows, LANE), 1)
    hit = (lane == h) | (lane == AUG_STRIDE + h) | (lane == 2 * AUG_STRIDE + h)
    return jnp.where(hit, 1.0, 0.0).astype(BF16)


def _forget_bias_kernel(x_ref, tri_ref, o_ref, carry_sc, *, tb):
    @pl.when(pl.program_id(1) == 0)
    def _():
        carry_sc[...] = jnp.zeros_like(carry_sc)

    terms = jnp.concatenate(_split3(x_ref[0]), axis=1).astype(BF16)
    sums = _dot(tri_ref[...], terms)
    cs = carry_sc[0:1, :] + sums[:, :LANE] + sums[:, LANE:2 * LANE] + sums[:, 2 * LANE:]
    carry_sc[...] = jnp.broadcast_to(cs[tb - 1:tb, :], carry_sc.shape)
    o_ref[0] = _place3(*_split3(cs * (-LOG2E)))


def _forget_bias_call(logf):
    b, t, hh = logf.shape
    tb = max(d for d in range(LANE, 4 * LANE + 1, LANE) if t % d == 0)
    x = jnp.pad(logf, ((0, 0), (0, 0), (0, LANE - hh)))
    tri = jnp.tril(jnp.ones((tb, tb), BF16))
    blk = pl.BlockSpec((1, tb, LANE), lambda i, j: (i, j, 0))
    return pl.pallas_call(
        functools.partial(_forget_bias_kernel, tb=tb), grid=(b, t // tb),
        in_specs=[blk, pl.BlockSpec((tb, tb), lambda i, j: (0, 0))], out_specs=blk,
        out_shape=jax.ShapeDtypeStruct((b, t, LANE), BF16),
        scratch_shapes=[pltpu.VMEM((8, LANE), F32)],
        compiler_params=_cparams(("parallel", "arbitrary")), name="forget_bias",
    )(x, tri)


def _alibi_bias_block(t_k):
    slopes = 2.0 ** (-8.0 * np.arange(1, N_DIFF_HEADS + 1) / N_DIFF_HEADS)
    b = jnp.zeros((t_k, LANE), F32).at[:, :N_DIFF_HEADS].set(
        jnp.arange(t_k, dtype=F32)[:, None] * jnp.asarray(slopes * LOG2E, F32)[None, :])
    x1, x2, x3 = _split3(b)
    lane = jnp.arange(LANE)[None, :]
    placed = jnp.where(lane < AUG_STRIDE, x1, jnp.where(lane < 2 * AUG_STRIDE, jnp.roll(x2, AUG_STRIDE, 1),
                                                        jnp.roll(x3, 2 * AUG_STRIDE, 1)))
    return placed.astype(BF16)[None]


ROW_CHUNK = CHUNK
STRIP = 4 * LANE
ONES_ROWS = 16


def _softmax_probs(s_ref, p_ref, m_ref, alpha_ref, c, mask, tk, tq):
    sub = ROW_CHUNK // 8
    sw = min(STRIP, tq)
    for c0 in range(0, tq, sw):
        cols = slice(c0, c0 + sw)
        chunks = [(r0, "all" if mask is None else mask[0](r0, c0, sw)) for r0 in range(0, tk, ROW_CHUNK)]

        def logits(r0, vis):
            x = s_ref[c, r0:r0 + ROW_CHUNK, cols]
            return x if vis == "all" else mask[1](r0, c0, x)

        mrun = jnp.full((8, sw), NEG_INF, F32)
        for r0, vis in chunks:
            if vis != "none":
                mrun = jnp.maximum(mrun, jnp.max(logits(r0, vis).reshape(sub, 8, sw), axis=0))
        m_old = m_ref[c, :, cols]
        m_new = jnp.maximum(m_old, jnp.max(mrun, axis=0, keepdims=True))
        for r0, vis in chunks:
            if vis == "none":
                p_ref[c, r0:r0 + ROW_CHUNK, cols] = jnp.zeros((ROW_CHUNK, sw), BF16)
            else:
                p_ref[c, r0:r0 + ROW_CHUNK, cols] = jnp.exp2(logits(r0, vis) - m_new).astype(BF16)
        m_ref[c, :, cols] = m_new
        alpha_ref[c, :, cols] = jnp.exp2(m_old - m_new)


def _apply_values(p_ref, alpha_ref, acc_ref, c, vt):
    acc_ref[c] = alpha_ref[c] * acc_ref[c] + _dot(vt, p_ref[c])


def _causal_mask(qoff):
    def visibility(r0, c0, sw):
        if r0 + ROW_CHUNK - 1 <= c0 + qoff:
            return "all"
        return "none" if r0 > c0 + sw - 1 + qoff else "some"

    def apply(r0, c0, x):
        kk, qq = _key_query_iota(*x.shape)
        return jnp.where(kk - qq <= c0 + qoff - r0, x, NEG_INF)
    return visibility, apply


def _chunk_visibility(r0, c0, sw, qoff):
    if r0 // CHUNK <= (c0 + qoff) // CHUNK:
        return "all"
    return "none" if r0 // CHUNK > (c0 + sw - 1 + qoff) // CHUNK else "some"


def _chunk_mask(r0, c0, sw, qoff):
    qq = lax.broadcasted_iota(jnp.int32, (1, sw), 1)
    return r0 // CHUNK <= (qq + (c0 + qoff)) // CHUNK


def _init_states(m_ref, acc_ref):
    m_ref[...] = jnp.full(m_ref.shape, NEG_INF, F32)
    acc_ref[...] = jnp.zeros(acc_ref.shape, F32)


def _normalized(acc_ref, c, dv):
    return acc_ref[c, :dv, :] / acc_ref[c, dv:dv + 1, :]


def _key_query_iota(tk, tq):
    return (lax.broadcasted_iota(jnp.int32, (tk, tq), 0), lax.broadcasted_iota(jnp.int32, (tk, tq), 1))


def _store_pair(o_ref, o0, o1, row_scale=None):
    out = jnp.concatenate([o0, o1], axis=0).T
    if row_scale is not None:
        out = out * row_scale
    o_ref[0] = out.astype(BF16)


def _block_start(j, size):
    return j * size if isinstance(j, int) else pl.multiple_of(j * size, size)


N_BUF = 3


def _attend(i, blocked, qk_fn, probs_fn, values_fn):
    if not blocked:
        qk_fn(0, 0)
        probs_fn(0, 0, True)
        values_fn(0, 0)
        return
    lead = i % N_BUF

    def single(j, carry):
        qk_fn(N_BUF - 1, j)
        probs_fn(N_BUF - 1, j, False)
        values_fn(N_BUF - 1, j)
        return carry

    lax.fori_loop(0, lead, single, 0)
    qk_fn(0, lead)
    qk_fn(1, jnp.minimum(lead + 1, i))

    def rotate(t, carry):
        b0 = lead + N_BUF * t
        for u in range(N_BUF):
            qk_fn((u + 2) % N_BUF, jnp.minimum(b0 + u + 2, i))
            if u > 0:
                values_fn(u - 1, b0 + u - 1)
            probs_fn(u, b0 + u, False)
        values_fn(N_BUF - 1, b0 + N_BUF - 1)
        return carry

    lax.fori_loop(0, (i - lead) // N_BUF, rotate, 0)
    probs_fn(0, i, True)
    values_fn(0, i)


def _vt_rows(vt_ref, j, e, dv):
    return vt_ref[0, 0, j, e * (dv + ONES_ROWS):(e + 1) * (dv + ONES_ROWS), :]


def _split_scratch(scratch):
    nbuf = (len(scratch) - 3) // 2
    return (scratch[:nbuf], scratch[nbuf:2 * nbuf]) + tuple(scratch[2 * nbuf:])


def _fox_kernel(q_ref, k_ref, aug_ref, vt_ref, o_ref, *scratch, tq, tkd, qoff, blocked):
    s_sc, p_sc, m_sc, alpha_sc, acc_sc = _split_scratch(scratch)
    hp, i = pl.program_id(1), pl.program_id(2)
    _init_states(m_sc, acc_sc)
    q = q_ref[0]
    lane = lax.broadcasted_iota(jnp.int32, (1, LANE), 1)
    qs = [jnp.concatenate([jnp.where((lane >= e * HEAD_DIM) & (lane < (e + 1) * HEAD_DIM), q, jnp.zeros_like(q)),
                           _aug_selector(2 * hp + e, tq)], axis=1) for e in range(2)]

    def qk_fn(buf, j):
        rows = pl.ds(_block_start(j, tq), tkd)
        k = jnp.concatenate([k_ref[0, rows, :], aug_ref[0, rows, :]], axis=1)
        for e in range(2):
            s_sc[buf][e] = _dot_nt(k, qs[e])

    def probs_fn(buf, j, diag):
        for e in range(2):
            _softmax_probs(s_sc[buf], p_sc[buf], m_sc, alpha_sc, e, _causal_mask(qoff) if diag else None,
                           tkd, tq)

    def values_fn(buf, j):
        for e in range(2):
            _apply_values(p_sc[buf],alpha_sc, acc_sc, e, _vt_rows(vt_ref, j, e, HEAD_DIM))

    _attend(i, blocked, qk_fn, probs_fn, values_fn)
    _store_pair(o_ref, *(_normalized(acc_sc, e, HEAD_DIM) for e in range(2)))


def _mla_kernel(q_ref, k_ref, vt_ref, o_ref, *scratch, tq, tkd, qoff, blocked):
    s_sc, p_sc, m_sc, alpha_sc, acc_sc = _split_scratch(scratch)
    i = pl.program_id(2)
    _init_states(m_sc, acc_sc)
    q = q_ref[0]
    lane2 = lax.broadcasted_iota(jnp.int32, (1, MLA_PAIR_W), 1)
    sel = []
    for e in range(2):
        nope = (lane2 >= e * MLA_NOPE_DIM) & (lane2 < (e + 1) * MLA_NOPE_DIM)
        rope = (lane2 >= LANE + e * MLA_ROPE_DIM) & (lane2 < LANE + (e + 1) * MLA_ROPE_DIM)
        sel.append(jnp.where(nope | rope, q, jnp.zeros_like(q)))

    def qk_fn(buf, j):
        k = k_ref[0, pl.ds(_block_start(j, tq), tkd), :]
        for e in range(2):
            s_sc[buf][e] = _dot_nt(k, sel[e])

    def probs_fn(buf, j, diag):
        mask = (lambda r0, c0, sw: _chunk_visibility(r0, c0, sw, qoff),
                lambda r0, c0, x: jnp.where(_chunk_mask(r0, c0, x.shape[1], qoff), x, NEG_INF))
        for e in range(2):
            _softmax_probs(s_sc[buf], p_sc[buf], m_sc, alpha_sc, e, mask if diag else None, tkd, tq)

    def values_fn(buf, j):
        for e in range(2):
            _apply_values(p_sc[buf],alpha_sc, acc_sc, e, _vt_rows(vt_ref, j, e, MLA_V_DIM))

    _attend(i, blocked, qk_fn, probs_fn, values_fn)
    _store_pair(o_ref, *(_normalized(acc_sc, e, MLA_V_DIM) for e in range(2)))


def _diff_kernel(q_ref, k_ref, aug_ref, vt_ref, slope_ref, lam_ref, subln_ref, o_ref, *scratch,
                 tq, tkd, qoff, blocked, lam_init):
    s_sc, p_sc, m_sc, alpha_sc, acc_sc = _split_scratch(scratch)
    hp, i = pl.program_id(1), pl.program_id(2)
    _init_states(m_sc, acc_sc)
    q = q_ref[0]
    lane = lax.broadcasted_iota(jnp.int32, (1, LANE), 1)
    sel = [jnp.concatenate([jnp.where((lane >= (2 * e + t) * DIFF_QK_DIM) & (lane < (2 * e + t + 1) * DIFF_QK_DIM),
                                      q, jnp.zeros_like(q)), _aug_selector(2 * hp + e, tq)], axis=1)
           for e in range(2) for t in range(2)]
    slopes = [slope_ref[0, e:e + 1, 0:1] * LOG2E for e in range(2)]

    def qk_fn(buf, j):
        rows = pl.ds(_block_start(j, tq), tkd)
        k = jnp.concatenate([k_ref[0, rows, :], aug_ref[0, rows, :]], axis=1)
        for c in range(4):
            s_sc[buf][c] = _dot_nt(k, sel[c])

    def values_fn(buf, j):
        for c in range(4):
            _apply_values(p_sc[buf],alpha_sc, acc_sc, c, _vt_rows(vt_ref, j, c // 2, DIFF_V_DIM))

    def probs_fn(buf, j, diag):
        for c in range(4):
            e = c // 2

            def visibility(r0, c0, sw):
                vis = _chunk_visibility(r0, c0, sw, qoff)
                return "some" if vis == "all" and r0 + ROW_CHUNK - 1 > c0 + qoff else vis

            def apply(r0, c0, x, e=e):
                kk, qq = _key_query_iota(*x.shape)
                ahead = jnp.maximum(kk - qq + (r0 - c0 - qoff), 0).astype(F32)
                return jnp.where(_chunk_mask(r0, c0, x.shape[1], qoff), x - (2.0 * slopes[e]) * ahead, NEG_INF)
            _softmax_probs(s_sc[buf], p_sc[buf], m_sc, alpha_sc, c, (visibility, apply) if diag else None,
                           tkd, tq)

    _attend(i, blocked, qk_fn, probs_fn, values_fn)
    res = [_normalized(acc_sc, c, DIFF_V_DIM) for c in range(4)]
    lq = lam_ref[...]
    lam = (jnp.exp(jnp.sum(lq[0:1] * lq[1:2], axis=-1, keepdims=True))
           - jnp.exp(jnp.sum(lq[2:3] * lq[3:4], axis=-1, keepdims=True)) + lam_init)
    outs = []
    for e in range(2):
        o = res[2 * e] - lam * res[2 * e + 1]
        outs.append(o * lax.rsqrt(jnp.mean(o * o, axis=0, keepdims=True) + RMS_EPS))
    _store_pair(o_ref, outs[0], outs[1], subln_ref[...] * (1.0 - lam_init))


def _attn_call(kernel, name, q, q_cb, k, k_cb, aug, vt, extras, extra_specs, *, npairs, qw, tq, tkd, qoff, blocked,
               chains, **kw):
    b, t_q = q.shape[0], q.shape[1]
    t_k = k.shape[1]
    acc_rows = LANE // 2 + ONES_ROWS
    nbuf = N_BUF if blocked else 1
    scratch = ([pltpu.VMEM((chains, tkd, tq), F32)] * nbuf + [pltpu.VMEM((chains, tkd, tq), BF16)] * nbuf
               + [pltpu.VMEM((chains, 1, tq), F32), pltpu.VMEM((chains, 1, tq), F32),
                  pltpu.VMEM((chains, acc_rows, tq), F32)])
    in_specs = [pl.BlockSpec((1, tq, qw), lambda bb, hp, i: (bb, i, q_cb + hp)),
                pl.BlockSpec((1, t_k, qw), lambda bb, hp, i: (bb, 0, k_cb + hp))]
    args = [q, k]
    if aug is not None:
        per_batch = aug.shape[0] > 1
        in_specs.append(pl.BlockSpec((1, t_k, LANE), lambda bb, hp, i: (bb if per_batch else 0, 0, 0)))
        args.append(aug)
    in_specs.append(pl.BlockSpec((1, 1) + vt.shape[2:], lambda bb, hp, i: (bb, hp, 0, 0, 0)))
    return pl.pallas_call(
        functools.partial(kernel, tq=tq, tkd=tkd, qoff=qoff, blocked=blocked, **kw),
        grid=(b, npairs, t_q // tq), in_specs=in_specs + extra_specs,
        out_specs=pl.BlockSpec((1, tq, LANE), lambda bb, hp, i: (bb, i, hp)),
        out_shape=jax.ShapeDtypeStruct((b, t_q, npairs * LANE), BF16), scratch_shapes=scratch,
        compiler_params=_cparams(("parallel", "parallel", "arbitrary")), name=name,
    )(*args, vt, *extras)


def _values_t(v, tkb):
    b, t, w = v.shape
    dv = LANE // 2
    vt = v.reshape(b, t // tkb, tkb, w // LANE, 2, dv).transpose(0, 3, 1, 4, 5, 2)
    ones = jnp.ones(vt.shape[:4] + (ONES_ROWS, tkb), v.dtype)
    return jnp.concatenate([vt, ones], axis=4).reshape(b, w // LANE, t // tkb, 2 * (dv + ONES_ROWS), tkb)


def _wout_kernel(oa_ref, ob_ref, oc_ref, h_ref, w_ref, g_ref, o_ref):
    y = (_dot(oa_ref[...], w_ref[0:FOX_WIDTH, :])
         + _dot(ob_ref[...], w_ref[FOX_WIDTH:FOX_WIDTH + MLA_WIDTH, :])
         + _dot(oc_ref[...], w_ref[FOX_WIDTH + MLA_WIDTH:, :]))
    o_ref[...] = h_ref[...] + _rms(y, g_ref[...])


def _wout_call(oa, ob, oc, h, w_out, g, tm):
    n = h.shape[0]
    row = lambda i: (i, 0)
    full = lambda i: (0, 0)
    return pl.pallas_call(
        _wout_kernel, grid=(n // tm,),
        in_specs=[pl.BlockSpec((tm, FOX_WIDTH), row), pl.BlockSpec((tm, MLA_WIDTH), row),
                  pl.BlockSpec((tm, DIFF_WIDTH), row), pl.BlockSpec((tm, D_MODEL), row),
                  pl.BlockSpec((D_MODEL, D_MODEL), full), pl.BlockSpec((1, D_MODEL), full)],
        out_specs=pl.BlockSpec((tm, D_MODEL), row),
        out_shape=jax.ShapeDtypeStruct((n, D_MODEL), F32),
        compiler_params=_cparams(("parallel",)), name="wout",
    )(oa, ob, oc, h, w_out, g)


FFN_HALVES = 2


def _ffn_kernel(h_ref, gpre_ref, wg_ref, wv_ref, cw_ref, cb_ref, wd_ref, gpost_ref, left_ref,
                o_ref, st_ref, xn_sc, acc_sc, carry_sc, *, tm, tf, nsb, seq_blocks):
    i = pl.program_id(0)
    f = pl.program_id(1)
    nf = pl.num_programs(1)
    tb = tm // nsb

    @pl.when(f == 0)
    def _():
        xn_sc[...] = _rms(h_ref[...], gpre_ref[...]).astype(BF16)
        acc_sc[...] = jnp.zeros_like(acc_sc)

    xn = xn_sc[...]
    th = tf // FFN_HALVES
    halves = [slice(a * th, (a + 1) * th) for a in range(FFN_HALVES)]
    ups = [(_dot(xn, wg_ref[:, cols]), _dot(xn, wv_ref[:, cols])) for cols in halves]
    rin = lax.broadcasted_iota(jnp.int32, (tm, 1), 0) & (tb - 1)

    def spread(rows):
        return jnp.broadcast_to(rows, (nsb, tb, th)).reshape(tm, th)

    for cols, (gate, val) in zip(halves, ups):
        left = left_ref[:, :, cols]
        if seq_blocks > 1:
            left = jnp.where(i % seq_blocks == 0, left, carry_sc[f, 0:CONV_WIDTH - 1, cols][None])
            carry_sc[f, 0:CONV_WIDTH - 1, cols] = gate[tm - (CONV_WIDTH - 1):, :]
        st_ref[f, pl.ds((i // seq_blocks) * nsb, nsb), :, cols] = (
            gate.reshape(nsb, tb, th)[:, tb - (CONV_WIDTH - 1):, :])
        l0 = spread(left[:, 0:1, :])
        l1 = spread(left[:, 1:2, :])
        g1 = jnp.where(rin == 0, l1, pltpu.roll(gate, 1, 0))
        g2 = jnp.where(rin == 0, l0, jnp.where(rin == 1, l1, pltpu.roll(gate, 2, 0)))
        conv = cw_ref[0:1, cols] * g2 + cw_ref[1:2, cols] * g1 + cw_ref[2:3, cols] * gate + cb_ref[:, cols]
        gelu = 0.5 * conv * (1.0 + jnp.tanh(math.sqrt(2.0 / math.pi) * (conv + 0.044715 * (conv * conv * conv))))
        acc_sc[...] += _dot((gelu * val).astype(BF16), wd_ref[cols, :])

    @pl.when(f == nf - 1)
    def _():
        o_ref[...] = h_ref[...] + _rms(acc_sc[...], gpost_ref[...])


def _ffn_call(h, lw, left, seq_len, tm, tf):
    n = h.shape[0]
    nseq = left.shape[0]
    nf = D_FF // tf
    if seq_len >= tm:
        nsb, seq_blocks = 1, seq_len // tm
    else:
        nsb, seq_blocks = tm // seq_len, 1
    assert (tm // nsb) & (tm // nsb - 1) == 0
    row = lambda i, f: (i, 0)
    full = lambda i, f: (0, 0)
    out, state = pl.pallas_call(
        functools.partial(_ffn_kernel, tm=tm, tf=tf, nsb=nsb, seq_blocks=seq_blocks),
        grid=(n // tm, nf),
        in_specs=[pl.BlockSpec((tm, D_MODEL), row), pl.BlockSpec((1, D_MODEL), full),
                  pl.BlockSpec((D_MODEL, tf), lambda i, f: (0, f)),
                  pl.BlockSpec((D_MODEL, tf), lambda i, f: (0, nf + f)),
                  pl.BlockSpec((CONV_WIDTH, tf), lambda i, f: (0, f)),
                  pl.BlockSpec((1, tf), lambda i, f: (0, f)),
                  pl.BlockSpec((tf, D_MODEL), lambda i, f: (f, 0)),
                  pl.BlockSpec((1, D_MODEL), full),
                  pl.BlockSpec((nsb, CONV_WIDTH - 1, tf), lambda i, f: (i // seq_blocks, 0, f))],
        out_specs=[pl.BlockSpec((tm, D_MODEL), row),
                   pl.BlockSpec((nf, nseq, CONV_WIDTH - 1, tf), lambda i, f: (0, 0, 0, 0))],
        out_shape=[jax.ShapeDtypeStruct((n, D_MODEL), F32),
                   jax.ShapeDtypeStruct((nf, nseq, CONV_WIDTH - 1, tf), F32)],
        scratch_shapes=[pltpu.VMEM((tm, D_MODEL), BF16), pltpu.VMEM((tm, D_MODEL), F32),
                        pltpu.VMEM((nf, 8, tf), F32)],
        compiler_params=_cparams(("arbitrary", "arbitrary")), name="ffn",
    )(h, lw["g_ffn_pre"], lw["w_up"], lw["w_up"], lw["conv_w"], lw["conv_b"], lw["w_down"], lw["g_ffn_post"], left)
    return out, state.transpose(1, 2, 0, 3).reshape(nseq, CONV_WIDTH - 1, D_FF)


def _ple_kernel(h_ref, p_ref, gpre_ref, wg_ref, wp_ref, gpost_ref, o_ref):
    h = h_ref[...]
    gate = jax.nn.sigmoid(_dot(_rms(h, gpre_ref[...]).astype(BF16), wg_ref[...]))
    proj = _dot(p_ref[...].astype(BF16), wp_ref[...])
    o_ref[...] = h + _rms(proj * gate, gpost_ref[...])


def _ple_call(h, p, lw, tm):
    n = h.shape[0]
    row = lambda i: (i, 0)
    full = lambda i: (0, 0)
    return pl.pallas_call(
        _ple_kernel, grid=(n // tm,),
        in_specs=[pl.BlockSpec((tm, D_MODEL), row), pl.BlockSpec((tm, PLE_DIM), row),
                  pl.BlockSpec((1, D_MODEL), full), pl.BlockSpec((D_MODEL, D_MODEL), full),
                  pl.BlockSpec((PLE_DIM, D_MODEL), full), pl.BlockSpec((1, D_MODEL), full)],
        out_specs=pl.BlockSpec((tm, D_MODEL), row),
        out_shape=jax.ShapeDtypeStruct((n, D_MODEL), F32),
        compiler_params=_cparams(("parallel",)), name="ple",
    )(h, p, lw["g_ple_pre"], lw["w_ple_gate"], lw["w_ple_proj"], lw["g_ple_post"])


def _swap_halves(w):
    half = MLA_ROPE_DIM // 2
    return jnp.concatenate([w[..., half:], w[..., :half]], axis=-1)


def _pack_layer(w_in, b_forget, mla_q_norm, w_mla_uq, mla_kv_norm, w_mla_uk, w_mla_uv, lams, diff_subln, w_out,
                norm_mix_pre, norm_mix_post, norm_ffn_pre, norm_ffn_post, norm_ple_pre, norm_ple_post,
                w_ffn_up, ffn_conv_w, ffn_conv_b, w_ffn_down, w_ple_gate, w_ple_proj):
    zeros = lambda r, c: jnp.zeros((r, c), F32)
    w_kr = w_in[:, OFF_MLA_KR:OFF_DIFF_Q]
    w_all = jnp.concatenate([
        w_in[:, OFF_FOX_Q:OFF_FOX_F], w_in[:, OFF_DIFF_Q:IN_WIDTH],
        w_in[:, OFF_MLA_CQ:OFF_MLA_CKV], w_in[:, OFF_MLA_CKV:OFF_MLA_KR],
        w_kr, w_kr, zeros(D_MODEL, LANE - 2 * MLA_ROPE_DIM),
        _swap_halves(w_kr), _swap_halves(w_kr), zeros(D_MODEL, LANE - 2 * MLA_ROPE_DIM),
        w_in[:, OFF_FOX_F:OFF_MLA_CQ], zeros(D_MODEL, LANE - N_FOX_HEADS)], axis=1).astype(BF16)
    wq = w_mla_uq.reshape(MLA_Q_RANK, N_MLA_HEADS, MLA_NOPE_DIM + MLA_ROPE_DIM)
    plain, swapped = [], []
    pad = zeros(MLA_Q_RANK, MLA_PAIR_W - 2 * (MLA_NOPE_DIM + MLA_ROPE_DIM))
    for p in range(N_MLA_PAIRS):
        a, b = 2 * p, 2 * p + 1
        plain += [wq[:, a, :MLA_NOPE_DIM], wq[:, b, :MLA_NOPE_DIM], wq[:, a, MLA_NOPE_DIM:], wq[:, b, MLA_NOPE_DIM:], pad]
        swapped += [zeros(MLA_Q_RANK, 2 * MLA_NOPE_DIM), _swap_halves(wq[:, a, MLA_NOPE_DIM:]),
                    _swap_halves(wq[:, b, MLA_NOPE_DIM:]), pad]
    return dict(
        w_all=w_all,
        b_f=jnp.pad(b_forget, (0, LANE - N_FOX_HEADS)).reshape(1, LANE),
        g_q=mla_q_norm.reshape(1, -1), g_kv=mla_kv_norm.reshape(1, -1),
        w_uq2=jnp.concatenate(plain + swapped, axis=1).astype(BF16),
        w_ukv=jnp.concatenate([w_mla_uk, w_mla_uv], axis=1).astype(BF16),
        lams=lams, subln=jnp.tile(diff_subln, 2).reshape(1, LANE),
        w_out=w_out.astype(BF16),
        g_mix_pre=norm_mix_pre.reshape(1, -1), g_mix_post=norm_mix_post.reshape(1, -1),
        g_ffn_pre=norm_ffn_pre.reshape(1, -1), g_ffn_post=norm_ffn_post.reshape(1, -1),
        g_ple_pre=norm_ple_pre.reshape(1, -1), g_ple_post=norm_ple_post.reshape(1, -1),
        w_up=w_ffn_up.astype(BF16), conv_w=ffn_conv_w, conv_b=ffn_conv_b.reshape(1, -1),
        w_down=w_ffn_down.astype(BF16), w_ple_gate=w_ple_gate.astype(BF16), w_ple_proj=w_ple_proj.astype(BF16))


def _rope_tables(pos):
    half = MLA_ROPE_DIM // 2
    inv_freq = ROPE_THETA ** (-jnp.arange(half, dtype=F32) / half)
    ang = pos.astype(F32)[:, None] * inv_freq[None, :]
    cos, sin = jnp.cos(ang), jnp.sin(ang)
    t = pos.shape[0]
    pad = jnp.zeros((t, MLA_PAIR_W - LANE - 2 * MLA_ROPE_DIM), F32)
    cos_t = jnp.concatenate([jnp.ones((t, LANE), F32), cos, cos, cos, cos, pad], axis=1)
    sin_t = jnp.concatenate([jnp.zeros((t, LANE), F32), -sin, sin, -sin, sin, pad], axis=1)
    return cos_t, sin_t


def _alibi_slopes():
    s = 2.0 ** (-8.0 * np.arange(1, N_DIFF_HEADS + 1) / N_DIFF_HEADS)
    return jnp.asarray(np.broadcast_to(s.reshape(N_DIFF_HEADS // 2, 2, 1), (N_DIFF_HEADS // 2, 2, LANE)), dtype=F32)


def _tile_rows(n, pref):
    t = min(n, pref)
    assert n % t == 0
    return t


def _layer(h, p, lw, cos, sin, cache, conv_left, lam_init, *, batch, seq, past, tq):
    n = h.shape[0]
    tm = _tile_rows(n, 512)
    (fox_k, fox_v, logf, ckv, krope, diff_k, diff_v,
     fox_bf, diff_bf, mla_q, mla_k, mla_v) = _proj_call(h, lw, cos, sin, tm)
    rows = (fox_k, fox_v, logf, ckv, krope, diff_k, diff_v)
    b3 = lambda a: a.reshape(batch, seq, a.shape[-1])
    diff_extras = [_alibi_slopes(), lw["lams"], lw["subln"]]
    diff_specs = [pl.BlockSpec((1, 2, LANE), lambda bb, hp, i: (hp, 0, 0)),
                  pl.BlockSpec((4, DIFF_QK_DIM), lambda bb, hp, i: (0, 0)),
                  pl.BlockSpec((1, LANE), lambda bb, hp, i: (0, 0))]
    fb, db = b3(fox_bf), b3(diff_bf)
    if cache is None:
        common = dict(tq=tq, tkd=tq, qoff=0, blocked=True)
        o_a = _attn_call(_fox_kernel, "fox_attn", fb, 0, fb, 3, _forget_bias_call(b3(logf)),
                         _values_t(fb[..., 2 * FOX_WIDTH:], tq), [], [],
                         npairs=N_FOX_HEADS // 2, qw=LANE, chains=2, **common)
        o_b = _attn_call(_mla_kernel, "mla_attn", b3(mla_q), 0, b3(mla_k), 0, None, _values_t(b3(mla_v), tq), [], [],
                         npairs=N_MLA_PAIRS, qw=MLA_PAIR_W, chains=2, **common)
        o_c = _attn_call(_diff_kernel, "diff_attn", db, 0, db, 2, _alibi_bias_block(seq),
                         _values_t(db[..., 2 * DIFF_WIDTH:], tq), diff_extras, diff_specs,
                         npairs=N_DIFF_HEADS // 2, qw=LANE, chains=4, lam_init=lam_init, **common)
    else:
        c_fox_k, c_fox_v, c_logf, c_ckv, c_krope, c_diff_k, c_diff_v = cache
        tk = -(-(past + seq) // LANE) * LANE
        padk = lambda a: jnp.pad(a, ((0, 0), (0, tk - past - seq), (0, 0)))
        cat = lambda old, new: padk(jnp.concatenate([old.astype(new.dtype), new], axis=1))
        flat = lambda a: a.reshape(batch, past, -1)
        padq = lambda a: jnp.pad(a, ((0, 0), (0, tq - seq), (0, 0)))
        kf = cat(flat(c_fox_k), fb[..., FOX_WIDTH:2 * FOX_WIDTH])
        vf = cat(flat(c_fox_v), fb[..., 2 * FOX_WIDTH:])
        kd = cat(flat(c_diff_k), db[..., DIFF_WIDTH:2 * DIFF_WIDTH])
        vd = cat(flat(c_diff_v), db[..., 2 * DIFF_WIDTH:])
        kn_c, v_c = _kvup_call(c_ckv.reshape(batch * past, MLA_KV_RANK), lw["w_ukv"], _tile_rows(batch * past, 1024))
        kr_c = c_krope.astype(BF16)
        krp_c = jnp.concatenate([kr_c, kr_c, jnp.zeros((batch, past, LANE - 2 * MLA_ROPE_DIM), BF16)], axis=-1)
        km_c = jnp.concatenate([kn_c.reshape(batch, past, N_MLA_PAIRS, LANE),
                                jnp.broadcast_to(krp_c[:, :, None, :], (batch, past, N_MLA_PAIRS, LANE))],
                               axis=-1).reshape(batch, past, MLA_QK_W)
        km = cat(km_c, b3(mla_k))
        vm = cat(v_c.reshape(batch, past, MLA_WIDTH), b3(mla_v))
        common = dict(tq=tq, tkd=tk, qoff=past, blocked=False)
        o_a = _attn_call(_fox_kernel, "fox_attn_s", padq(fb[..., :FOX_WIDTH]), 0, kf, 0,
                         _forget_bias_call(cat(c_logf, b3(logf))), _values_t(vf, tk), [], [],
                         npairs=N_FOX_HEADS // 2, qw=LANE, chains=2, **common)
        o_b = _attn_call(_mla_kernel, "mla_attn_s", padq(b3(mla_q)), 0, km, 0, None, _values_t(vm, tk), [], [],
                         npairs=N_MLA_PAIRS, qw=MLA_PAIR_W, chains=2, **common)
        o_c = _attn_call(_diff_kernel, "diff_attn_s", padq(db[..., :DIFF_WIDTH]), 0, kd, 0, _alibi_bias_block(tk),
                         _values_t(vd, tk), diff_extras, diff_specs,
                         npairs=N_DIFF_HEADS // 2, qw=LANE, chains=4, lam_init=lam_init, **common)
        o_a, o_b, o_c = o_a[:, :seq], o_b[:, :seq], o_c[:, :seq]
    f2 = lambda a: a.reshape(n, a.shape[-1])
    h = _wout_call(f2(o_a), f2(o_b), f2(o_c), h, lw["w_out"], lw["g_mix_post"], tm)
    h, conv_state = _ffn_call(h, lw, conv_left, seq, tm, 1024)
    h = _ple_call(h, p, lw, tm)
    return h, rows, conv_state


def kernel(x_prompt, x_sample, cache_fox_k, cache_fox_v, cache_fox_logf, cache_mla_ckv, cache_mla_krope, cache_diff_k, cache_diff_v, state_ffn_conv, p_prompt, p_sample, w_in, b_forget, mla_q_norm, w_mla_uq, mla_kv_norm, w_mla_uk, w_mla_uv, diff_lambda_q1, diff_lambda_k1, diff_lambda_q2, diff_lambda_k2, diff_subln, w_out, norm_mix_pre, norm_mix_post, norm_ffn_pre, norm_ffn_post, norm_ple_pre, norm_ple_post, w_ffn_up, ffn_conv_w, ffn_conv_b, w_ffn_down, w_ple_gate, w_ple_proj):
    bp, sp, _ = x_prompt.shape
    bs, ts, _ = x_sample.shape
    depth, _, past = cache_fox_k.shape[:3]
    assert past % CHUNK == 0 and ts <= CHUNK and sp % LANE == 0
    tq = _tile_rows(sp, 512)
    tms = _tile_rows(bs * ts, 512)

    cos_p, sin_p = _rope_tables(jnp.arange(sp))
    cos_s, sin_s = _rope_tables(past + jnp.arange(ts))
    cos_s, sin_s = jnp.tile(cos_s, (tms // ts, 1)), jnp.tile(sin_s, (tms // ts, 1))

    hp = x_prompt.reshape(bp * sp, D_MODEL)
    hs = x_sample.reshape(bs * ts, D_MODEL)
    rows_p, rows_s, conv_p, conv_s = [], [], [], []
    for l in range(depth):
        lams = jnp.stack([diff_lambda_q1[l], diff_lambda_k1[l], diff_lambda_q2[l], diff_lambda_k2[l]], axis=0)
        lw = _pack_layer(w_in[l], b_forget[l], mla_q_norm[l], w_mla_uq[l], mla_kv_norm[l], w_mla_uk[l], w_mla_uv[l],
                         lams, diff_subln[l], w_out[l], norm_mix_pre[l], norm_mix_post[l], norm_ffn_pre[l],
                         norm_ffn_post[l], norm_ple_pre[l], norm_ple_post[l], w_ffn_up[l], ffn_conv_w[l],
                         ffn_conv_b[l], w_ffn_down[l], w_ple_gate[l], w_ple_proj[l])
        lam_init = 0.8 - 0.6 * math.exp(-0.3 * l)
        hp, rp, cp = _layer(hp, p_prompt[l].reshape(bp * sp, PLE_DIM), lw, cos_p, sin_p, None,
                            jnp.zeros((bp, CONV_WIDTH - 1, D_FF), F32), lam_init,
                            batch=bp, seq=sp, past=0, tq=tq)
        cache_l = (cache_fox_k[l], cache_fox_v[l], cache_fox_logf[l], cache_mla_ckv[l], cache_mla_krope[l],
                   cache_diff_k[l], cache_diff_v[l])
        hs, rs, cs = _layer(hs, p_sample[l].reshape(bs * ts, PLE_DIM), lw, cos_s, sin_s, cache_l,
                            state_ffn_conv[l], lam_init, batch=bs, seq=ts, past=past, tq=LANE)
        rows_p.append(rp)
        rows_s.append(rs)
        conv_p.append(cp)
        conv_s.append(cs)

    def stack_rows(per_layer, batch, seq):
        fk, fv, lf, ckv, kr, dk, dv = (jnp.stack(a, axis=0) for a in zip(*per_layer))
        lead = (depth, batch, seq)
        return (fk.reshape(lead + (N_FOX_HEADS, HEAD_DIM)), fv.reshape(lead + (N_FOX_HEADS, HEAD_DIM)),
                lf.reshape(lead + (N_FOX_HEADS,)), ckv.reshape(lead + (MLA_KV_RANK,)),
                kr.reshape(lead + (MLA_ROPE_DIM,)), dk.reshape(lead + (N_DIFF_HEADS, 2 * DIFF_QK_DIM)),
                dv.reshape(lead + (N_DIFF_HEADS, DIFF_V_DIM)))

    out_p = stack_rows(rows_p, bp, sp)
    out_s = stack_rows(rows_s, bs, ts)
    return ((hp.reshape(bp, sp, D_MODEL), hs.reshape(bs, ts, D_MODEL)) + out_p + (jnp.stack(conv_p, axis=0),)
            + out_s + (jnp.stack(conv_s, axis=0),))
```

```python
import functools
import math

import jax
import jax.numpy as jnp
import numpy as np
from jax import lax
from jax.experimental import pallas as pl
from jax.experimental.pallas import tpu as pltpu

F32 = jnp.float32
BF16 = jnp.bfloat16

D_MODEL = 1024
HEAD_DIM = 64
N_FOX_HEADS = 6
N_MLA_HEADS = 6
N_DIFF_HEADS = 4
MLA_Q_RANK = 384
MLA_KV_RANK = 256
MLA_NOPE_DIM = 64
MLA_ROPE_DIM = 32
MLA_V_DIM = 64
DIFF_QK_DIM = 32
DIFF_V_DIM = 64
FOX_WIDTH = N_FOX_HEADS * HEAD_DIM
MLA_WIDTH = N_MLA_HEADS * MLA_V_DIM
DIFF_WIDTH = N_DIFF_HEADS * DIFF_V_DIM
D_FF = 4 * D_MODEL
CONV_WIDTH = 3
PLE_DIM = 256
CHUNK = 64
ROPE_THETA = 10000.0
RMS_EPS = 1e-6
NEG_INF = -1e30
LOG2E = math.log2(math.e)

OFF_FOX_Q = 0
OFF_FOX_F = 3 * FOX_WIDTH
OFF_MLA_CQ = OFF_FOX_F + N_FOX_HEADS
OFF_MLA_CKV = OFF_MLA_CQ + MLA_Q_RANK
OFF_MLA_KR = OFF_MLA_CKV + MLA_KV_RANK
OFF_DIFF_Q = OFF_MLA_KR + MLA_ROPE_DIM
IN_WIDTH = OFF_DIFF_Q + 2 * N_DIFF_HEADS * 2 * DIFF_QK_DIM + DIFF_WIDTH

LANE = 128
V7X_VMEM_BYTES = 64 * 1024 * 1024
VMEM_LIMIT = (V7X_VMEM_BYTES * 7) // 8

C_FOX = 0
C_DIFF = C_FOX + 3 * FOX_WIDTH
C_CQ = C_DIFF + 3 * DIFF_WIDTH
C_CKV = C_CQ + MLA_Q_RANK
C_KR = C_CKV + MLA_KV_RANK
C_F = C_KR + 2 * LANE
W_ALL = C_F + LANE
N_MLA_PAIRS = N_MLA_HEADS // 2
MLA_PAIR_W = 2 * LANE
MLA_QK_W = N_MLA_PAIRS * MLA_PAIR_W

FOX_QSCALE = HEAD_DIM ** -0.5 * LOG2E
MLA_QSCALE = (MLA_NOPE_DIM + MLA_ROPE_DIM) ** -0.5 * LOG2E
DIFF_QSCALE = DIFF_QK_DIM ** -0.5 * LOG2E


def _cparams(sem):
    return pltpu.CompilerParams(dimension_semantics=sem, vmem_limit_bytes=VMEM_LIMIT)


def _rms(x, g):
    return x * lax.rsqrt(jnp.mean(x * x, axis=-1, keepdims=True) + RMS_EPS) * g


def _dot(a, b):
    return jnp.dot(a, b, preferred_element_type=F32)


def _dot_nt(a, b):
    return lax.dot_general(a, b, (((1,), (1,)), ((), ())), preferred_element_type=F32)


def _proj_kernel(h_ref, gpre_ref, w_ref, bf_ref, gq_ref, gkv_ref, wuq_ref, wukv_ref, cos_ref, sin_ref,
                 foxk_ref, foxv_ref, logf_ref, ckv_ref, krope_ref, diffk_ref, diffv_ref,
                 foxbf_ref, diffbf_ref, mlaq_ref, mlak_ref, mlav_ref):
    xn = _rms(h_ref[...], gpre_ref[...]).astype(BF16)
    cos = cos_ref[...]
    sin = sin_ref[...]

    z = _dot(xn, w_ref[:, C_FOX:C_FOX + 3 * FOX_WIDTH])
    foxbf_ref[:, :FOX_WIDTH] = (z[:, :FOX_WIDTH] * FOX_QSCALE).astype(BF16)
    foxbf_ref[:, FOX_WIDTH:] = z[:, FOX_WIDTH:].astype(BF16)
    foxk_ref[...] = z[:, FOX_WIDTH:2 * FOX_WIDTH]
    foxv_ref[...] = z[:, 2 * FOX_WIDTH:3 * FOX_WIDTH]

    z = _dot(xn, w_ref[:, C_DIFF:C_DIFF + 3 * DIFF_WIDTH])
    diffbf_ref[:, :DIFF_WIDTH] = (z[:, :DIFF_WIDTH] * DIFF_QSCALE).astype(BF16)
    diffbf_ref[:, DIFF_WIDTH:] = z[:, DIFF_WIDTH:].astype(BF16)
    diffk_ref[...] = z[:, DIFF_WIDTH:2 * DIFF_WIDTH]
    diffv_ref[...] = z[:, 2 * DIFF_WIDTH:3 * DIFF_WIDTH]

    cq = _rms(_dot(xn, w_ref[:, C_CQ:C_CQ + MLA_Q_RANK]), gq_ref[...]).astype(BF16)
    q2 = _dot(cq, wuq_ref[...])
    for p in range(N_MLA_PAIRS):
        lo = p * MLA_PAIR_W
        roped = q2[:, lo:lo + MLA_PAIR_W] * cos + q2[:, MLA_QK_W + lo:MLA_QK_W + lo + MLA_PAIR_W] * sin
        mlaq_ref[:, lo:lo + MLA_PAIR_W] = (roped * MLA_QSCALE).astype(BF16)

    ckv = _rms(_dot(xn, w_ref[:, C_CKV:C_CKV + MLA_KV_RANK]), gkv_ref[...])
    ckv_ref[...] = ckv
    kv = _dot(ckv.astype(BF16), wukv_ref[...])
    mlav_ref[...] = kv[:, MLA_WIDTH:2 * MLA_WIDTH].astype(BF16)
    zkr = _dot(xn, w_ref[:, C_KR:C_KR + 2 * LANE])
    krp = zkr[:, :LANE] * cos[:, LANE:] + zkr[:, LANE:] * sin[:, LANE:]
    krope_ref[...] = krp[:, :MLA_ROPE_DIM]
    krp16 = krp.astype(BF16)
    for p in range(N_MLA_PAIRS):
        lo = p * MLA_PAIR_W
        mlak_ref[:, lo:lo + LANE] = kv[:, p * LANE:(p + 1) * LANE].astype(BF16)
        mlak_ref[:, lo + LANE:lo + 2 * LANE] = krp16

    zf = _dot(xn, w_ref[:, C_F:C_F + LANE]) + bf_ref[...]
    logf = -(jnp.maximum(-zf, 0.0) + jnp.log1p(jnp.exp(-jnp.abs(zf))))
    logf_ref[...] = logf[:, :N_FOX_HEADS]


def _proj_call(h, lw, cos, sin, tm):
    n = h.shape[0]
    ntab = cos.shape[0] // tm
    row = lambda i: (i, 0)
    full = lambda i: (0, 0)
    tab = lambda i: (i % ntab, 0)
    in_specs = [
        pl.BlockSpec((tm, D_MODEL), row),
        pl.BlockSpec((1, D_MODEL), full),
        pl.BlockSpec((D_MODEL, W_ALL), full),
        pl.BlockSpec((1, LANE), full),
        pl.BlockSpec((1, MLA_Q_RANK), full),
        pl.BlockSpec((1, MLA_KV_RANK), full),
        pl.BlockSpec((MLA_Q_RANK, 2 * MLA_QK_W), full),
        pl.BlockSpec((MLA_KV_RANK, 2 * MLA_WIDTH), full),
        pl.BlockSpec((tm, MLA_PAIR_W), tab),
        pl.BlockSpec((tm, MLA_PAIR_W), tab),
    ]
    widths = [(FOX_WIDTH, F32), (FOX_WIDTH, F32), (N_FOX_HEADS, F32), (MLA_KV_RANK, F32), (MLA_ROPE_DIM, F32),
              (DIFF_WIDTH, F32), (DIFF_WIDTH, F32),
              (3 * FOX_WIDTH, BF16), (3 * DIFF_WIDTH, BF16), (MLA_QK_W, BF16), (MLA_QK_W, BF16), (MLA_WIDTH, BF16)]
    out_shape = [jax.ShapeDtypeStruct((n, w), dt) for w, dt in widths]
    out_specs = [pl.BlockSpec((tm, w), row) for w, _ in widths]
    return pl.pallas_call(
        _proj_kernel, grid=(n // tm,), in_specs=in_specs, out_specs=out_specs, out_shape=out_shape,
        compiler_params=_cparams(("parallel",)), name="proj",
    )(h, lw["g_mix_pre"], lw["w_all"], lw["b_f"], lw["g_q"], lw["g_kv"], lw["w_uq2"], lw["w_ukv"], cos, sin)


def _kvup_kernel(ckv_ref, wukv_ref, kn_ref, v_ref):
    kv = _dot(ckv_ref[...].astype(BF16), wukv_ref[...])
    kn_ref[...] = kv[:, :MLA_WIDTH].astype(BF16)
    v_ref[...] = kv[:, MLA_WIDTH:].astype(BF16)


def _kvup_call(ckv, w_ukv, tm):
    n = ckv.shape[0]
    row = lambda i: (i, 0)
    return pl.pallas_call(
        _kvup_kernel, grid=(n // tm,),
        in_specs=[pl.BlockSpec((tm, MLA_KV_RANK), row), pl.BlockSpec((MLA_KV_RANK, 2 * MLA_WIDTH), lambda i: (0, 0))],
        out_specs=[pl.BlockSpec((tm, MLA_WIDTH), row), pl.BlockSpec((tm, MLA_WIDTH), row)],
        out_shape=[jax.ShapeDtypeStruct((n, MLA_WIDTH), BF16)] * 2,
        compiler_params=_cparams(("parallel",)), name="kvup",
    )(ckv, w_ukv)


AUG_STRIDE = 8


def _keep_bf16_bits(x):
    bits = lax.bitcast_convert_type(x, jnp.uint32) & jnp.uint32(0xFFFF0000)
    return lax.bitcast_convert_type(bits, F32)


def _split3(x):
    x1 = _keep_bf16_bits(x)
    r = x - x1
    x2 = _keep_bf16_bits(r)
    return x1, x2, r - x2


def _place3(x1, x2, x3):
    lane = lax.broadcasted_iota(jnp.int32, (1, LANE), 1)
    return jnp.where(lane < AUG_STRIDE, x1,
                     jnp.where(lane < 2 * AUG_STRIDE, pltpu.roll(x2, AUG_STRIDE, 1),
                               pltpu.roll(x3, 2 * AUG_STRIDE, 1))).astype(BF16)


def _aug_selector(h, rows):
    lane = lax.broadcasted_iota(jnp.int32, (rows, LANE), 1)
    hit = (lane == h) | (lane == AUG_STRIDE + h) | (lane == 2 * AUG_STRIDE + h)
    return jnp.where(hit, 1.0, 0.0).astype(BF16)


def _forget_bias_kernel(x_ref, tri_ref, o_ref, carry_sc, *, tb):
    @pl.when(pl.program_id(1) == 0)
    def _():
        carry_sc[...] = jnp.zeros_like(carry_sc)

    terms = jnp.concatenate(_split3(x_ref[0]), axis=1).astype(BF16)
    sums = _dot(tri_ref[...], terms)
    cs = carry_sc[0:1, :] + sums[:, :LANE] + sums[:, LANE:2 * LANE] + sums[:, 2 * LANE:]
    carry_sc[...] = jnp.broadcast_to(cs[tb - 1:tb, :], carry_sc.shape)
    o_ref[0] = _place3(*_split3(cs * (-LOG2E)))


def _forget_bias_call(logf):
    b, t, hh = logf.shape
    tb = max(d for d in range(LANE, 4 * LANE + 1, LANE) if t % d == 0)
    x = jnp.pad(logf, ((0, 0), (0, 0), (0, LANE - hh)))
    tri = jnp.tril(jnp.ones((tb, tb), BF16))
    blk = pl.BlockSpec((1, tb, LANE), lambda i, j: (i, j, 0))
    return pl.pallas_call(
        functools.partial(_forget_bias_kernel, tb=tb), grid=(b, t // tb),
        in_specs=[blk, pl.BlockSpec((tb, tb), lambda i, j: (0, 0))], out_specs=blk,
        out_shape=jax.ShapeDtypeStruct((b, t, LANE), BF16),
        scratch_shapes=[pltpu.VMEM((8, LANE), F32)],
        compiler_params=_cparams(("parallel", "arbitrary")), name="forget_bias",
    )(x, tri)


def _alibi_bias_block(t_k):
    slopes = 2.0 ** (-8.0 * np.arange(1, N_DIFF_HEADS + 1) / N_DIFF_HEADS)
    b = jnp.zeros((t_k, LANE), F32).at[:, :N_DIFF_HEADS].set(
        jnp.arange(t_k, dtype=F32)[:, None] * jnp.asarray(slopes * LOG2E, F32)[None, :])
    x1, x2, x3 = _split3(b)
    lane = jnp.arange(LANE)[None, :]
    placed = jnp.where(lane < AUG_STRIDE, x1, jnp.where(lane < 2 * AUG_STRIDE, jnp.roll(x2, AUG_STRIDE, 1),
                                                        jnp.roll(x3, 2 * AUG_STRIDE, 1)))
    return placed.astype(BF16)[None]


ROW_CHUNK = CHUNK
STRIP = 4 * LANE
ONES_ROWS = 16


def _masked_chunks(s_ref, c, mask, tk, tq):
    chunks = [(r0, "all" if mask is None else mask[0](r0, 0, tq)) for r0 in range(0, tk, ROW_CHUNK)]

    def logits(r0, vis):
        x = s_ref[c, r0:r0 + ROW_CHUNK, :]
        return x if vis == "all" else mask[1](r0, 0, x)
    return chunks, logits


def _softmax_max(s_ref, m_ref, c, mask, tk, tq):
    chunks, logits = _masked_chunks(s_ref, c, mask, tk, tq)
    mrun = jnp.full((8, tq), NEG_INF, F32)
    for r0, vis in chunks:
        if vis != "none":
            mrun = jnp.maximum(mrun, jnp.max(logits(r0, vis).reshape(ROW_CHUNK // 8, 8, tq), axis=0))
    m_old = m_ref[c]
    return m_old, jnp.maximum(m_old, jnp.max(mrun, axis=0, keepdims=True))


def _softmax_exp(s_ref, p_ref, m_ref, alpha_ref, c, mask, tk, tq, m_old, m_new):
    chunks, logits = _masked_chunks(s_ref, c, mask, tk, tq)
    for r0, vis in chunks:
        if vis == "none":
            p_ref[c, r0:r0 + ROW_CHUNK, :] = jnp.zeros((ROW_CHUNK, tq), BF16)
        else:
            p_ref[c, r0:r0 + ROW_CHUNK, :] = jnp.exp2(logits(r0, vis) - m_new).astype(BF16)
    m_ref[c] = m_new
    alpha_ref[c] = jnp.exp2(m_old - m_new)


def _apply_values(p_ref, alpha_ref, acc_ref, c, vt):
    acc_ref[c] = alpha_ref[c] * acc_ref[c] + _dot(vt, p_ref[c])


def _causal_mask(qoff):
    def visibility(r0, c0, sw):
        if r0 + ROW_CHUNK - 1 <= c0 + qoff:
            return "all"
        return "none" if r0 > c0 + sw - 1 + qoff else "some"

    def apply(r0, c0, x):
        kk, qq = _key_query_iota(*x.shape)
        return jnp.where(kk - qq <= c0 + qoff - r0, x, NEG_INF)
    return visibility, apply


def _chunk_visibility(r0, c0, sw, qoff):
    if r0 // CHUNK <= (c0 + qoff) // CHUNK:
        return "all"
    return "none" if r0 // CHUNK > (c0 + sw - 1 + qoff) // CHUNK else "some"


def _chunk_mask(r0, c0, sw, qoff):
    qq = lax.broadcasted_iota(jnp.int32, (1, sw), 1)
    return r0 // CHUNK <= (qq + (c0 + qoff)) // CHUNK


def _init_states(m_ref, acc_ref):
    m_ref[...] = jnp.full(m_ref.shape, NEG_INF, F32)
    acc_ref[...] = jnp.zeros(acc_ref.shape, F32)


def _normalized(acc_ref, c, dv):
    return acc_ref[c, :dv, :] / acc_ref[c, dv:dv + 1, :]


def _key_query_iota(tk, tq):
    return (lax.broadcasted_iota(jnp.int32, (tk, tq), 0), lax.broadcasted_iota(jnp.int32, (tk, tq), 1))


def _store_pair(o_ref, o0, o1, row_scale=None):
    out = jnp.concatenate([o0, o1], axis=0).T
    if row_scale is not None:
        out = out * row_scale
    o_ref[0] = out.astype(BF16)


def _block_start(j, size):
    return j * size if isinstance(j, int) else pl.multiple_of(j * size, size)


N_BUF = 3


def _attend(i, blocked, chains, qk_fn, max_fn, exp_fn, pv_fn):
    def step(qk, pv, buf, diag):
        stats = []
        for c in range(chains):
            if qk is not None:
                qk_fn(qk[0], qk[1], c)
            stats.append(max_fn(buf, c, diag))
        for c in range(chains):
            if pv is not None:
                pv_fn(pv[0], pv[1], c)
            exp_fn(buf, c, diag, *stats[c])

    def all_chains(fn, *args):
        for c in range(chains):
            fn(*args, c)

    if not blocked:
        all_chains(qk_fn, 0, 0)
        step(None, None, 0, True)
        all_chains(pv_fn, 0, 0)
        return
    last = N_BUF - 1
    lead = i % N_BUF

    def single(j, carry):
        all_chains(qk_fn, last, j)
        step(None, None, last, False)
        all_chains(pv_fn, last, j)
        return carry

    lax.fori_loop(0, lead, single, 0)
    all_chains(qk_fn, 0, lead)
    all_chains(qk_fn, 1, jnp.minimum(lead + 1, i))

    def rotate(t, carry):
        b0 = lead + N_BUF * t
        for u in range(N_BUF):
            step(((u + 2) % N_BUF, jnp.minimum(b0 + u + 2, i)), (u - 1, b0 + u - 1) if u > 0 else None, u, False)
        all_chains(pv_fn, last, b0 + last)
        return carry

    lax.fori_loop(0, (i - lead) // N_BUF, rotate, 0)
    step(None, None, 0, True)
    all_chains(pv_fn, 0, i)


def _vt_rows(vt_ref, j, e, dv):
    return vt_ref[0, 0, j, e * (dv + ONES_ROWS):(e + 1) * (dv + ONES_ROWS), :]


def _split_scratch(scratch):
    nbuf = (len(scratch) - 3) // 2
    return (scratch[:nbuf], scratch[nbuf:2 * nbuf]) + tuple(scratch[2 * nbuf:])


def _chain_fns(scratch, vt_ref, mask_of, dv, head_of, tkd, tq):
    s_sc, p_sc, m_sc, alpha_sc, acc_sc = _split_scratch(scratch)

    def max_fn(buf, c, diag):
        return _softmax_max(s_sc[buf], m_sc, c, mask_of(c) if diag else None, tkd, tq)

    def exp_fn(buf, c, diag, m_old, m_new):
        _softmax_exp(s_sc[buf], p_sc[buf], m_sc, alpha_sc, c, mask_of(c) if diag else None, tkd, tq, m_old, m_new)

    def pv_fn(buf, j, c):
        _apply_values(p_sc[buf], alpha_sc, acc_sc, c, _vt_rows(vt_ref, j, head_of(c), dv))
    return max_fn, exp_fn, pv_fn


def _fox_kernel(q_ref, k_ref, aug_ref, vt_ref, o_ref, *scratch, tq, tkd, qoff, blocked):
    s_sc, p_sc, m_sc, alpha_sc, acc_sc = _split_scratch(scratch)
    hp, i = pl.program_id(1), pl.program_id(2)
    _init_states(m_sc, acc_sc)
    q = q_ref[0]
    lane = lax.broadcasted_iota(jnp.int32, (1, LANE), 1)
    qs = [jnp.concatenate([jnp.where((lane >= e * HEAD_DIM) & (lane < (e + 1) * HEAD_DIM), q, jnp.zeros_like(q)),
                           _aug_selector(2 * hp + e, tq)], axis=1) for e in range(2)]

    def qk_fn(buf, j, e):
        rows = pl.ds(_block_start(j, tq), tkd)
        k = jnp.concatenate([k_ref[0, rows, :], aug_ref[0, rows, :]], axis=1)
        s_sc[buf][e] = _dot_nt(k, qs[e])

    _attend(i, blocked, 2, qk_fn, *_chain_fns(scratch, vt_ref, lambda e: _causal_mask(qoff), HEAD_DIM,
                                              lambda e: e, tkd, tq))
    _store_pair(o_ref, *(_normalized(acc_sc, e, HEAD_DIM) for e in range(2)))


def _mla_kernel(q_ref, k_ref, vt_ref, o_ref, *scratch, tq, tkd, qoff, blocked):
    s_sc, p_sc, m_sc, alpha_sc, acc_sc = _split_scratch(scratch)
    i = pl.program_id(2)
    _init_states(m_sc, acc_sc)
    q = q_ref[0]
    lane2 = lax.broadcasted_iota(jnp.int32, (1, MLA_PAIR_W), 1)
    sel = []
    for e in range(2):
        nope = (lane2 >= e * MLA_NOPE_DIM) & (lane2 < (e + 1) * MLA_NOPE_DIM)
        rope = (lane2 >= LANE + e * MLA_ROPE_DIM) & (lane2 < LANE + (e + 1) * MLA_ROPE_DIM)
        sel.append(jnp.where(nope | rope, q, jnp.zeros_like(q)))

    def qk_fn(buf, j, e):
        s_sc[buf][e] = _dot_nt(k_ref[0, pl.ds(_block_start(j, tq), tkd), :], sel[e])

    mask = (lambda r0, c0, sw: _chunk_visibility(r0, c0, sw, qoff),
            lambda r0, c0, x: jnp.where(_chunk_mask(r0, c0, x.shape[1], qoff), x, NEG_INF))
    _attend(i, blocked, 2, qk_fn, *_chain_fns(scratch, vt_ref, lambda e: mask, MLA_V_DIM, lambda e: e, tkd, tq))
    _store_pair(o_ref, *(_normalized(acc_sc, e, MLA_V_DIM) for e in range(2)))


def _diff_kernel(q_ref, k_ref, aug_ref, vt_ref, slope_ref, lam_ref, subln_ref, o_ref, *scratch,
                 tq, tkd, qoff, blocked, lam_init):
    s_sc, p_sc, m_sc, alpha_sc, acc_sc = _split_scratch(scratch)
    hp, i = pl.program_id(1), pl.program_id(2)
    _init_states(m_sc, acc_sc)
    q = q_ref[0]
    lane = lax.broadcasted_iota(jnp.int32, (1, LANE), 1)
    sel = [jnp.concatenate([jnp.where((lane >= (2 * e + t) * DIFF_QK_DIM) & (lane < (2 * e + t + 1) * DIFF_QK_DIM),
                                      q, jnp.zeros_like(q)), _aug_selector(2 * hp + e, tq)], axis=1)
           for e in range(2) for t in range(2)]
    slopes = [slope_ref[0, e:e + 1, 0:1] * LOG2E for e in range(2)]

    def qk_fn(buf, j, c):
        rows = pl.ds(_block_start(j, tq), tkd)
        k = jnp.concatenate([k_ref[0, rows, :], aug_ref[0, rows, :]], axis=1)
        s_sc[buf][c] = _dot_nt(k, sel[c])

    def mask_of(c):
        def visibility(r0, c0, sw):
            vis = _chunk_visibility(r0, c0, sw, qoff)
            return "some" if vis == "all" and r0 + ROW_CHUNK - 1 > c0 + qoff else vis

        def apply(r0, c0, x):
            kk, qq = _key_query_iota(*x.shape)
            ahead = jnp.maximum(kk - qq + (r0 - c0 - qoff), 0).astype(F32)
            return jnp.where(_chunk_mask(r0, c0, x.shape[1], qoff), x - (2.0 * slopes[c // 2]) * ahead, NEG_INF)
        return visibility, apply

    _attend(i, blocked, 4, qk_fn, *_chain_fns(scratch, vt_ref, mask_of, DIFF_V_DIM, lambda c: c // 2, tkd, tq))
    res = [_normalized(acc_sc, c, DIFF_V_DIM) for c in range(4)]
    lq = lam_ref[...]
    lam = (jnp.exp(jnp.sum(lq[0:1] * lq[1:2], axis=-1, keepdims=True))
           - jnp.exp(jnp.sum(lq[2:3] * lq[3:4], axis=-1, keepdims=True)) + lam_init)
    outs = []
    for e in range(2):
        o = res[2 * e] - lam * res[2 * e + 1]
        outs.append(o * lax.rsqrt(jnp.mean(o * o, axis=0, keepdims=True) + RMS_EPS))
    _store_pair(o_ref, outs[0], outs[1], subln_ref[...] * (1.0 - lam_init))


def _attn_call(kernel, name, q, q_cb, k, k_cb, aug, vt, extras, extra_specs, *, npairs, qw, tq, tkd, qoff, blocked,
               chains, **kw):
    b, t_q = q.shape[0], q.shape[1]
    t_k = k.shape[1]
    acc_rows = LANE // 2 + ONES_ROWS
    nbuf = N_BUF if blocked else 1
    scratch = ([pltpu.VMEM((chains, tkd, tq), F32)] * nbuf + [pltpu.VMEM((chains, tkd, tq), BF16)] * nbuf
               + [pltpu.VMEM((chains, 1, tq), F32), pltpu.VMEM((chains, 1, tq), F32),
                  pltpu.VMEM((chains, acc_rows, tq), F32)])
    in_specs = [pl.BlockSpec((1, tq, qw), lambda bb, hp, i: (bb, i, q_cb + hp)),
                pl.BlockSpec((1, t_k, qw), lambda bb, hp, i: (bb, 0, k_cb + hp))]
    args = [q, k]
    if aug is not None:
        per_batch = aug.shape[0] > 1
        in_specs.append(pl.BlockSpec((1, t_k, LANE), lambda bb, hp, i: (bb if per_batch else 0, 0, 0)))
        args.append(aug)
    in_specs.append(pl.BlockSpec((1, 1) + vt.shape[2:], lambda bb, hp, i: (bb, hp, 0, 0, 0)))
    return pl.pallas_call(
        functools.partial(kernel, tq=tq, tkd=tkd, qoff=qoff, blocked=blocked, **kw),
        grid=(b, npairs, t_q // tq), in_specs=in_specs + extra_specs,
        out_specs=pl.BlockSpec((1, tq, LANE), lambda bb, hp, i: (bb, i, hp)),
        out_shape=jax.ShapeDtypeStruct((b, t_q, npairs * LANE), BF16), scratch_shapes=scratch,
        compiler_params=_cparams(("parallel", "parallel", "arbitrary")), name=name,
    )(*args, vt, *extras)


def _values_t(v, tkb):
    b, t, w = v.shape
    dv = LANE // 2
    vt = v.reshape(b, t // tkb, tkb, w // LANE, 2, dv).transpose(0, 3, 1, 4, 5, 2)
    ones = jnp.ones(vt.shape[:4] + (ONES_ROWS, tkb), v.dtype)
    return jnp.concatenate([vt, ones], axis=4).reshape(b, w // LANE, t // tkb, 2 * (dv + ONES_ROWS), tkb)


FFN_HALVES = 2


def _ffn_kernel(oa_ref, ob_ref, oc_ref, h_ref, wout_ref, gmix_ref,
                gpre_ref, wg_ref, wv_ref, cw_ref, cb_ref, wd_ref, gpost_ref, left_ref,
                p_ref, gple_ref, wgate_ref, wproj_ref, gplepost_ref,
                o_ref, st_ref, h1_sc, xn_sc, acc_sc, carry_sc, *, tm, tf, nsb, seq_blocks):
    i = pl.program_id(0)
    f = pl.program_id(1)
    nf = pl.num_programs(1)
    tb = tm // nsb

    @pl.when(f == 0)
    def _():
        y = (_dot(oa_ref[...], wout_ref[0:FOX_WIDTH, :])
             + _dot(ob_ref[...], wout_ref[FOX_WIDTH:FOX_WIDTH + MLA_WIDTH, :])
             + _dot(oc_ref[...], wout_ref[FOX_WIDTH + MLA_WIDTH:, :]))
        h1 = h_ref[...] + _rms(y, gmix_ref[...])
        h1_sc[...] = h1
        xn_sc[...] = _rms(h1, gpre_ref[...]).astype(BF16)
        acc_sc[...] = jnp.zeros_like(acc_sc)

    xn = xn_sc[...]
    th = tf // FFN_HALVES
    halves = [slice(a * th, (a + 1) * th) for a in range(FFN_HALVES)]
    ups = [(_dot(xn, wg_ref[:, cols]), _dot(xn, wv_ref[:, cols])) for cols in halves]
    rin = lax.broadcasted_iota(jnp.int32, (tm, 1), 0) & (tb - 1)

    def spread(rows):
        return jnp.broadcast_to(rows, (nsb, tb, th)).reshape(tm, th)

    for cols, (gate, val) in zip(halves, ups):
        left = left_ref[:, :, cols]
        if seq_blocks > 1:
            left = jnp.where(i % seq_blocks == 0, left, carry_sc[f, 0:CONV_WIDTH - 1, cols][None])
            carry_sc[f, 0:CONV_WIDTH - 1, cols] = gate[tm - (CONV_WIDTH - 1):, :]
        st_ref[f, pl.ds((i // seq_blocks) * nsb, nsb), :, cols] = (
            gate.reshape(nsb, tb, th)[:, tb - (CONV_WIDTH - 1):, :])
        l0 = spread(left[:, 0:1, :])
        l1 = spread(left[:, 1:2, :])
        g1 = jnp.where(rin == 0, l1, pltpu.roll(gate, 1, 0))
        g2 = jnp.where(rin == 0, l0, jnp.where(rin == 1, l1, pltpu.roll(gate, 2, 0)))
        conv = cw_ref[0:1, cols] * g2 + cw_ref[1:2, cols] * g1 + cw_ref[2:3, cols] * gate + cb_ref[:, cols]
        gelu = 0.5 * conv * (1.0 + jnp.tanh(math.sqrt(2.0 / math.pi) * (conv + 0.044715 * (conv * conv * conv))))
        acc_sc[...] += _dot((gelu * val).astype(BF16), wd_ref[cols, :])

    @pl.when(f == nf - 1)
    def _():
        h2 = h1_sc[...] + _rms(acc_sc[...], gpost_ref[...])
        gate = jax.nn.sigmoid(_dot(_rms(h2, gple_ref[...]).astype(BF16), wgate_ref[...]))
        proj = _dot(p_ref[...].astype(BF16), wproj_ref[...])
        o_ref[...] = h2 + _rms(proj * gate, gplepost_ref[...])


def _ffn_call(oa, ob, oc, h, p, lw, left, seq_len, tm, tf):
    n = h.shape[0]
    nseq = left.shape[0]
    nf = D_FF // tf
    if seq_len >= tm:
        nsb, seq_blocks = 1, seq_len // tm
    else:
        nsb, seq_blocks = tm // seq_len, 1
    assert (tm // nsb) & (tm // nsb - 1) == 0
    row = lambda i, f: (i, 0)
    full = lambda i, f: (0, 0)
    out, state = pl.pallas_call(
        functools.partial(_ffn_kernel, tm=tm, tf=tf, nsb=nsb, seq_blocks=seq_blocks),
        grid=(n // tm, nf),
        in_specs=[pl.BlockSpec((tm, FOX_WIDTH), row), pl.BlockSpec((tm, MLA_WIDTH), row),
                  pl.BlockSpec((tm, DIFF_WIDTH), row), pl.BlockSpec((tm, D_MODEL), row),
                  pl.BlockSpec((D_MODEL, D_MODEL), full), pl.BlockSpec((1, D_MODEL), full),
                  pl.BlockSpec((1, D_MODEL), full),
                  pl.BlockSpec((D_MODEL, tf), lambda i, f: (0, f)),
                  pl.BlockSpec((D_MODEL, tf), lambda i, f: (0, nf + f)),
                  pl.BlockSpec((CONV_WIDTH, tf), lambda i, f: (0, f)),
                  pl.BlockSpec((1, tf), lambda i, f: (0, f)),
                  pl.BlockSpec((tf, D_MODEL), lambda i, f: (f, 0)),
                  pl.BlockSpec((1, D_MODEL), full),
                  pl.BlockSpec((nsb, CONV_WIDTH - 1, tf), lambda i, f: (i // seq_blocks, 0, f)),
                  pl.BlockSpec((tm, PLE_DIM), row), pl.BlockSpec((1, D_MODEL), full),
                  pl.BlockSpec((D_MODEL, D_MODEL), full), pl.BlockSpec((PLE_DIM, D_MODEL), full),
                  pl.BlockSpec((1, D_MODEL), full)],
        out_specs=[pl.BlockSpec((tm, D_MODEL), row),
                   pl.BlockSpec((nf, nseq, CONV_WIDTH - 1, tf), lambda i, f: (0, 0, 0, 0))],
        out_shape=[jax.ShapeDtypeStruct((n, D_MODEL), F32),
                   jax.ShapeDtypeStruct((nf, nseq, CONV_WIDTH - 1, tf), F32)],
        scratch_shapes=[pltpu.VMEM((tm, D_MODEL), F32), pltpu.VMEM((tm, D_MODEL), BF16),
                        pltpu.VMEM((tm, D_MODEL), F32), pltpu.VMEM((nf, 8, tf), F32)],
        compiler_params=_cparams(("arbitrary", "arbitrary")), name="ffn",
    )(oa, ob, oc, h, lw["w_out"], lw["g_mix_post"],
      lw["g_ffn_pre"], lw["w_up"], lw["w_up"], lw["conv_w"], lw["conv_b"], lw["w_down"], lw["g_ffn_post"], left,
      p, lw["g_ple_pre"], lw["w_ple_gate"], lw["w_ple_proj"], lw["g_ple_post"])
    return out, state.transpose(1, 2, 0, 3).reshape(nseq, CONV_WIDTH - 1, D_FF)


def _swap_halves(w):
    half = MLA_ROPE_DIM // 2
    return jnp.concatenate([w[..., half:], w[..., :half]], axis=-1)


def _pack_layer(w_in, b_forget, mla_q_norm, w_mla_uq, mla_kv_norm, w_mla_uk, w_mla_uv, lams, diff_subln, w_out,
                norm_mix_pre, norm_mix_post, norm_ffn_pre, norm_ffn_post, norm_ple_pre, norm_ple_post,
                w_ffn_up, ffn_conv_w, ffn_conv_b, w_ffn_down, w_ple_gate, w_ple_proj):
    zeros = lambda r, c: jnp.zeros((r, c), F32)
    w_kr = w_in[:, OFF_MLA_KR:OFF_DIFF_Q]
    w_all = jnp.concatenate([
        w_in[:, OFF_FOX_Q:OFF_FOX_F], w_in[:, OFF_DIFF_Q:IN_WIDTH],
        w_in[:, OFF_MLA_CQ:OFF_MLA_CKV], w_in[:, OFF_MLA_CKV:OFF_MLA_KR],
        w_kr, w_kr, zeros(D_MODEL, LANE - 2 * MLA_ROPE_DIM),
        _swap_halves(w_kr), _swap_halves(w_kr), zeros(D_MODEL, LANE - 2 * MLA_ROPE_DIM),
        w_in[:, OFF_FOX_F:OFF_MLA_CQ], zeros(D_MODEL, LANE - N_FOX_HEADS)], axis=1).astype(BF16)
    wq = w_mla_uq.reshape(MLA_Q_RANK, N_MLA_HEADS, MLA_NOPE_DIM + MLA_ROPE_DIM)
    plain, swapped = [], []
    pad = zeros(MLA_Q_RANK, MLA_PAIR_W - 2 * (MLA_NOPE_DIM + MLA_ROPE_DIM))
    for p in range(N_MLA_PAIRS):
        a, b = 2 * p, 2 * p + 1
        plain += [wq[:, a, :MLA_NOPE_DIM], wq[:, b, :MLA_NOPE_DIM], wq[:, a, MLA_NOPE_DIM:], wq[:, b, MLA_NOPE_DIM:], pad]
        swapped += [zeros(MLA_Q_RANK, 2 * MLA_NOPE_DIM), _swap_halves(wq[:, a, MLA_NOPE_DIM:]),
                    _swap_halves(wq[:, b, MLA_NOPE_DIM:]), pad]
    return dict(
        w_all=w_all,
        b_f=jnp.pad(b_forget, (0, LANE - N_FOX_HEADS)).reshape(1, LANE),
        g_q=mla_q_norm.reshape(1, -1), g_kv=mla_kv_norm.reshape(1, -1),
        w_uq2=jnp.concatenate(plain + swapped, axis=1).astype(BF16),
        w_ukv=jnp.concatenate([w_mla_uk, w_mla_uv], axis=1).astype(BF16),
        lams=lams, subln=jnp.tile(diff_subln, 2).reshape(1, LANE),
        w_out=w_out.astype(BF16),
        g_mix_pre=norm_mix_pre.reshape(1, -1), g_mix_post=norm_mix_post.reshape(1, -1),
        g_ffn_pre=norm_ffn_pre.reshape(1, -1), g_ffn_post=norm_ffn_post.reshape(1, -1),
        g_ple_pre=norm_ple_pre.reshape(1, -1), g_ple_post=norm_ple_post.reshape(1, -1),
        w_up=w_ffn_up.astype(BF16), conv_w=ffn_conv_w, conv_b=ffn_conv_b.reshape(1, -1),
        w_down=w_ffn_down.astype(BF16), w_ple_gate=w_ple_gate.astype(BF16), w_ple_proj=w_ple_proj.astype(BF16))


def _rope_tables(pos):
    half = MLA_ROPE_DIM // 2
    inv_freq = ROPE_THETA ** (-jnp.arange(half, dtype=F32) / half)
    ang = pos.astype(F32)[:, None] * inv_freq[None, :]
    cos, sin = jnp.cos(ang), jnp.sin(ang)
    t = pos.shape[0]
    pad = jnp.zeros((t, MLA_PAIR_W - LANE - 2 * MLA_ROPE_DIM), F32)
    cos_t = jnp.concatenate([jnp.ones((t, LANE), F32), cos, cos, cos, cos, pad], axis=1)
    sin_t = jnp.concatenate([jnp.zeros((t, LANE), F32), -sin, sin, -sin, sin, pad], axis=1)
    return cos_t, sin_t


def _alibi_slopes():
    s = 2.0 ** (-8.0 * np.arange(1, N_DIFF_HEADS + 1) / N_DIFF_HEADS)
    return jnp.asarray(np.broadcast_to(s.reshape(N_DIFF_HEADS // 2, 2, 1), (N_DIFF_HEADS // 2, 2, LANE)), dtype=F32)


def _tile_rows(n, pref):
    t = min(n, pref)
    assert n % t == 0
    return t


def _layer(h, p, lw, cos, sin, cache, conv_left, lam_init, *, batch, seq, past, tq):
    n = h.shape[0]
    tm = _tile_rows(n, 512)
    (fox_k, fox_v, logf, ckv, krope, diff_k, diff_v,
     fox_bf, diff_bf, mla_q, mla_k, mla_v) = _proj_call(h, lw, cos, sin, tm)
    rows = (fox_k, fox_v, logf, ckv, krope, diff_k, diff_v)
    b3 = lambda a: a.reshape(batch, seq, a.shape[-1])
    diff_extras = [_alibi_slopes(), lw["lams"], lw["subln"]]
    diff_specs = [pl.BlockSpec((1, 2, LANE), lambda bb, hp, i: (hp, 0, 0)),
                  pl.BlockSpec((4, DIFF_QK_DIM), lambda bb, hp, i: (0, 0)),
                  pl.BlockSpec((1, LANE), lambda bb, hp, i: (0, 0))]
    fb, db = b3(fox_bf), b3(diff_bf)
    if cache is None:
        common = dict(tq=tq, tkd=tq, qoff=0, blocked=True)
        o_a = _attn_call(_fox_kernel, "fox_attn", fb, 0, fb, 3, _forget_bias_call(b3(logf)),
                         _values_t(fb[..., 2 * FOX_WIDTH:], tq), [], [],
                         npairs=N_FOX_HEADS // 2, qw=LANE, chains=2, **common)
        o_b = _attn_call(_mla_kernel, "mla_attn", b3(mla_q), 0, b3(mla_k), 0, None, _values_t(b3(mla_v), tq), [], [],
                         npairs=N_MLA_PAIRS, qw=MLA_PAIR_W, chains=2, **common)
        o_c = _attn_call(_diff_kernel, "diff_attn", db, 0, db, 2, _alibi_bias_block(seq),
                         _values_t(db[..., 2 * DIFF_WIDTH:], tq), diff_extras, diff_specs,
                         npairs=N_DIFF_HEADS // 2, qw=LANE, chains=4, lam_init=lam_init, **common)
    else:
        c_fox_k, c_fox_v, c_logf, c_ckv, c_krope, c_diff_k, c_diff_v = cache
        tk = -(-(past + seq) // LANE) * LANE
        padk = lambda a: jnp.pad(a, ((0, 0), (0, tk - past - seq), (0, 0)))
        cat = lambda old, new: padk(jnp.concatenate([old.astype(new.dtype), new], axis=1))
        flat = lambda a: a.reshape(batch, past, -1)
        padq = lambda a: jnp.pad(a, ((0, 0), (0, tq - seq), (0, 0)))
        kf = cat(flat(c_fox_k), fb[..., FOX_WIDTH:2 * FOX_WIDTH])
        vf = cat(flat(c_fox_v), fb[..., 2 * FOX_WIDTH:])
        kd = cat(flat(c_diff_k), db[..., DIFF_WIDTH:2 * DIFF_WIDTH])
        vd = cat(flat(c_diff_v), db[..., 2 * DIFF_WIDTH:])
        kn_c, v_c = _kvup_call(c_ckv.reshape(batch * past, MLA_KV_RANK), lw["w_ukv"], _tile_rows(batch * past, 1024))
        kr_c = c_krope.astype(BF16)
        krp_c = jnp.concatenate([kr_c, kr_c, jnp.zeros((batch, past, LANE - 2 * MLA_ROPE_DIM), BF16)], axis=-1)
        km_c = jnp.concatenate([kn_c.reshape(batch, past, N_MLA_PAIRS, LANE),
                                jnp.broadcast_to(krp_c[:, :, None, :], (batch, past, N_MLA_PAIRS, LANE))],
                               axis=-1).reshape(batch, past, MLA_QK_W)
        km = cat(km_c, b3(mla_k))
        vm = cat(v_c.reshape(batch, past, MLA_WIDTH), b3(mla_v))
        common = dict(tq=tq, tkd=tk, qoff=past, blocked=False)
        o_a = _attn_call(_fox_kernel, "fox_attn_s", padq(fb[..., :FOX_WIDTH]), 0, kf, 0,
                         _forget_bias_call(cat(c_logf, b3(logf))), _values_t(vf, tk), [], [],
                         npairs=N_FOX_HEADS // 2, qw=LANE, chains=2, **common)
        o_b = _attn_call(_mla_kernel, "mla_attn_s", padq(b3(mla_q)), 0, km, 0, None, _values_t(vm, tk), [], [],
                         npairs=N_MLA_PAIRS, qw=MLA_PAIR_W, chains=2, **common)
        o_c = _attn_call(_diff_kernel, "diff_attn_s", padq(db[..., :DIFF_WIDTH]), 0, kd, 0, _alibi_bias_block(tk),
                         _values_t(vd, tk), diff_extras, diff_specs,
                         npairs=N_DIFF_HEADS // 2, qw=LANE, chains=4, lam_init=lam_init, **common)
        o_a, o_b, o_c = o_a[:, :seq], o_b[:, :seq], o_c[:, :seq]
    f2 = lambda a: a.reshape(n, a.shape[-1])
    h, conv_state = _ffn_call(f2(o_a), f2(o_b), f2(o_c), h, p, lw, conv_left, seq, tm, 1024)
    return h, rows, conv_state


def kernel(x_prompt, x_sample, cache_fox_k, cache_fox_v, cache_fox_logf, cache_mla_ckv, cache_mla_krope, cache_diff_k, cache_diff_v, state_ffn_conv, p_prompt, p_sample, w_in, b_forget, mla_q_norm, w_mla_uq, mla_kv_norm, w_mla_uk, w_mla_uv, diff_lambda_q1, diff_lambda_k1, diff_lambda_q2, diff_lambda_k2, diff_subln, w_out, norm_mix_pre, norm_mix_post, norm_ffn_pre, norm_ffn_post, norm_ple_pre, norm_ple_post, w_ffn_up, ffn_conv_w, ffn_conv_b, w_ffn_down, w_ple_gate, w_ple_proj):
    bp, sp, _ = x_prompt.shape
    bs, ts, _ = x_sample.shape
    depth, _, past = cache_fox_k.shape[:3]
    assert past % CHUNK == 0 and ts <= CHUNK and sp % LANE == 0
    tq = _tile_rows(sp, 512)
    tms = _tile_rows(bs * ts, 512)

    cos_p, sin_p = _rope_tables(jnp.arange(sp))
    cos_s, sin_s = _rope_tables(past + jnp.arange(ts))
    cos_s, sin_s = jnp.tile(cos_s, (tms // ts, 1)), jnp.tile(sin_s, (tms // ts, 1))

    hp = x_prompt.reshape(bp * sp, D_MODEL)
    hs = x_sample.reshape(bs * ts, D_MODEL)
    rows_p, rows_s, conv_p, conv_s = [], [], [], []
    for l in range(depth):
        lams = jnp.stack([diff_lambda_q1[l], diff_lambda_k1[l], diff_lambda_q2[l], diff_lambda_k2[l]], axis=0)
        lw = _pack_layer(w_in[l], b_forget[l], mla_q_norm[l], w_mla_uq[l], mla_kv_norm[l], w_mla_uk[l], w_mla_uv[l],
                         lams, diff_subln[l], w_out[l], norm_mix_pre[l], norm_mix_post[l], norm_ffn_pre[l],
                         norm_ffn_post[l], norm_ple_pre[l], norm_ple_post[l], w_ffn_up[l], ffn_conv_w[l],
                         ffn_conv_b[l], w_ffn_down[l], w_ple_gate[l], w_ple_proj[l])
        lam_init = 0.8 - 0.6 * math.exp(-0.3 * l)
        hp, rp, cp = _layer(hp, p_prompt[l].reshape(bp * sp, PLE_DIM), lw, cos_p, sin_p, None,
                            jnp.zeros((bp, CONV_WIDTH - 1, D_FF), F32), lam_init,
                            batch=bp, seq=sp, past=0, tq=tq)
        cache_l = (cache_fox_k[l], cache_fox_v[l], cache_fox_logf[l], cache_mla_ckv[l], cache_mla_krope[l],
                   cache_diff_k[l], cache_diff_v[l])
        hs, rs, cs = _layer(hs, p_sample[l].reshape(bs * ts, PLE_DIM), lw, cos_s, sin_s, cache_l,
                            state_ffn_conv[l], lam_init, batch=bs, seq=ts, past=past, tq=LANE)
        rows_p.append(rp)
        rows_s.append(rs)
        conv_p.append(cp)
        conv_s.append(cs)

    def stack_rows(per_layer, batch, seq):
        fk, fv, lf, ckv, kr, dk, dv = (jnp.stack(a, axis=0) for a in zip(*per_layer))
        lead = (depth, batch, seq)
        return (fk.reshape(lead + (N_FOX_HEADS, HEAD_DIM)), fv.reshape(lead + (N_FOX_HEADS, HEAD_DIM)),
                lf.reshape(lead + (N_FOX_HEADS,)), ckv.reshape(lead + (MLA_KV_RANK,)),
                kr.reshape(lead + (MLA_ROPE_DIM,)), dk.reshape(lead + (N_DIFF_HEADS, 2 * DIFF_QK_DIM)),
                dv.reshape(lead + (N_DIFF_HEADS, DIFF_V_DIM)))

    out_p = stack_rows(rows_p, bp, sp)
    out_s = stack_rows(rows_s, bs, ts)
    return ((hp.reshape(bp, sp, D_MODEL), hs.reshape(bs, ts, D_MODEL)) + out_p + (jnp.stack(conv_p, axis=0),)
            + out_s + (jnp.stack(conv_s, axis=0),))
```

```python
import functools
import math

import jax
import jax.numpy as jnp
import numpy as np
from jax import lax
from jax.experimental import pallas as pl
from jax.experimental.pallas import tpu as pltpu

F32 = jnp.float32
BF16 = jnp.bfloat16

D_MODEL = 1024
HEAD_DIM = 64
N_FOX_HEADS = 6
N_MLA_HEADS = 6
N_DIFF_HEADS = 4
MLA_Q_RANK = 384
MLA_KV_RANK = 256
MLA_NOPE_DIM = 64
MLA_ROPE_DIM = 32
MLA_V_DIM = 64
DIFF_QK_DIM = 32
DIFF_V_DIM = 64
FOX_WIDTH = N_FOX_HEADS * HEAD_DIM
MLA_WIDTH = N_MLA_HEADS * MLA_V_DIM
DIFF_WIDTH = N_DIFF_HEADS * DIFF_V_DIM
D_FF = 4 * D_MODEL
CONV_WIDTH = 3
PLE_DIM = 256
CHUNK = 64
ROPE_THETA = 10000.0
RMS_EPS = 1e-6
NEG_INF = -1e30
LOG2E = math.log2(math.e)

OFF_FOX_Q = 0
OFF_FOX_F = 3 * FOX_WIDTH
OFF_MLA_CQ = OFF_FOX_F + N_FOX_HEADS
OFF_MLA_CKV = OFF_MLA_CQ + MLA_Q_RANK
OFF_MLA_KR = OFF_MLA_CKV + MLA_KV_RANK
OFF_DIFF_Q = OFF_MLA_KR + MLA_ROPE_DIM
IN_WIDTH = OFF_DIFF_Q + 2 * N_DIFF_HEADS * 2 * DIFF_QK_DIM + DIFF_WIDTH

LANE = 128
V7X_VMEM_BYTES = 64 * 1024 * 1024
VMEM_LIMIT = (V7X_VMEM_BYTES * 7) // 8

C_FOX = 0
C_DIFF = C_FOX + 3 * FOX_WIDTH
C_CQ = C_DIFF + 3 * DIFF_WIDTH
C_CKV = C_CQ + MLA_Q_RANK
C_KR = C_CKV + MLA_KV_RANK
C_F = C_KR + 2 * LANE
W_ALL = C_F + LANE
N_MLA_PAIRS = N_MLA_HEADS // 2
MLA_PAIR_W = 2 * LANE
MLA_QK_W = N_MLA_PAIRS * MLA_PAIR_W

FOX_QSCALE = HEAD_DIM ** -0.5 * LOG2E
MLA_QSCALE = (MLA_NOPE_DIM + MLA_ROPE_DIM) ** -0.5 * LOG2E
DIFF_QSCALE = DIFF_QK_DIM ** -0.5 * LOG2E


def _cparams(sem):
    return pltpu.CompilerParams(dimension_semantics=sem, vmem_limit_bytes=VMEM_LIMIT)


def _rms(x, g):
    return x * lax.rsqrt(jnp.mean(x * x, axis=-1, keepdims=True) + RMS_EPS) * g


def _dot(a, b):
    return jnp.dot(a, b, preferred_element_type=F32)


def _dot_nt(a, b):
    return lax.dot_general(a, b, (((1,), (1,)), ((), ())), preferred_element_type=F32)


def _proj_kernel(h_ref, gpre_ref, w_ref, bf_ref, gq_ref, gkv_ref, wuq_ref, wukv_ref, cos_ref, sin_ref,
                 foxk_ref, foxv_ref, logf_ref, ckv_ref, krope_ref, diffk_ref, diffv_ref,
                 foxbf_ref, diffbf_ref, mlaq_ref, mlak_ref, mlav_ref):
    xn = _rms(h_ref[...], gpre_ref[...]).astype(BF16)
    cos = cos_ref[...]
    sin = sin_ref[...]

    z = _dot(xn, w_ref[:, C_FOX:C_FOX + 3 * FOX_WIDTH])
    foxbf_ref[:, :FOX_WIDTH] = (z[:, :FOX_WIDTH] * FOX_QSCALE).astype(BF16)
    foxbf_ref[:, FOX_WIDTH:] = z[:, FOX_WIDTH:].astype(BF16)
    foxk_ref[...] = z[:, FOX_WIDTH:2 * FOX_WIDTH]
    foxv_ref[...] = z[:, 2 * FOX_WIDTH:3 * FOX_WIDTH]

    z = _dot(xn, w_ref[:, C_DIFF:C_DIFF + 3 * DIFF_WIDTH])
    diffbf_ref[:, :DIFF_WIDTH] = (z[:, :DIFF_WIDTH] * DIFF_QSCALE).astype(BF16)
    diffbf_ref[:, DIFF_WIDTH:] = z[:, DIFF_WIDTH:].astype(BF16)
    diffk_ref[...] = z[:, DIFF_WIDTH:2 * DIFF_WIDTH]
    diffv_ref[...] = z[:, 2 * DIFF_WIDTH:3 * DIFF_WIDTH]

    cq = _rms(_dot(xn, w_ref[:, C_CQ:C_CQ + MLA_Q_RANK]), gq_ref[...]).astype(BF16)
    q2 = _dot(cq, wuq_ref[...])
    for p in range(N_MLA_PAIRS):
        lo = p * MLA_PAIR_W
        roped = q2[:, lo:lo + MLA_PAIR_W] * cos + q2[:, MLA_QK_W + lo:MLA_QK_W + lo + MLA_PAIR_W] * sin
        mlaq_ref[:, lo:lo + MLA_PAIR_W] = (roped * MLA_QSCALE).astype(BF16)

    ckv = _rms(_dot(xn, w_ref[:, C_CKV:C_CKV + MLA_KV_RANK]), gkv_ref[...])
    ckv_ref[...] = ckv
    kv = _dot(ckv.astype(BF16), wukv_ref[...])
    mlav_ref[...] = kv[:, MLA_WIDTH:2 * MLA_WIDTH].astype(BF16)
    zkr = _dot(xn, w_ref[:, C_KR:C_KR + 2 * LANE])
    krp = zkr[:, :LANE] * cos[:, LANE:] + zkr[:, LANE:] * sin[:, LANE:]
    krope_ref[...] = krp[:, :MLA_ROPE_DIM]
    krp16 = krp.astype(BF16)
    for p in range(N_MLA_PAIRS):
        lo = p * MLA_PAIR_W
        mlak_ref[:, lo:lo + LANE] = kv[:, p * LANE:(p + 1) * LANE].astype(BF16)
        mlak_ref[:, lo + LANE:lo + 2 * LANE] = krp16

    zf = _dot(xn, w_ref[:, C_F:C_F + LANE]) + bf_ref[...]
    logf = -(jnp.maximum(-zf, 0.0) + jnp.log1p(jnp.exp(-jnp.abs(zf))))
    logf_ref[...] = logf[:, :N_FOX_HEADS]


def _proj_call(h, lw, cos, sin, tm):
    n = h.shape[0]
    ntab = cos.shape[0] // tm
    row = lambda i: (i, 0)
    full = lambda i: (0, 0)
    tab = lambda i: (i % ntab, 0)
    in_specs = [
        pl.BlockSpec((tm, D_MODEL), row),
        pl.BlockSpec((1, D_MODEL), full),
        pl.BlockSpec((D_MODEL, W_ALL), full),
        pl.BlockSpec((1, LANE), full),
        pl.BlockSpec((1, MLA_Q_RANK), full),
        pl.BlockSpec((1, MLA_KV_RANK), full),
        pl.BlockSpec((MLA_Q_RANK, 2 * MLA_QK_W), full),
        pl.BlockSpec((MLA_KV_RANK, 2 * MLA_WIDTH), full),
        pl.BlockSpec((tm, MLA_PAIR_W), tab),
        pl.BlockSpec((tm, MLA_PAIR_W), tab),
    ]
    widths = [(FOX_WIDTH, F32), (FOX_WIDTH, F32), (N_FOX_HEADS, F32), (MLA_KV_RANK, F32), (MLA_ROPE_DIM, F32),
              (DIFF_WIDTH, F32), (DIFF_WIDTH, F32),
              (3 * FOX_WIDTH, BF16), (3 * DIFF_WIDTH, BF16), (MLA_QK_W, BF16), (MLA_QK_W, BF16), (MLA_WIDTH, BF16)]
    out_shape = [jax.ShapeDtypeStruct((n, w), dt) for w, dt in widths]
    out_specs = [pl.BlockSpec((tm, w), row) for w, _ in widths]
    return pl.pallas_call(
        _proj_kernel, grid=(n // tm,), in_specs=in_specs, out_specs=out_specs, out_shape=out_shape,
        compiler_params=_cparams(("parallel",)), name="proj",
    )(h, lw["g_mix_pre"], lw["w_all"], lw["b_f"], lw["g_q"], lw["g_kv"], lw["w_uq2"], lw["w_ukv"], cos, sin)


def _kvup_kernel(ckv_ref, wukv_ref, kn_ref, v_ref):
    kv = _dot(ckv_ref[...].astype(BF16), wukv_ref[...])
    kn_ref[...] = kv[:, :MLA_WIDTH].astype(BF16)
    v_ref[...] = kv[:, MLA_WIDTH:].astype(BF16)


def _kvup_call(ckv, w_ukv, tm):
    n = ckv.shape[0]
    row = lambda i: (i, 0)
    return pl.pallas_call(
        _kvup_kernel, grid=(n // tm,),
        in_specs=[pl.BlockSpec((tm, MLA_KV_RANK), row), pl.BlockSpec((MLA_KV_RANK, 2 * MLA_WIDTH), lambda i: (0, 0))],
        out_specs=[pl.BlockSpec((tm, MLA_WIDTH), row), pl.BlockSpec((tm, MLA_WIDTH), row)],
        out_shape=[jax.ShapeDtypeStruct((n, MLA_WIDTH), BF16)] * 2,
        compiler_params=_cparams(("parallel",)), name="kvup",
    )(ckv, w_ukv)


AUG_STRIDE = 8


def _keep_bf16_bits(x):
    bits = lax.bitcast_convert_type(x, jnp.uint32) & jnp.uint32(0xFFFF0000)
    return lax.bitcast_convert_type(bits, F32)


def _split3(x):
    x1 = _keep_bf16_bits(x)
    r = x - x1
    x2 = _keep_bf16_bits(r)
    return x1, x2, r - x2


def _place3(x1, x2, x3):
    lane = lax.broadcasted_iota(jnp.int32, (1, LANE), 1)
    return jnp.where(lane < AUG_STRIDE, x1,
                     jnp.where(lane < 2 * AUG_STRIDE, pltpu.roll(x2, AUG_STRIDE, 1),
                               pltpu.roll(x3, 2 * AUG_STRIDE, 1))).astype(BF16)


def _aug_selector(h, rows):
    lane = lax.broadcasted_iota(jnp.int32, (rows, LANE), 1)
    hit = (lane == h) | (lane == AUG_STRIDE + h) | (lane == 2 * AUG_STRIDE + h)
    return jnp.where(hit, 1.0, 0.0).astype(BF16)


def _forget_bias_kernel(x_ref, tri_ref, o_ref, carry_sc, *, tb):
    @pl.when(pl.program_id(1) == 0)
    def _():
        carry_sc[...] = jnp.zeros_like(carry_sc)

    terms = jnp.concatenate(_split3(x_ref[0]), axis=1).astype(BF16)
    sums = _dot(tri_ref[...], terms)
    cs = carry_sc[0:1, :] + sums[:, :LANE] + sums[:, LANE:2 * LANE] + sums[:, 2 * LANE:]
    carry_sc[...] = jnp.broadcast_to(cs[tb - 1:tb, :], carry_sc.shape)
    o_ref[0] = _place3(*_split3(cs * (-LOG2E)))


def _forget_bias_call(logf):
    b, t, hh = logf.shape
    tb = max(d for d in range(LANE, 4 * LANE + 1, LANE) if t % d == 0)
    x = jnp.pad(logf, ((0, 0), (0, 0), (0, LANE - hh)))
    tri = jnp.tril(jnp.ones((tb, tb), BF16))
    blk = pl.BlockSpec((1, tb, LANE), lambda i, j: (i, j, 0))
    return pl.pallas_call(
        functools.partial(_forget_bias_kernel, tb=tb), grid=(b, t // tb),
        in_specs=[blk, pl.BlockSpec((tb, tb), lambda i, j: (0, 0))], out_specs=blk,
        out_shape=jax.ShapeDtypeStruct((b, t, LANE), BF16),
        scratch_shapes=[pltpu.VMEM((8, LANE), F32)],
        compiler_params=_cparams(("parallel", "arbitrary")), name="forget_bias",
    )(x, tri)


def _alibi_bias_block(t_k):
    slopes = 2.0 ** (-8.0 * np.arange(1, N_DIFF_HEADS + 1) / N_DIFF_HEADS)
    b = jnp.zeros((t_k, LANE), F32).at[:, :N_DIFF_HEADS].set(
        jnp.arange(t_k, dtype=F32)[:, None] * jnp.asarray(slopes * LOG2E, F32)[None, :])
    x1, x2, x3 = _split3(b)
    lane = jnp.arange(LANE)[None, :]
    placed = jnp.where(lane < AUG_STRIDE, x1, jnp.where(lane < 2 * AUG_STRIDE, jnp.roll(x2, AUG_STRIDE, 1),
                                                        jnp.roll(x3, 2 * AUG_STRIDE, 1)))
    return placed.astype(BF16)[None]


ROW_CHUNK = CHUNK
STRIP = 4 * LANE
ONES_ROWS = 16


def _masked_chunks(s_ref, c, mask, tk, tq):
    chunks = [(r0, "all" if mask is None else mask[0](r0, 0, tq)) for r0 in range(0, tk, ROW_CHUNK)]

    def logits(r0, vis):
        x = s_ref[c, r0:r0 + ROW_CHUNK, :]
        return x if vis == "all" else mask[1](r0, 0, x)
    return chunks, logits


def _softmax_max(s_ref, m_ref, c, mask, tk, tq):
    chunks, logits = _masked_chunks(s_ref, c, mask, tk, tq)
    mrun = jnp.full((8, tq), NEG_INF, F32)
    for r0, vis in chunks:
        if vis != "none":
            mrun = jnp.maximum(mrun, jnp.max(logits(r0, vis).reshape(ROW_CHUNK // 8, 8, tq), axis=0))
    m_old = m_ref[c]
    return m_old, jnp.maximum(m_old, jnp.max(mrun, axis=0, keepdims=True))


def _softmax_exp(s_ref, p_ref, m_ref, alpha_ref, c, mask, tk, tq, m_old, m_new):
    chunks, logits = _masked_chunks(s_ref, c, mask, tk, tq)
    for r0, vis in chunks:
        if vis == "none":
            p_ref[c, r0:r0 + ROW_CHUNK, :] = jnp.zeros((ROW_CHUNK, tq), BF16)
        else:
            p_ref[c, r0:r0 + ROW_CHUNK, :] = jnp.exp2(logits(r0, vis) - m_new).astype(BF16)
    m_ref[c] = m_new
    alpha_ref[c] = jnp.exp2(m_old - m_new)


def _apply_values(p_ref, alpha_ref, acc_ref, c, vt):
    acc_ref[c] = alpha_ref[c] * acc_ref[c] + _dot(vt, p_ref[c])


def _causal_mask(qoff):
    def visibility(r0, c0, sw):
        if r0 + ROW_CHUNK - 1 <= c0 + qoff:
            return "all"
        return "none" if r0 > c0 + sw - 1 + qoff else "some"

    def apply(r0, c0, x):
        kk, qq = _key_query_iota(*x.shape)
        return jnp.where(kk - qq <= c0 + qoff - r0, x, NEG_INF)
    return visibility, apply


def _chunk_visibility(r0, c0, sw, qoff):
    if r0 // CHUNK <= (c0 + qoff) // CHUNK:
        return "all"
    return "none" if r0 // CHUNK > (c0 + sw - 1 + qoff) // CHUNK else "some"


def _chunk_mask(r0, c0, sw, qoff):
    qq = lax.broadcasted_iota(jnp.int32, (1, sw), 1)
    return r0 // CHUNK <= (qq + (c0 + qoff)) // CHUNK


def _init_states(m_ref, acc_ref):
    m_ref[...] = jnp.full(m_ref.shape, NEG_INF, F32)
    acc_ref[...] = jnp.zeros(acc_ref.shape, F32)


def _normalized(acc_ref, c, dv):
    return acc_ref[c, :dv, :] / acc_ref[c, dv:dv + 1, :]


def _key_query_iota(tk, tq):
    return (lax.broadcasted_iota(jnp.int32, (tk, tq), 0), lax.broadcasted_iota(jnp.int32, (tk, tq), 1))


def _store_pair(o_ref, o0, o1, row_scale=None):
    out = jnp.concatenate([o0, o1], axis=0).T
    if row_scale is not None:
        out = out * row_scale
    o_ref[0] = out.astype(BF16)


def _block_start(j, size):
    return j * size if isinstance(j, int) else pl.multiple_of(j * size, size)


N_BUF = 3


def _attend(i, blocked, chains, qk_fn, max_fn, exp_fn, pv_fn):
    def step(qk, pv, buf, diag):
        if qk is not None:
            all_chains(qk_fn, *qk)
        if pv is not None:
            all_chains(pv_fn, *pv)
        for c in range(chains):
            exp_fn(buf, c, diag, *max_fn(buf, c, diag))

    def all_chains(fn, *args):
        for c in range(chains):
            fn(*args, c)

    if not blocked:
        all_chains(qk_fn, 0, 0)
        step(None, None, 0, True)
        all_chains(pv_fn, 0, 0)
        return
    last = N_BUF - 1
    lead = i % N_BUF

    def single(j, carry):
        all_chains(qk_fn, last, j)
        step(None, None, last, False)
        all_chains(pv_fn, last, j)
        return carry

    lax.fori_loop(0, lead, single, 0)
    all_chains(qk_fn, 0, lead)
    all_chains(qk_fn, 1, jnp.minimum(lead + 1, i))

    def rotate(t, carry):
        b0 = lead + N_BUF * t
        for u in range(N_BUF):
            step(((u + 2) % N_BUF, jnp.minimum(b0 + u + 2, i)), (u - 1, b0 + u - 1) if u > 0 else None, u, False)
        all_chains(pv_fn, last, b0 + last)
        return carry

    lax.fori_loop(0, (i - lead) // N_BUF, rotate, 0)
    step(None, None, 0, True)
    all_chains(pv_fn, 0, i)


def _vt_rows(vt_ref, j, e, dv):
    return vt_ref[0, 0, j, e * (dv + ONES_ROWS):(e + 1) * (dv + ONES_ROWS), :]


def _split_scratch(scratch):
    nbuf = (len(scratch) - 3) // 2
    return (scratch[:nbuf], scratch[nbuf:2 * nbuf]) + tuple(scratch[2 * nbuf:])


def _chain_fns(scratch, vt_ref, mask_of, dv, head_of, tkd, tq):
    s_sc, p_sc, m_sc, alpha_sc, acc_sc = _split_scratch(scratch)

    def max_fn(buf, c, diag):
        return _softmax_max(s_sc[buf], m_sc, c, mask_of(c) if diag else None, tkd, tq)

    def exp_fn(buf, c, diag, m_old, m_new):
        _softmax_exp(s_sc[buf], p_sc[buf], m_sc, alpha_sc, c, mask_of(c) if diag else None, tkd, tq, m_old, m_new)

    def pv_fn(buf, j, c):
        _apply_values(p_sc[buf], alpha_sc, acc_sc, c, _vt_rows(vt_ref, j, head_of(c), dv))
    return max_fn, exp_fn, pv_fn


def _fox_kernel(q_ref, k_ref, aug_ref, vt_ref, o_ref, *scratch, tq, tkd, qoff, blocked):
    s_sc, p_sc, m_sc, alpha_sc, acc_sc = _split_scratch(scratch)
    hp, i = pl.program_id(1), pl.program_id(2)
    _init_states(m_sc, acc_sc)
    q = q_ref[0]
    lane = lax.broadcasted_iota(jnp.int32, (1, LANE), 1)
    qs = [jnp.concatenate([jnp.where((lane >= e * HEAD_DIM) & (lane < (e + 1) * HEAD_DIM), q, jnp.zeros_like(q)),
                           _aug_selector(2 * hp + e, tq)], axis=1) for e in range(2)]

    def qk_fn(buf, j, e):
        rows = pl.ds(_block_start(j, tq), tkd)
        k = jnp.concatenate([k_ref[0, rows, :], aug_ref[0, rows, :]], axis=1)
        s_sc[buf][e] = _dot_nt(k, qs[e])

    _attend(i, blocked, 2, qk_fn, *_chain_fns(scratch, vt_ref, lambda e: _causal_mask(qoff), HEAD_DIM,
                                              lambda e: e, tkd, tq))
    _store_pair(o_ref, *(_normalized(acc_sc, e, HEAD_DIM) for e in range(2)))


def _mla_kernel(q_ref, k_ref, vt_ref, o_ref, *scratch, tq, tkd, qoff, blocked):
    s_sc, p_sc, m_sc, alpha_sc, acc_sc = _split_scratch(scratch)
    i = pl.program_id(2)
    _init_states(m_sc, acc_sc)
    q = q_ref[0]
    lane2 = lax.broadcasted_iota(jnp.int32, (1, MLA_PAIR_W), 1)
    sel = []
    for e in range(2):
        nope = (lane2 >= e * MLA_NOPE_DIM) & (lane2 < (e + 1) * MLA_NOPE_DIM)
        rope = (lane2 >= LANE + e * MLA_ROPE_DIM) & (lane2 < LANE + (e + 1) * MLA_ROPE_DIM)
        sel.append(jnp.where(nope | rope, q, jnp.zeros_like(q)))

    def qk_fn(buf, j, e):
        s_sc[buf][e] = _dot_nt(k_ref[0, pl.ds(_block_start(j, tq), tkd), :], sel[e])

    mask = (lambda r0, c0, sw: _chunk_visibility(r0, c0, sw, qoff),
            lambda r0, c0, x: jnp.where(_chunk_mask(r0, c0, x.shape[1], qoff), x, NEG_INF))
    _attend(i, blocked, 2, qk_fn, *_chain_fns(scratch, vt_ref, lambda e: mask, MLA_V_DIM, lambda e: e, tkd, tq))
    _store_pair(o_ref, *(_normalized(acc_sc, e, MLA_V_DIM) for e in range(2)))


def _diff_kernel(q_ref, k_ref, aug_ref, vt_ref, slope_ref, lam_ref, subln_ref, o_ref, *scratch,
                 tq, tkd, qoff, blocked, lam_init):
    s_sc, p_sc, m_sc, alpha_sc, acc_sc = _split_scratch(scratch)
    hp, i = pl.program_id(1), pl.program_id(2)
    _init_states(m_sc, acc_sc)
    q = q_ref[0]
    lane = lax.broadcasted_iota(jnp.int32, (1, LANE), 1)
    sel = [jnp.concatenate([jnp.where((lane >= (2 * e + t) * DIFF_QK_DIM) & (lane < (2 * e + t + 1) * DIFF_QK_DIM),
                                      q, jnp.zeros_like(q)), _aug_selector(2 * hp + e, tq)], axis=1)
           for e in range(2) for t in range(2)]
    slopes = [slope_ref[0, e:e + 1, 0:1] * LOG2E for e in range(2)]

    def qk_fn(buf, j, c):
        rows = pl.ds(_block_start(j, tq), tkd)
        k = jnp.concatenate([k_ref[0, rows, :], aug_ref[0, rows, :]], axis=1)
        s_sc[buf][c] = _dot_nt(k, sel[c])

    def mask_of(c):
        def visibility(r0, c0, sw):
            vis = _chunk_visibility(r0, c0, sw, qoff)
            return "some" if vis == "all" and r0 + ROW_CHUNK - 1 > c0 + qoff else vis

        def apply(r0, c0, x):
            kk, qq = _key_query_iota(*x.shape)
            ahead = jnp.maximum(kk - qq + (r0 - c0 - qoff), 0).astype(F32)
            return jnp.where(_chunk_mask(r0, c0, x.shape[1], qoff), x - (2.0 * slopes[c // 2]) * ahead, NEG_INF)
        return visibility, apply

    _attend(i, blocked, 4, qk_fn, *_chain_fns(scratch, vt_ref, mask_of, DIFF_V_DIM, lambda c: c // 2, tkd, tq))
    res = [_normalized(acc_sc, c, DIFF_V_DIM) for c in range(4)]
    lq = lam_ref[...]
    lam = (jnp.exp(jnp.sum(lq[0:1] * lq[1:2], axis=-1, keepdims=True))
           - jnp.exp(jnp.sum(lq[2:3] * lq[3:4], axis=-1, keepdims=True)) + lam_init)
    outs = []
    for e in range(2):
        o = res[2 * e] - lam * res[2 * e + 1]
        outs.append(o * lax.rsqrt(jnp.mean(o * o, axis=0, keepdims=True) + RMS_EPS))
    _store_pair(o_ref, outs[0], outs[1], subln_ref[...] * (1.0 - lam_init))


def _attn_call(kernel, name, q, q_cb, k, k_cb, aug, vt, extras, extra_specs, *, npairs, qw, tq, tkd, qoff, blocked,
               chains, **kw):
    b, t_q = q.shape[0], q.shape[1]
    t_k = k.shape[1]
    acc_rows = LANE // 2 + ONES_ROWS
    nbuf = N_BUF if blocked else 1
    scratch = ([pltpu.VMEM((chains, tkd, tq), F32)] * nbuf + [pltpu.VMEM((chains, tkd, tq), BF16)] * nbuf
               + [pltpu.VMEM((chains, 1, tq), F32), pltpu.VMEM((chains, 1, tq), F32),
                  pltpu.VMEM((chains, acc_rows, tq), F32)])
    in_specs = [pl.BlockSpec((1, tq, qw), lambda bb, hp, i: (bb, i, q_cb + hp)),
                pl.BlockSpec((1, t_k, qw), lambda bb, hp, i: (bb, 0, k_cb + hp))]
    args = [q, k]
    if aug is not None:
        per_batch = aug.shape[0] > 1
        in_specs.append(pl.BlockSpec((1, t_k, LANE), lambda bb, hp, i: (bb if per_batch else 0, 0, 0)))
        args.append(aug)
    in_specs.append(pl.BlockSpec((1, 1) + vt.shape[2:], lambda bb, hp, i: (bb, hp, 0, 0, 0)))
    return pl.pallas_call(
        functools.partial(kernel, tq=tq, tkd=tkd, qoff=qoff, blocked=blocked, **kw),
        grid=(b, npairs, t_q // tq), in_specs=in_specs + extra_specs,
        out_specs=pl.BlockSpec((1, tq, LANE), lambda bb, hp, i: (bb, i, hp)),
        out_shape=jax.ShapeDtypeStruct((b, t_q, npairs * LANE), BF16), scratch_shapes=scratch,
        compiler_params=_cparams(("parallel", "parallel", "arbitrary")), name=name,
    )(*args, vt, *extras)


def _values_t(v, tkb):
    b, t, w = v.shape
    dv = LANE // 2
    vt = v.reshape(b, t // tkb, tkb, w // LANE, 2, dv).transpose(0, 3, 1, 4, 5, 2)
    ones = jnp.ones(vt.shape[:4] + (ONES_ROWS, tkb), v.dtype)
    return jnp.concatenate([vt, ones], axis=4).reshape(b, w // LANE, t // tkb, 2 * (dv + ONES_ROWS), tkb)


FFN_HALVES = 4


def _ffn_kernel(oa_ref, ob_ref, oc_ref, h_ref, wout_ref, gmix_ref,
                gpre_ref, wg_ref, wv_ref, cw_ref, cb_ref, wd_ref, gpost_ref, left_ref,
                p_ref, gple_ref, wgate_ref, wproj_ref, gplepost_ref,
                o_ref, st_ref, h1_sc, xn_sc, acc_sc, carry_sc, *, tm, tf, nsb, seq_blocks):
    i = pl.program_id(0)
    f = pl.program_id(1)
    nf = pl.num_programs(1)
    tb = tm // nsb

    @pl.when(f == 0)
    def _():
        y = (_dot(oa_ref[...], wout_ref[0:FOX_WIDTH, :])
             + _dot(ob_ref[...], wout_ref[FOX_WIDTH:FOX_WIDTH + MLA_WIDTH, :])
             + _dot(oc_ref[...], wout_ref[FOX_WIDTH + MLA_WIDTH:, :]))
        h1 = h_ref[...] + _rms(y, gmix_ref[...])
        h1_sc[...] = h1
        xn_sc[...] = _rms(h1, gpre_ref[...]).astype(BF16)
        acc_sc[...] = jnp.zeros_like(acc_sc)

    xn = xn_sc[...]
    th = tf // FFN_HALVES
    halves = [slice(a * th, (a + 1) * th) for a in range(FFN_HALVES)]
    ups = [(_dot(xn, wg_ref[:, cols]), _dot(xn, wv_ref[:, cols])) for cols in halves]
    rin = lax.broadcasted_iota(jnp.int32, (tm, 1), 0) & (tb - 1)

    def spread(rows):
        return jnp.broadcast_to(rows, (nsb, tb, th)).reshape(tm, th)

    for cols, (gate, val) in zip(halves, ups):
        left = left_ref[:, :, cols]
        if seq_blocks > 1:
            left = jnp.where(i % seq_blocks == 0, left, carry_sc[f, 0:CONV_WIDTH - 1, cols][None])
            carry_sc[f, 0:CONV_WIDTH - 1, cols] = gate[tm - (CONV_WIDTH - 1):, :]
        st_ref[f, pl.ds((i // seq_blocks) * nsb, nsb), :, cols] = (
            gate.reshape(nsb, tb, th)[:, tb - (CONV_WIDTH - 1):, :])
        l0 = spread(left[:, 0:1, :])
        l1 = spread(left[:, 1:2, :])
        g1 = jnp.where(rin == 0, l1, pltpu.roll(gate, 1, 0))
        g2 = jnp.where(rin == 0, l0, jnp.where(rin == 1, l1, pltpu.roll(gate, 2, 0)))
        conv = cw_ref[0:1, cols] * g2 + cw_ref[1:2, cols] * g1 + cw_ref[2:3, cols] * gate + cb_ref[:, cols]
        gelu = 0.5 * conv * (1.0 + jnp.tanh(math.sqrt(2.0 / math.pi) * (conv + 0.044715 * (conv * conv * conv))))
        acc_sc[...] += _dot((gelu * val).astype(BF16), wd_ref[cols, :])

    @pl.when(f == nf - 1)
    def _():
        h2 = h1_sc[...] + _rms(acc_sc[...], gpost_ref[...])
        gate = jax.nn.sigmoid(_dot(_rms(h2, gple_ref[...]).astype(BF16), wgate_ref[...]))
        proj = _dot(p_ref[...].astype(BF16), wproj_ref[...])
        o_ref[...] = h2 + _rms(proj * gate, gplepost_ref[...])


def _ffn_call(oa, ob, oc, h, p, lw, left, seq_len, tm, tf):
    n = h.shape[0]
    p_all, p_layer = p
    nseq = left.shape[0]
    nf = D_FF // tf
    if seq_len >= tm:
        nsb, seq_blocks = 1, seq_len // tm
    else:
        nsb, seq_blocks = tm // seq_len, 1
    assert (tm // nsb) & (tm // nsb - 1) == 0
    row = lambda i, f: (i, 0)
    full = lambda i, f: (0, 0)
    out, state = pl.pallas_call(
        functools.partial(_ffn_kernel, tm=tm, tf=tf, nsb=nsb, seq_blocks=seq_blocks),
        grid=(n // tm, nf),
        in_specs=[pl.BlockSpec((tm, FOX_WIDTH), row), pl.BlockSpec((tm, MLA_WIDTH), row),
                  pl.BlockSpec((tm, DIFF_WIDTH), row), pl.BlockSpec((tm, D_MODEL), row),
                  pl.BlockSpec((D_MODEL, D_MODEL), full), pl.BlockSpec((1, D_MODEL), full),
                  pl.BlockSpec((1, D_MODEL), full),
                  pl.BlockSpec((D_MODEL, tf), lambda i, f: (0, f)),
                  pl.BlockSpec((D_MODEL, tf), lambda i, f: (0, nf + f)),
                  pl.BlockSpec((CONV_WIDTH, tf), lambda i, f: (0, f)),
                  pl.BlockSpec((1, tf), lambda i, f: (0, f)),
                  pl.BlockSpec((tf, D_MODEL), lambda i, f: (f, 0)),
                  pl.BlockSpec((1, D_MODEL), full),
                  pl.BlockSpec((nsb, CONV_WIDTH - 1, tf), lambda i, f: (i // seq_blocks, 0, f)),
                  pl.BlockSpec((None, tm, PLE_DIM), lambda i, f: (p_layer, i, 0)), pl.BlockSpec((1, D_MODEL), full),
                  pl.BlockSpec((D_MODEL, D_MODEL), full), pl.BlockSpec((PLE_DIM, D_MODEL), full),
                  pl.BlockSpec((1, D_MODEL), full)],
        out_specs=[pl.BlockSpec((tm, D_MODEL), row),
                   pl.BlockSpec((nf, nseq, CONV_WIDTH - 1, tf), lambda i, f: (0, 0, 0, 0))],
        out_shape=[jax.ShapeDtypeStruct((n, D_MODEL), F32),
                   jax.ShapeDtypeStruct((nf, nseq, CONV_WIDTH - 1, tf), F32)],
        scratch_shapes=[pltpu.VMEM((tm, D_MODEL), F32), pltpu.VMEM((tm, D_MODEL), BF16),
                        pltpu.VMEM((tm, D_MODEL), F32), pltpu.VMEM((nf, 8, tf), F32)],
        compiler_params=_cparams(("arbitrary", "arbitrary")), name="ffn",
    )(oa, ob, oc, h, lw["w_out"], lw["g_mix_post"],
      lw["g_ffn_pre"], lw["w_up"], lw["w_up"], lw["conv_w"], lw["conv_b"], lw["w_down"], lw["g_ffn_post"], left,
      p_all, lw["g_ple_pre"], lw["w_ple_gate"], lw["w_ple_proj"], lw["g_ple_post"])
    return out, state.transpose(1, 2, 0, 3).reshape(nseq, CONV_WIDTH - 1, D_FF)


def _swap_halves(w):
    half = MLA_ROPE_DIM // 2
    return jnp.concatenate([w[..., half:], w[..., :half]], axis=-1)


def _pack_layer(w_in, b_forget, mla_q_norm, w_mla_uq, mla_kv_norm, w_mla_uk, w_mla_uv, lams, diff_subln, w_out,
                norm_mix_pre, norm_mix_post, norm_ffn_pre, norm_ffn_post, norm_ple_pre, norm_ple_post,
                w_ffn_up, ffn_conv_w, ffn_conv_b, w_ffn_down, w_ple_gate, w_ple_proj):
    zeros = lambda r, c: jnp.zeros((r, c), F32)
    w_kr = w_in[:, OFF_MLA_KR:OFF_DIFF_Q]
    w_all = jnp.concatenate([
        w_in[:, OFF_FOX_Q:OFF_FOX_F], w_in[:, OFF_DIFF_Q:IN_WIDTH],
        w_in[:, OFF_MLA_CQ:OFF_MLA_CKV], w_in[:, OFF_MLA_CKV:OFF_MLA_KR],
        w_kr, w_kr, zeros(D_MODEL, LANE - 2 * MLA_ROPE_DIM),
        _swap_halves(w_kr), _swap_halves(w_kr), zeros(D_MODEL, LANE - 2 * MLA_ROPE_DIM),
        w_in[:, OFF_FOX_F:OFF_MLA_CQ], zeros(D_MODEL, LANE - N_FOX_HEADS)], axis=1).astype(BF16)
    wq = w_mla_uq.reshape(MLA_Q_RANK, N_MLA_HEADS, MLA_NOPE_DIM + MLA_ROPE_DIM)
    plain, swapped = [], []
    pad = zeros(MLA_Q_RANK, MLA_PAIR_W - 2 * (MLA_NOPE_DIM + MLA_ROPE_DIM))
    for p in range(N_MLA_PAIRS):
        a, b = 2 * p, 2 * p + 1
        plain += [wq[:, a, :MLA_NOPE_DIM], wq[:, b, :MLA_NOPE_DIM], wq[:, a, MLA_NOPE_DIM:], wq[:, b, MLA_NOPE_DIM:], pad]
        swapped += [zeros(MLA_Q_RANK, 2 * MLA_NOPE_DIM), _swap_halves(wq[:, a, MLA_NOPE_DIM:]),
                    _swap_halves(wq[:, b, MLA_NOPE_DIM:]), pad]
    return dict(
        w_all=w_all,
        b_f=jnp.pad(b_forget, (0, LANE - N_FOX_HEADS)).reshape(1, LANE),
        g_q=mla_q_norm.reshape(1, -1), g_kv=mla_kv_norm.reshape(1, -1),
        w_uq2=jnp.concatenate(plain + swapped, axis=1).astype(BF16),
        w_ukv=jnp.concatenate([w_mla_uk, w_mla_uv], axis=1).astype(BF16),
        lams=lams, subln=jnp.tile(diff_subln, 2).reshape(1, LANE),
        w_out=w_out.astype(BF16),
        g_mix_pre=norm_mix_pre.reshape(1, -1), g_mix_post=norm_mix_post.reshape(1, -1),
        g_ffn_pre=norm_ffn_pre.reshape(1, -1), g_ffn_post=norm_ffn_post.reshape(1, -1),
        g_ple_pre=norm_ple_pre.reshape(1, -1), g_ple_post=norm_ple_post.reshape(1, -1),
        w_up=w_ffn_up.astype(BF16), conv_w=ffn_conv_w, conv_b=ffn_conv_b.reshape(1, -1),
        w_down=w_ffn_down.astype(BF16), w_ple_gate=w_ple_gate.astype(BF16), w_ple_proj=w_ple_proj.astype(BF16))


def _rope_tables(pos):
    half = MLA_ROPE_DIM // 2
    inv_freq = ROPE_THETA ** (-jnp.arange(half, dtype=F32) / half)
    ang = pos.astype(F32)[:, None] * inv_freq[None, :]
    cos, sin = jnp.cos(ang), jnp.sin(ang)
    t = pos.shape[0]
    pad = jnp.zeros((t, MLA_PAIR_W - LANE - 2 * MLA_ROPE_DIM), F32)
    cos_t = jnp.concatenate([jnp.ones((t, LANE), F32), cos, cos, cos, cos, pad], axis=1)
    sin_t = jnp.concatenate([jnp.zeros((t, LANE), F32), -sin, sin, -sin, sin, pad], axis=1)
    return cos_t, sin_t


def _alibi_slopes():
    s = 2.0 ** (-8.0 * np.arange(1, N_DIFF_HEADS + 1) / N_DIFF_HEADS)
    return jnp.asarray(np.broadcast_to(s.reshape(N_DIFF_HEADS // 2, 2, 1), (N_DIFF_HEADS // 2, 2, LANE)), dtype=F32)


def _tile_rows(n, pref):
    t = min(n, pref)
    assert n % t == 0
    return t


def _layer(h, p, lw, cos, sin, cache, conv_left, lam_init, *, batch, seq, past, tq):
    n = h.shape[0]
    tm = _tile_rows(n, 512)
    (fox_k, fox_v, logf, ckv, krope, diff_k, diff_v,
     fox_bf, diff_bf, mla_q, mla_k, mla_v) = _proj_call(h, lw, cos, sin, tm)
    rows = (fox_k, fox_v, logf, ckv, krope, diff_k, diff_v)
    b3 = lambda a: a.reshape(batch, seq, a.shape[-1])
    diff_extras = [_alibi_slopes(), lw["lams"], lw["subln"]]
    diff_specs = [pl.BlockSpec((1, 2, LANE), lambda bb, hp, i: (hp, 0, 0)),
                  pl.BlockSpec((4, DIFF_QK_DIM), lambda bb, hp, i: (0, 0)),
                  pl.BlockSpec((1, LANE), lambda bb, hp, i: (0, 0))]
    fb, db = b3(fox_bf), b3(diff_bf)
    if cache is None:
        common = dict(tq=tq, tkd=tq, qoff=0, blocked=True)
        o_a = _attn_call(_fox_kernel, "fox_attn", fb, 0, fb, 3, _forget_bias_call(b3(logf)),
                         _values_t(fb[..., 2 * FOX_WIDTH:], tq), [], [],
                         npairs=N_FOX_HEADS // 2, qw=LANE, chains=2, **common)
        o_b = _attn_call(_mla_kernel, "mla_attn", b3(mla_q), 0, b3(mla_k), 0, None, _values_t(b3(mla_v), tq), [], [],
                         npairs=N_MLA_PAIRS, qw=MLA_PAIR_W, chains=2, **common)
        o_c = _attn_call(_diff_kernel, "diff_attn", db, 0, db, 2, _alibi_bias_block(seq),
                         _values_t(db[..., 2 * DIFF_WIDTH:], tq), diff_extras, diff_specs,
                         npairs=N_DIFF_HEADS // 2, qw=LANE, chains=4, lam_init=lam_init, **common)
    else:
        c_fox_k, c_fox_v, c_logf, c_ckv, c_krope, c_diff_k, c_diff_v = cache
        tk = -(-(past + seq) // LANE) * LANE
        padk = lambda a: jnp.pad(a, ((0, 0), (0, tk - past - seq), (0, 0)))
        cat = lambda old, new: padk(jnp.concatenate([old.astype(new.dtype), new], axis=1))
        flat = lambda a: a.reshape(batch, past, -1)
        padq = lambda a: jnp.pad(a, ((0, 0), (0, tq - seq), (0, 0)))
        kf = cat(flat(c_fox_k), fb[..., FOX_WIDTH:2 * FOX_WIDTH])
        vf = cat(flat(c_fox_v), fb[..., 2 * FOX_WIDTH:])
        kd = cat(flat(c_diff_k), db[..., DIFF_WIDTH:2 * DIFF_WIDTH])
        vd = cat(flat(c_diff_v), db[..., 2 * DIFF_WIDTH:])
        kn_c, v_c = _kvup_call(c_ckv.reshape(batch * past, MLA_KV_RANK), lw["w_ukv"], _tile_rows(batch * past, 1024))
        kr_c = c_krope.astype(BF16)
        krp_c = jnp.concatenate([kr_c, kr_c, jnp.zeros((batch, past, LANE - 2 * MLA_ROPE_DIM), BF16)], axis=-1)
        km_c = jnp.concatenate([kn_c.reshape(batch, past, N_MLA_PAIRS, LANE),
                                jnp.broadcast_to(krp_c[:, :, None, :], (batch, past, N_MLA_PAIRS, LANE))],
                               axis=-1).reshape(batch, past, MLA_QK_W)
        km = cat(km_c, b3(mla_k))
        vm = cat(v_c.reshape(batch, past, MLA_WIDTH), b3(mla_v))
        common = dict(tq=tq, tkd=tk, qoff=past, blocked=False)
        o_a = _attn_call(_fox_kernel, "fox_attn_s", padq(fb[..., :FOX_WIDTH]), 0, kf, 0,
                         _forget_bias_call(cat(c_logf, b3(logf))), _values_t(vf, tk), [], [],
                         npairs=N_FOX_HEADS // 2, qw=LANE, chains=2, **common)
        o_b = _attn_call(_mla_kernel, "mla_attn_s", padq(b3(mla_q)), 0, km, 0, None, _values_t(vm, tk), [], [],
                         npairs=N_MLA_PAIRS, qw=MLA_PAIR_W, chains=2, **common)
        o_c = _attn_call(_diff_kernel, "diff_attn_s", padq(db[..., :DIFF_WIDTH]), 0, kd, 0, _alibi_bias_block(tk),
                         _values_t(vd, tk), diff_extras, diff_specs,
                         npairs=N_DIFF_HEADS // 2, qw=LANE, chains=4, lam_init=lam_init, **common)
        o_a, o_b, o_c = o_a[:, :seq], o_b[:, :seq], o_c[:, :seq]
    f2 = lambda a: a.reshape(n, a.shape[-1])
    h, conv_state = _ffn_call(f2(o_a), f2(o_b), f2(o_c), h, p, lw, conv_left, seq, tm, 2048)
    return h, rows, conv_state


def kernel(x_prompt, x_sample, cache_fox_k, cache_fox_v, cache_fox_logf, cache_mla_ckv, cache_mla_krope, cache_diff_k, cache_diff_v, state_ffn_conv, p_prompt, p_sample, w_in, b_forget, mla_q_norm, w_mla_uq, mla_kv_norm, w_mla_uk, w_mla_uv, diff_lambda_q1, diff_lambda_k1, diff_lambda_q2, diff_lambda_k2, diff_subln, w_out, norm_mix_pre, norm_mix_post, norm_ffn_pre, norm_ffn_post, norm_ple_pre, norm_ple_post, w_ffn_up, ffn_conv_w, ffn_conv_b, w_ffn_down, w_ple_gate, w_ple_proj):
    bp, sp, _ = x_prompt.shape
    bs, ts, _ = x_sample.shape
    depth, _, past = cache_fox_k.shape[:3]
    assert past % CHUNK == 0 and ts <= CHUNK and sp % LANE == 0
    tq = _tile_rows(sp, 512)
    tms = _tile_rows(bs * ts, 512)

    cos_p, sin_p = _rope_tables(jnp.arange(sp))
    cos_s, sin_s = _rope_tables(past + jnp.arange(ts))
    cos_s, sin_s = jnp.tile(cos_s, (tms // ts, 1)), jnp.tile(sin_s, (tms // ts, 1))

    hp = x_prompt.reshape(bp * sp, D_MODEL)
    hs = x_sample.reshape(bs * ts, D_MODEL)
    rows_p, rows_s, conv_p, conv_s = [], [], [], []
    for l in range(depth):
        lams = jnp.stack([diff_lambda_q1[l], diff_lambda_k1[l], diff_lambda_q2[l], diff_lambda_k2[l]], axis=0)
        lw = _pack_layer(w_in[l], b_forget[l], mla_q_norm[l], w_mla_uq[l], mla_kv_norm[l], w_mla_uk[l], w_mla_uv[l],
                         lams, diff_subln[l], w_out[l], norm_mix_pre[l], norm_mix_post[l], norm_ffn_pre[l],
                         norm_ffn_post[l], norm_ple_pre[l], norm_ple_post[l], w_ffn_up[l], ffn_conv_w[l],
                         ffn_conv_b[l], w_ffn_down[l], w_ple_gate[l], w_ple_proj[l])
        lam_init = 0.8 - 0.6 * math.exp(-0.3 * l)
        hp, rp, cp = _layer(hp, (p_prompt.reshape(depth, bp * sp, PLE_DIM), l), lw, cos_p, sin_p, None,
                            jnp.zeros((bp, CONV_WIDTH - 1, D_FF), F32), lam_init,
                            batch=bp, seq=sp, past=0, tq=tq)
        cache_l = (cache_fox_k[l], cache_fox_v[l], cache_fox_logf[l], cache_mla_ckv[l], cache_mla_krope[l],
                   cache_diff_k[l], cache_diff_v[l])
        hs, rs, cs = _layer(hs, (p_sample.reshape(depth, bs * ts, PLE_DIM), l), lw, cos_s, sin_s, cache_l,
                            state_ffn_conv[l], lam_init, batch=bs, seq=ts, past=past, tq=LANE)
        rows_p.append(rp)
        rows_s.append(rs)
        conv_p.append(cp)
        conv_s.append(cs)

    def stack_rows(per_layer, batch, seq):
        fk, fv, lf, ckv, kr, dk, dv = (jnp.stack(a, axis=0) for a in zip(*per_layer))
        lead = (depth, batch, seq)
        return (fk.reshape(lead + (N_FOX_HEADS, HEAD_DIM)), fv.reshape(lead + (N_FOX_HEADS, HEAD_DIM)),
                lf.reshape(lead + (N_FOX_HEADS,)), ckv.reshape(lead + (MLA_KV_RANK,)),
                kr.reshape(lead + (MLA_ROPE_DIM,)), dk.reshape(lead + (N_DIFF_HEADS, 2 * DIFF_QK_DIM)),
                dv.reshape(lead + (N_DIFF_HEADS, DIFF_V_DIM)))

    out_p = stack_rows(rows_p, bp, sp)
    out_s = stack_rows(rows_s, bs, ts)
    return ((hp.reshape(bp, sp, D_MODEL), hs.reshape(bs, ts, D_MODEL)) + out_p + (jnp.stack(conv_p, axis=0),)
            + out_s + (jnp.stack(conv_s, axis=0),))
```

```python
import functools
import math

import jax
import jax.numpy as jnp
import numpy as np
from jax import lax
from jax.experimental import pallas as pl
from jax.experimental.pallas import tpu as pltpu

F32 = jnp.float32
BF16 = jnp.bfloat16

D_MODEL = 1024
HEAD_DIM = 64
N_FOX_HEADS = 6
N_MLA_HEADS = 6
N_DIFF_HEADS = 4
MLA_Q_RANK = 384
MLA_KV_RANK = 256
MLA_NOPE_DIM = 64
MLA_ROPE_DIM = 32
MLA_V_DIM = 64
DIFF_QK_DIM = 32
DIFF_V_DIM = 64
FOX_WIDTH = N_FOX_HEADS * HEAD_DIM
MLA_WIDTH = N_MLA_HEADS * MLA_V_DIM
DIFF_WIDTH = N_DIFF_HEADS * DIFF_V_DIM
D_FF = 4 * D_MODEL
CONV_WIDTH = 3
PLE_DIM = 256
CHUNK = 64
ROPE_THETA = 10000.0
RMS_EPS = 1e-6
NEG_INF = -1e30
LOG2E = math.log2(math.e)

OFF_FOX_Q = 0
OFF_FOX_F = 3 * FOX_WIDTH
OFF_MLA_CQ = OFF_FOX_F + N_FOX_HEADS
OFF_MLA_CKV = OFF_MLA_CQ + MLA_Q_RANK
OFF_MLA_KR = OFF_MLA_CKV + MLA_KV_RANK
OFF_DIFF_Q = OFF_MLA_KR + MLA_ROPE_DIM
IN_WIDTH = OFF_DIFF_Q + 2 * N_DIFF_HEADS * 2 * DIFF_QK_DIM + DIFF_WIDTH

LANE = 128
V7X_VMEM_BYTES = 64 * 1024 * 1024
VMEM_LIMIT = (V7X_VMEM_BYTES * 7) // 8

C_FOX = 0
C_DIFF = C_FOX + 3 * FOX_WIDTH
C_CQ = C_DIFF + 3 * DIFF_WIDTH
C_CKV = C_CQ + MLA_Q_RANK
C_KR = C_CKV + MLA_KV_RANK
C_F = C_KR + 2 * LANE
W_ALL = C_F + LANE
N_MLA_PAIRS = N_MLA_HEADS // 2
MLA_PAIR_W = 2 * LANE
MLA_QK_W = N_MLA_PAIRS * MLA_PAIR_W

FOX_QSCALE = HEAD_DIM ** -0.5 * LOG2E
MLA_QSCALE = (MLA_NOPE_DIM + MLA_ROPE_DIM) ** -0.5 * LOG2E
DIFF_QSCALE = DIFF_QK_DIM ** -0.5 * LOG2E


def _cparams(sem):
    return pltpu.CompilerParams(dimension_semantics=sem, vmem_limit_bytes=VMEM_LIMIT)


def _rms(x, g):
    return x * lax.rsqrt(jnp.mean(x * x, axis=-1, keepdims=True) + RMS_EPS) * g


def _dot(a, b):
    return jnp.dot(a, b, preferred_element_type=F32)


def _dot_nt(a, b):
    return lax.dot_general(a, b, (((1,), (1,)), ((), ())), preferred_element_type=F32)


def _store_values_t(vt_ref, v):
    dv = LANE // 2
    ones = jnp.ones((ONES_ROWS, v.shape[0]), BF16)
    for p in range(v.shape[1] // LANE):
        t = v[:, p * LANE:(p + 1) * LANE].T.astype(BF16)
        for e in range(2):
            r0 = e * (dv + ONES_ROWS)
            vt_ref[0, p, 0, r0:r0 + dv, :] = t[e * dv:(e + 1) * dv, :]
            vt_ref[0, p, 0, r0 + dv:r0 + dv + ONES_ROWS, :] = ones


def _proj_kernel(h_ref, gpre_ref, w_ref, bf_ref, gq_ref, gkv_ref, wuq_ref, wukv_ref, cos_ref, sin_ref, *rest,
                 n_alias, emit_vt):
    (foxk_ref, foxv_ref, logf_ref, ckv_ref, krope_ref, diffk_ref, diffv_ref,
     foxbf_ref, diffbf_ref, mlaq_ref, mlak_ref, mlav_ref) = rest[n_alias:n_alias + 12]
    xn = _rms(h_ref[...], gpre_ref[...]).astype(BF16)
    cos = cos_ref[...]
    sin = sin_ref[...]

    z = _dot(xn, w_ref[:, C_FOX:C_FOX + 3 * FOX_WIDTH])
    foxbf_ref[:, :FOX_WIDTH] = (z[:, :FOX_WIDTH] * FOX_QSCALE).astype(BF16)
    foxbf_ref[:, FOX_WIDTH:] = z[:, FOX_WIDTH:].astype(BF16)
    foxk_ref[...] = z[:, FOX_WIDTH:2 * FOX_WIDTH]
    foxv_ref[...] = z[:, 2 * FOX_WIDTH:3 * FOX_WIDTH]
    if emit_vt:
        _store_values_t(rest[n_alias + 12], z[:, 2 * FOX_WIDTH:3 * FOX_WIDTH])

    z = _dot(xn, w_ref[:, C_DIFF:C_DIFF + 3 * DIFF_WIDTH])
    diffbf_ref[:, :DIFF_WIDTH] = (z[:, :DIFF_WIDTH] * DIFF_QSCALE).astype(BF16)
    diffbf_ref[:, DIFF_WIDTH:] = z[:, DIFF_WIDTH:].astype(BF16)
    diffk_ref[...] = z[:, DIFF_WIDTH:2 * DIFF_WIDTH]
    diffv_ref[...] = z[:, 2 * DIFF_WIDTH:3 * DIFF_WIDTH]
    if emit_vt:
        _store_values_t(rest[n_alias + 14], z[:, 2 * DIFF_WIDTH:3 * DIFF_WIDTH])

    cq = _rms(_dot(xn, w_ref[:, C_CQ:C_CQ + MLA_Q_RANK]), gq_ref[...]).astype(BF16)
    q2 = _dot(cq, wuq_ref[...])
    for p in range(N_MLA_PAIRS):
        lo = p * MLA_PAIR_W
        roped = q2[:, lo:lo + MLA_PAIR_W] * cos + q2[:, MLA_QK_W + lo:MLA_QK_W + lo + MLA_PAIR_W] * sin
        mlaq_ref[:, lo:lo + MLA_PAIR_W] = (roped * MLA_QSCALE).astype(BF16)

    ckv = _rms(_dot(xn, w_ref[:, C_CKV:C_CKV + MLA_KV_RANK]), gkv_ref[...])
    ckv_ref[...] = ckv
    kv = _dot(ckv.astype(BF16), wukv_ref[...])
    mlav_ref[...] = kv[:, MLA_WIDTH:2 * MLA_WIDTH].astype(BF16)
    if emit_vt:
        _store_values_t(rest[n_alias + 13], kv[:, MLA_WIDTH:2 * MLA_WIDTH])
    zkr =_dot(xn, w_ref[:, C_KR:C_KR + 2 * LANE])
    krp = zkr[:, :LANE] * cos[:, LANE:] + zkr[:, LANE:] * sin[:, LANE:]
    krope_ref[...] = krp[:, :MLA_ROPE_DIM]
    krp16 = krp.astype(BF16)
    for p in range(N_MLA_PAIRS):
        lo = p * MLA_PAIR_W
        mlak_ref[:, lo:lo + LANE] = kv[:, p * LANE:(p + 1) * LANE].astype(BF16)
        mlak_ref[:, lo + LANE:lo + 2 * LANE] = krp16

    zf = _dot(xn, w_ref[:, C_F:C_F + LANE]) + bf_ref[...]
    logf = -(jnp.maximum(-zf, 0.0) + jnp.log1p(jnp.exp(-jnp.abs(zf))))
    logf_ref[...] = logf[:, :N_FOX_HEADS]


N_ROW_OUTPUTS = 7


def _proj_call(h, lw, cos, sin, tm, layer, depth, prev_rows, vt_seq):
    n = h.shape[0]
    ntab = cos.shape[0] // tm
    row = lambda i: (i, 0)
    full = lambda i: (0, 0)
    tab = lambda i: (i % ntab, 0)
    in_specs = [
        pl.BlockSpec((tm, D_MODEL), row),
        pl.BlockSpec((1, D_MODEL), full),
        pl.BlockSpec((D_MODEL, W_ALL), full),
        pl.BlockSpec((1, LANE), full),
        pl.BlockSpec((1, MLA_Q_RANK), full),
        pl.BlockSpec((1, MLA_KV_RANK), full),
        pl.BlockSpec((MLA_Q_RANK, 2 * MLA_QK_W), full),
        pl.BlockSpec((MLA_KV_RANK, 2 * MLA_WIDTH), full),
        pl.BlockSpec((tm, MLA_PAIR_W), tab),
        pl.BlockSpec((tm, MLA_PAIR_W), tab),
    ]
    widths = [(FOX_WIDTH, F32), (FOX_WIDTH, F32), (N_FOX_HEADS, F32), (MLA_KV_RANK, F32), (MLA_ROPE_DIM, F32),
              (DIFF_WIDTH, F32), (DIFF_WIDTH, F32),
              (3 * FOX_WIDTH, BF16), (3 * DIFF_WIDTH, BF16), (MLA_QK_W, BF16), (MLA_QK_W, BF16), (MLA_WIDTH, BF16)]
    out_shape = ([jax.ShapeDtypeStruct((depth, n, w), dt) for w, dt in widths[:N_ROW_OUTPUTS]]
                 + [jax.ShapeDtypeStruct((n, w), dt) for w, dt in widths[N_ROW_OUTPUTS:]])
    out_specs = ([pl.BlockSpec((None, tm, w), lambda i: (layer, i, 0)) for w, _ in widths[:N_ROW_OUTPUTS]]
                 + [pl.BlockSpec((tm, w), row) for w, _ in widths[N_ROW_OUTPUTS:]])
    if vt_seq is not None:
        nblk = vt_seq // tm
        for width in (FOX_WIDTH, MLA_WIDTH, DIFF_WIDTH):
            shape = (n // vt_seq, width // LANE, nblk, 2 * (LANE // 2 + ONES_ROWS), tm)
            out_shape.append(jax.ShapeDtypeStruct(shape, BF16))
            out_specs.append(pl.BlockSpec((1,) + shape[1:2] + (1,) + shape[3:],
                                          lambda i: (i // nblk, 0, i % nblk, 0, 0)))
    args = [h, lw["g_mix_pre"], lw["w_all"], lw["b_f"], lw["g_q"], lw["g_kv"], lw["w_uq2"], lw["w_ukv"], cos, sin]
    aliases = {}
    if prev_rows is not None:
        aliases = {len(args) + k: k for k in range(N_ROW_OUTPUTS)}
        in_specs = in_specs + [pl.BlockSpec(memory_space=pl.ANY)] * N_ROW_OUTPUTS
        args = args + list(prev_rows)
    return pl.pallas_call(
        functools.partial(_proj_kernel, n_alias=len(aliases), emit_vt=vt_seq is not None), grid=(n // tm,),
        in_specs=in_specs,
        out_specs=out_specs, out_shape=out_shape, input_output_aliases=aliases,
        compiler_params=_cparams(("parallel",)), name="proj",
    )(*args)


def _kvup_kernel(ckv_ref, wukv_ref, kn_ref, v_ref):
    kv = _dot(ckv_ref[...].astype(BF16), wukv_ref[...])
    kn_ref[...] = kv[:, :MLA_WIDTH].astype(BF16)
    v_ref[...] = kv[:, MLA_WIDTH:].astype(BF16)


def _kvup_call(ckv, w_ukv, tm):
    n = ckv.shape[0]
    row = lambda i: (i, 0)
    return pl.pallas_call(
        _kvup_kernel, grid=(n // tm,),
        in_specs=[pl.BlockSpec((tm, MLA_KV_RANK), row), pl.BlockSpec((MLA_KV_RANK, 2 * MLA_WIDTH), lambda i: (0, 0))],
        out_specs=[pl.BlockSpec((tm, MLA_WIDTH), row), pl.BlockSpec((tm, MLA_WIDTH), row)],
        out_shape=[jax.ShapeDtypeStruct((n, MLA_WIDTH), BF16)] * 2,
        compiler_params=_cparams(("parallel",)), name="kvup",
    )(ckv, w_ukv)


AUG_STRIDE = 8


def _keep_bf16_bits(x):
    bits = lax.bitcast_convert_type(x, jnp.uint32) & jnp.uint32(0xFFFF0000)
    return lax.bitcast_convert_type(bits, F32)


def _split3(x):
    x1 = _keep_bf16_bits(x)
    r = x - x1
    x2 = _keep_bf16_bits(r)
    return x1, x2, r - x2


def _place3(x1, x2, x3):
    lane = lax.broadcasted_iota(jnp.int32, (1, LANE), 1)
    return jnp.where(lane < AUG_STRIDE, x1,
                     jnp.where(lane < 2 * AUG_STRIDE, pltpu.roll(x2, AUG_STRIDE, 1),
                               pltpu.roll(x3, 2 * AUG_STRIDE, 1))).astype(BF16)


def _aug_selector(h, rows):
    lane = lax.broadcasted_iota(jnp.int32, (rows, LANE), 1)
    hit = (lane == h) | (lane == AUG_STRIDE + h) | (lane == 2 * AUG_STRIDE + h)
    return jnp.where(hit, 1.0, 0.0).astype(BF16)


def _forget_bias_kernel(x_ref, tri_ref, o_ref, carry_sc, *, tb):
    @pl.when(pl.program_id(1) == 0)
    def _():
        carry_sc[...] = jnp.zeros_like(carry_sc)

    terms = jnp.concatenate(_split3(x_ref[0]), axis=1).astype(BF16)
    sums = _dot(tri_ref[...], terms)
    cs = carry_sc[0:1, :] + sums[:, :LANE] + sums[:, LANE:2 * LANE] + sums[:, 2 * LANE:]
    carry_sc[...] = jnp.broadcast_to(cs[tb - 1:tb, :], carry_sc.shape)
    o_ref[0] = _place3(*_split3(cs * (-LOG2E)))


def _forget_bias_call(logf):
    b, t, hh = logf.shape
    tb = max(d for d in range(LANE, 4 * LANE + 1, LANE) if t % d == 0)
    x = jnp.pad(logf, ((0, 0), (0, 0), (0, LANE - hh)))
    tri = jnp.tril(jnp.ones((tb, tb), BF16))
    blk = pl.BlockSpec((1, tb, LANE), lambda i, j: (i, j, 0))
    return pl.pallas_call(
        functools.partial(_forget_bias_kernel, tb=tb), grid=(b, t // tb),
        in_specs=[blk, pl.BlockSpec((tb, tb), lambda i, j: (0, 0))], out_specs=blk,
        out_shape=jax.ShapeDtypeStruct((b, t, LANE), BF16),
        scratch_shapes=[pltpu.VMEM((8, LANE), F32)],
        compiler_params=_cparams(("parallel", "arbitrary")), name="forget_bias",
    )(x, tri)


def _alibi_bias_block(t_k):
    slopes = 2.0 ** (-8.0 * np.arange(1, N_DIFF_HEADS + 1) / N_DIFF_HEADS)
    b = jnp.zeros((t_k, LANE), F32).at[:, :N_DIFF_HEADS].set(
        jnp.arange(t_k, dtype=F32)[:, None] * jnp.asarray(slopes * LOG2E, F32)[None, :])
    x1, x2, x3 = _split3(b)
    lane = jnp.arange(LANE)[None, :]
    placed = jnp.where(lane < AUG_STRIDE, x1, jnp.where(lane < 2 * AUG_STRIDE, jnp.roll(x2, AUG_STRIDE, 1),
                                                        jnp.roll(x3, 2 * AUG_STRIDE, 1)))
    return placed.astype(BF16)[None]


ROW_CHUNK = CHUNK
STRIP = 4 * LANE
ONES_ROWS = 16


def _masked_chunks(s_ref, c, mask, tk, tq):
    chunks = [(r0, "all" if mask is None else mask[0](r0, 0, tq)) for r0 in range(0, tk, ROW_CHUNK)]

    def logits(r0, vis):
        x = s_ref[c, r0:r0 + ROW_CHUNK, :]
        return x if vis == "all" else mask[1](r0, 0, x)
    return chunks, logits


def _softmax_max(s_ref, m_ref, c, mask, tk, tq):
    chunks, logits = _masked_chunks(s_ref, c, mask, tk, tq)
    mrun = jnp.full((8, tq), NEG_INF, F32)
    for r0, vis in chunks:
        if vis != "none":
            mrun = jnp.maximum(mrun, jnp.max(logits(r0, vis).reshape(ROW_CHUNK // 8, 8, tq), axis=0))
    m_old = m_ref[c]
    return m_old, jnp.maximum(m_old, jnp.max(mrun, axis=0, keepdims=True))


def _softmax_exp(s_ref, p_ref, m_ref, alpha_ref, c, mask, tk, tq, m_old, m_new):
    chunks, logits = _masked_chunks(s_ref, c, mask, tk, tq)
    for r0, vis in chunks:
        if vis == "none":
            p_ref[c, r0:r0 + ROW_CHUNK, :] = jnp.zeros((ROW_CHUNK, tq), BF16)
        else:
            p_ref[c, r0:r0 + ROW_CHUNK, :] = jnp.exp2(logits(r0, vis) - m_new).astype(BF16)
    m_ref[c] = m_new
    alpha_ref[c] = jnp.exp2(m_old - m_new)


def _apply_values(p_ref, alpha_ref, acc_ref, c, vt):
    acc_ref[c] = alpha_ref[c] * acc_ref[c] + _dot(vt, p_ref[c])


def _causal_mask(qoff):
    def visibility(r0, c0, sw):
        if r0 + ROW_CHUNK - 1 <= c0 + qoff:
            return "all"
        return "none" if r0 > c0 + sw - 1 + qoff else "some"

    def apply(r0, c0, x):
        kk, qq = _key_query_iota(*x.shape)
        return jnp.where(kk - qq <= c0 + qoff - r0, x, NEG_INF)
    return visibility, apply


def _chunk_visibility(r0, c0, sw, qoff):
    if r0 // CHUNK <= (c0 + qoff) // CHUNK:
        return "all"
    return "none" if r0 // CHUNK > (c0 + sw - 1 + qoff) // CHUNK else "some"


def _chunk_mask(r0, c0, sw, qoff):
    qq = lax.broadcasted_iota(jnp.int32, (1, sw), 1)
    return r0 // CHUNK <= (qq + (c0 + qoff)) // CHUNK


def _init_states(m_ref, acc_ref):
    m_ref[...] = jnp.full(m_ref.shape, NEG_INF, F32)
    acc_ref[...] = jnp.zeros(acc_ref.shape, F32)


def _normalized(acc_ref, c, dv):
    return acc_ref[c, :dv, :] / acc_ref[c, dv:dv + 1, :]


def _key_query_iota(tk, tq):
    return (lax.broadcasted_iota(jnp.int32, (tk, tq), 0), lax.broadcasted_iota(jnp.int32, (tk, tq), 1))


def _store_pair(o_ref, o0, o1, row_scale=None):
    out = jnp.concatenate([o0, o1], axis=0).T
    if row_scale is not None:
        out = out * row_scale
    o_ref[0] = out.astype(BF16)


def _block_start(j, size):
    return j * size if isinstance(j, int) else pl.multiple_of(j * size, size)


N_BUF = 3


def _attend(i, blocked, chains, qk_fn, max_fn, exp_fn, pv_fn):
    def step(qk, pv, buf, diag):
        if qk is not None:
            all_chains(qk_fn, *qk)
        if pv is not None:
            all_chains(pv_fn, *pv)
        for c in range(chains):
            exp_fn(buf, c, diag, *max_fn(buf, c, diag))

    def all_chains(fn, *args):
        for c in range(chains):
            fn(*args, c)

    if not blocked:
        all_chains(qk_fn, 0, 0)
        step(None, None, 0, True)
        all_chains(pv_fn, 0, 0)
        return
    last = N_BUF - 1
    lead = i % N_BUF

    def single(j, carry):
        all_chains(qk_fn, last, j)
        step(None, None, last, False)
        all_chains(pv_fn, last, j)
        return carry

    lax.fori_loop(0, lead, single, 0)
    all_chains(qk_fn, 0, lead)
    all_chains(qk_fn, 1, jnp.minimum(lead + 1, i))

    def rotate(t, carry):
        b0 = lead + N_BUF * t
        for u in range(N_BUF):
            step(((u + 2) % N_BUF, jnp.minimum(b0 + u + 2, i)), (u - 1, b0 + u - 1) if u > 0 else None, u, False)
        all_chains(pv_fn, last, b0 + last)
        return carry

    lax.fori_loop(0, (i - lead) // N_BUF, rotate, 0)
    step(None, None, 0, True)
    all_chains(pv_fn, 0, i)


def _vt_rows(vt_ref, j, e, dv):
    return vt_ref[0, 0, j, e * (dv + ONES_ROWS):(e + 1) * (dv + ONES_ROWS), :]


def _split_scratch(scratch):
    nbuf = (len(scratch) - 3) // 2
    return (scratch[:nbuf], scratch[nbuf:2 * nbuf]) + tuple(scratch[2 * nbuf:])


def _chain_fns(scratch, vt_ref, mask_of, dv, head_of, tkd, tq):
    s_sc, p_sc, m_sc, alpha_sc, acc_sc = _split_scratch(scratch)

    def max_fn(buf, c, diag):
        return _softmax_max(s_sc[buf], m_sc, c, mask_of(c) if diag else None, tkd, tq)

    def exp_fn(buf, c, diag, m_old, m_new):
        _softmax_exp(s_sc[buf], p_sc[buf], m_sc, alpha_sc, c, mask_of(c) if diag else None, tkd, tq, m_old, m_new)

    def pv_fn(buf, j, c):
        _apply_values(p_sc[buf], alpha_sc, acc_sc, c, _vt_rows(vt_ref, j, head_of(c), dv))
    return max_fn, exp_fn, pv_fn


def _fox_kernel(q_ref, k_ref, aug_ref, vt_ref, o_ref, *scratch, tq, tkd, qoff, blocked):
    s_sc, p_sc, m_sc, alpha_sc, acc_sc = _split_scratch(scratch)
    hp, i = pl.program_id(1), pl.program_id(2)
    _init_states(m_sc, acc_sc)
    q = q_ref[0]
    lane = lax.broadcasted_iota(jnp.int32, (1, LANE), 1)
    qs = [jnp.concatenate([jnp.where((lane >= e * HEAD_DIM) & (lane < (e + 1) * HEAD_DIM), q, jnp.zeros_like(q)),
                           _aug_selector(2 * hp + e, tq)], axis=1) for e in range(2)]

    def qk_fn(buf, j, e):
        rows = pl.ds(_block_start(j, tq), tkd)
        k = jnp.concatenate([k_ref[0, rows, :], aug_ref[0, rows, :]], axis=1)
        s_sc[buf][e] = _dot_nt(k, qs[e])

    _attend(i, blocked, 2, qk_fn, *_chain_fns(scratch, vt_ref, lambda e: _causal_mask(qoff), HEAD_DIM,
                                              lambda e: e, tkd, tq))
    _store_pair(o_ref, *(_normalized(acc_sc, e, HEAD_DIM) for e in range(2)))


def _mla_kernel(q_ref, k_ref, vt_ref, o_ref, *scratch, tq, tkd, qoff, blocked):
    s_sc, p_sc, m_sc, alpha_sc, acc_sc = _split_scratch(scratch)
    i = pl.program_id(2)
    _init_states(m_sc, acc_sc)
    q = q_ref[0]
    lane2 = lax.broadcasted_iota(jnp.int32, (1, MLA_PAIR_W), 1)
    sel = []
    for e in range(2):
        nope = (lane2 >= e * MLA_NOPE_DIM) & (lane2 < (e + 1) * MLA_NOPE_DIM)
        rope = (lane2 >= LANE + e * MLA_ROPE_DIM) & (lane2 < LANE + (e + 1) * MLA_ROPE_DIM)
        sel.append(jnp.where(nope | rope, q, jnp.zeros_like(q)))

    def qk_fn(buf, j, e):
        s_sc[buf][e] = _dot_nt(k_ref[0, pl.ds(_block_start(j, tq), tkd), :], sel[e])

    mask = (lambda r0, c0, sw: _chunk_visibility(r0, c0, sw, qoff),
            lambda r0, c0, x: jnp.where(_chunk_mask(r0, c0, x.shape[1], qoff), x, NEG_INF))
    _attend(i, blocked, 2, qk_fn, *_chain_fns(scratch, vt_ref, lambda e: mask, MLA_V_DIM, lambda e: e, tkd, tq))
    _store_pair(o_ref, *(_normalized(acc_sc, e, MLA_V_DIM) for e in range(2)))


def _diff_kernel(q_ref, k_ref, aug_ref, vt_ref, slope_ref, lam_ref, subln_ref, o_ref, *scratch,
                 tq, tkd, qoff, blocked, lam_init):
    s_sc, p_sc, m_sc, alpha_sc, acc_sc = _split_scratch(scratch)
    hp, i = pl.program_id(1), pl.program_id(2)
    _init_states(m_sc, acc_sc)
    q = q_ref[0]
    lane = lax.broadcasted_iota(jnp.int32, (1, LANE), 1)
    sel = [jnp.concatenate([jnp.where((lane >= (2 * e + t) * DIFF_QK_DIM) & (lane < (2 * e + t + 1) * DIFF_QK_DIM),
                                      q, jnp.zeros_like(q)), _aug_selector(2 * hp + e, tq)], axis=1)
           for e in range(2) for t in range(2)]
    slopes = [slope_ref[0, e:e + 1, 0:1] * LOG2E for e in range(2)]

    def qk_fn(buf, j, c):
        rows = pl.ds(_block_start(j, tq), tkd)
        k = jnp.concatenate([k_ref[0, rows, :], aug_ref[0, rows, :]], axis=1)
        s_sc[buf][c] = _dot_nt(k, sel[c])

    def mask_of(c):
        def visibility(r0, c0, sw):
            vis = _chunk_visibility(r0, c0, sw, qoff)
            return "some" if vis == "all" and r0 + ROW_CHUNK - 1 > c0 + qoff else vis

        def apply(r0, c0, x):
            kk, qq = _key_query_iota(*x.shape)
            ahead = jnp.maximum(kk - qq + (r0 - c0 - qoff), 0).astype(F32)
            return jnp.where(_chunk_mask(r0, c0, x.shape[1], qoff), x - (2.0 * slopes[c // 2]) * ahead, NEG_INF)
        return visibility, apply

    _attend(i, blocked, 4, qk_fn, *_chain_fns(scratch, vt_ref, mask_of, DIFF_V_DIM, lambda c: c // 2, tkd, tq))
    res = [_normalized(acc_sc, c, DIFF_V_DIM) for c in range(4)]
    lq = lam_ref[...]
    lam = (jnp.exp(jnp.sum(lq[0:1] * lq[1:2], axis=-1, keepdims=True))
           - jnp.exp(jnp.sum(lq[2:3] * lq[3:4], axis=-1, keepdims=True)) + lam_init)
    outs = []
    for e in range(2):
        o = res[2 * e] - lam * res[2 * e + 1]
        outs.append(o * lax.rsqrt(jnp.mean(o * o, axis=0, keepdims=True) + RMS_EPS))
    _store_pair(o_ref, outs[0], outs[1], subln_ref[...] * (1.0 - lam_init))


def _attn_call(kernel, name, q, q_cb, k, k_cb, aug, vt, extras, extra_specs, *, npairs, qw, tq, tkd, qoff, blocked,
               chains, **kw):
    b, t_q = q.shape[0], q.shape[1]
    t_k = k.shape[1]
    acc_rows = LANE // 2 + ONES_ROWS
    nbuf = N_BUF if blocked else 1
    scratch = ([pltpu.VMEM((chains, tkd, tq), F32)] * nbuf + [pltpu.VMEM((chains, tkd, tq), BF16)] * nbuf
               + [pltpu.VMEM((chains, 1, tq), F32), pltpu.VMEM((chains, 1, tq), F32),
                  pltpu.VMEM((chains, acc_rows, tq), F32)])
    in_specs = [pl.BlockSpec((1, tq, qw), lambda bb, hp, i: (bb, i, q_cb + hp)),
                pl.BlockSpec((1, t_k, qw), lambda bb, hp, i: (bb, 0, k_cb + hp))]
    args = [q, k]
    if aug is not None:
        per_batch = aug.shape[0] > 1
        in_specs.append(pl.BlockSpec((1, t_k, LANE), lambda bb, hp, i: (bb if per_batch else 0, 0, 0)))
        args.append(aug)
    in_specs.append(pl.BlockSpec((1, 1) + vt.shape[2:], lambda bb, hp, i: (bb, hp, 0, 0, 0)))
    return pl.pallas_call(
        functools.partial(kernel, tq=tq, tkd=tkd, qoff=qoff, blocked=blocked, **kw),
        grid=(b, npairs, t_q // tq), in_specs=in_specs + extra_specs,
        out_specs=pl.BlockSpec((1, tq, LANE), lambda bb, hp, i: (bb, i, hp)),
        out_shape=jax.ShapeDtypeStruct((b, t_q, npairs * LANE), BF16), scratch_shapes=scratch,
        compiler_params=_cparams(("parallel", "parallel", "arbitrary")), name=name,
    )(*args, vt, *extras)


def _values_t(v, tkb):
    b, t, w = v.shape
    dv = LANE // 2
    vt = v.reshape(b, t // tkb, tkb, w // LANE, 2, dv).transpose(0, 3, 1, 4, 5, 2)
    ones = jnp.ones(vt.shape[:4] + (ONES_ROWS, tkb), v.dtype)
    return jnp.concatenate([vt, ones], axis=4).reshape(b, w // LANE, t // tkb, 2 * (dv + ONES_ROWS), tkb)


FFN_HALVES = 4


def _ffn_kernel(oa_ref, ob_ref, oc_ref, h_ref, wout_ref, gmix_ref,
                gpre_ref, wg_ref, wv_ref, cw_ref, cb_ref, wd_ref, gpost_ref, left_ref,
                p_ref, gple_ref, wgate_ref, wproj_ref, gplepost_ref,
                o_ref, st_ref, h1_sc, xn_sc, acc_sc, carry_sc, *, tm, tf, nsb, seq_blocks):
    i = pl.program_id(0)
    f = pl.program_id(1)
    nf = pl.num_programs(1)
    tb = tm // nsb

    @pl.when(f == 0)
    def _():
        y = (_dot(oa_ref[...], wout_ref[0:FOX_WIDTH, :])
             + _dot(ob_ref[...], wout_ref[FOX_WIDTH:FOX_WIDTH + MLA_WIDTH, :])
             + _dot(oc_ref[...], wout_ref[FOX_WIDTH + MLA_WIDTH:, :]))
        h1 = h_ref[...] + _rms(y, gmix_ref[...])
        h1_sc[...] = h1
        xn_sc[...] = _rms(h1, gpre_ref[...]).astype(BF16)
        acc_sc[...] = jnp.zeros_like(acc_sc)

    xn = xn_sc[...]
    th = tf // FFN_HALVES
    halves = [slice(a * th, (a + 1) * th) for a in range(FFN_HALVES)]
    ups = [(_dot(xn, wg_ref[:, cols]), _dot(xn, wv_ref[:, cols])) for cols in halves]
    rin = lax.broadcasted_iota(jnp.int32, (tm, 1), 0) & (tb - 1)

    def spread(rows):
        return jnp.broadcast_to(rows, (nsb, tb, th)).reshape(tm, th)

    for cols, (gate, val) in zip(halves, ups):
        left = left_ref[:, :, cols]
        if seq_blocks > 1:
            left = jnp.where(i % seq_blocks == 0, left, carry_sc[f, 0:CONV_WIDTH - 1, cols][None])
            carry_sc[f, 0:CONV_WIDTH - 1, cols] = gate[tm - (CONV_WIDTH - 1):, :]
        st_ref[f, pl.ds((i // seq_blocks) * nsb, nsb), :, cols] = (
            gate.reshape(nsb, tb, th)[:, tb - (CONV_WIDTH - 1):, :])
        l0 = spread(left[:, 0:1, :])
        l1 = spread(left[:, 1:2, :])
        g1 = jnp.where(rin == 0, l1, pltpu.roll(gate, 1, 0))
        g2 = jnp.where(rin == 0, l0, jnp.where(rin == 1, l1, pltpu.roll(gate, 2, 0)))
        conv = cw_ref[0:1, cols] * g2 + cw_ref[1:2, cols] * g1 + cw_ref[2:3, cols] * gate + cb_ref[:, cols]
        gelu = 0.5 * conv * (1.0 + jnp.tanh(math.sqrt(2.0 / math.pi) * (conv + 0.044715 * (conv * conv * conv))))
        acc_sc[...] += _dot((gelu * val).astype(BF16), wd_ref[cols, :])

    @pl.when(f == nf - 1)
    def _():
        h2 = h1_sc[...] + _rms(acc_sc[...], gpost_ref[...])
        gate = jax.nn.sigmoid(_dot(_rms(h2, gple_ref[...]).astype(BF16), wgate_ref[...]))
        proj = _dot(p_ref[...].astype(BF16), wproj_ref[...])
        o_ref[...] = h2 + _rms(proj * gate, gplepost_ref[...])


def _ffn_call(oa, ob, oc, h, p, lw, left, seq_len, tm, tf):
    n = h.shape[0]
    p_all, p_layer = p
    nseq = left.shape[0]
    nf = D_FF // tf
    if seq_len >= tm:
        nsb, seq_blocks = 1, seq_len // tm
    else:
        nsb, seq_blocks = tm // seq_len, 1
    assert (tm // nsb) & (tm // nsb - 1) == 0
    row = lambda i, f: (i, 0)
    full = lambda i, f: (0, 0)
    out, state = pl.pallas_call(
        functools.partial(_ffn_kernel, tm=tm, tf=tf, nsb=nsb, seq_blocks=seq_blocks),
        grid=(n // tm, nf),
        in_specs=[pl.BlockSpec((tm, FOX_WIDTH), row), pl.BlockSpec((tm, MLA_WIDTH), row),
                  pl.BlockSpec((tm, DIFF_WIDTH), row), pl.BlockSpec((tm, D_MODEL), row),
                  pl.BlockSpec((D_MODEL, D_MODEL), full), pl.BlockSpec((1, D_MODEL), full),
                  pl.BlockSpec((1, D_MODEL), full),
                  pl.BlockSpec((D_MODEL, tf), lambda i, f: (0, f)),
                  pl.BlockSpec((D_MODEL, tf), lambda i, f: (0, nf + f)),
                  pl.BlockSpec((CONV_WIDTH, tf), lambda i, f: (0, f)),
                  pl.BlockSpec((1, tf), lambda i, f: (0, f)),
                  pl.BlockSpec((tf, D_MODEL), lambda i, f: (f, 0)),
                  pl.BlockSpec((1, D_MODEL), full),
                  pl.BlockSpec((nsb, CONV_WIDTH - 1, tf), lambda i, f: (i // seq_blocks, 0, f)),
                  pl.BlockSpec((None, tm, PLE_DIM), lambda i, f: (p_layer, i, 0)), pl.BlockSpec((1, D_MODEL), full),
                  pl.BlockSpec((D_MODEL, D_MODEL), full), pl.BlockSpec((PLE_DIM, D_MODEL), full),
                  pl.BlockSpec((1, D_MODEL), full)],
        out_specs=[pl.BlockSpec((tm, D_MODEL), row),
                   pl.BlockSpec((nf, nseq, CONV_WIDTH - 1, tf), lambda i, f: (0, 0, 0, 0))],
        out_shape=[jax.ShapeDtypeStruct((n, D_MODEL), F32),
                   jax.ShapeDtypeStruct((nf, nseq, CONV_WIDTH - 1, tf), F32)],
        scratch_shapes=[pltpu.VMEM((tm, D_MODEL), F32), pltpu.VMEM((tm, D_MODEL), BF16),
                        pltpu.VMEM((tm, D_MODEL), F32), pltpu.VMEM((nf, 8, tf), F32)],
        compiler_params=_cparams(("arbitrary", "arbitrary")), name="ffn",
    )(oa, ob, oc, h, lw["w_out"], lw["g_mix_post"],
      lw["g_ffn_pre"], lw["w_up"], lw["w_up"], lw["conv_w"], lw["conv_b"], lw["w_down"], lw["g_ffn_post"], left,
      p_all, lw["g_ple_pre"], lw["w_ple_gate"], lw["w_ple_proj"], lw["g_ple_post"])
    return out, state.transpose(1, 2, 0, 3).reshape(nseq, CONV_WIDTH - 1, D_FF)


def _swap_halves(w):
    half = MLA_ROPE_DIM // 2
    return jnp.concatenate([w[..., half:], w[..., :half]], axis=-1)


def _pack_layer(w_in, b_forget, mla_q_norm, w_mla_uq, mla_kv_norm, w_mla_uk, w_mla_uv, lams, diff_subln, w_out,
                norm_mix_pre, norm_mix_post, norm_ffn_pre, norm_ffn_post, norm_ple_pre, norm_ple_post,
                w_ffn_up, ffn_conv_w, ffn_conv_b, w_ffn_down, w_ple_gate, w_ple_proj):
    zeros = lambda r, c: jnp.zeros((r, c), F32)
    w_kr = w_in[:, OFF_MLA_KR:OFF_DIFF_Q]
    w_all = jnp.concatenate([
        w_in[:, OFF_FOX_Q:OFF_FOX_F], w_in[:, OFF_DIFF_Q:IN_WIDTH],
        w_in[:, OFF_MLA_CQ:OFF_MLA_CKV], w_in[:, OFF_MLA_CKV:OFF_MLA_KR],
        w_kr, w_kr, zeros(D_MODEL, LANE - 2 * MLA_ROPE_DIM),
        _swap_halves(w_kr), _swap_halves(w_kr), zeros(D_MODEL, LANE - 2 * MLA_ROPE_DIM),
        w_in[:, OFF_FOX_F:OFF_MLA_CQ], zeros(D_MODEL, LANE - N_FOX_HEADS)], axis=1).astype(BF16)
    wq = w_mla_uq.reshape(MLA_Q_RANK, N_MLA_HEADS, MLA_NOPE_DIM + MLA_ROPE_DIM)
    plain, swapped = [], []
    pad = zeros(MLA_Q_RANK, MLA_PAIR_W - 2 * (MLA_NOPE_DIM + MLA_ROPE_DIM))
    for p in range(N_MLA_PAIRS):
        a, b = 2 * p, 2 * p + 1
        plain += [wq[:, a, :MLA_NOPE_DIM], wq[:, b, :MLA_NOPE_DIM], wq[:, a, MLA_NOPE_DIM:], wq[:, b, MLA_NOPE_DIM:], pad]
        swapped += [zeros(MLA_Q_RANK, 2 * MLA_NOPE_DIM), _swap_halves(wq[:, a, MLA_NOPE_DIM:]),
                    _swap_halves(wq[:, b, MLA_NOPE_DIM:]), pad]
    return dict(
        w_all=w_all,
        b_f=jnp.pad(b_forget, (0, LANE - N_FOX_HEADS)).reshape(1, LANE),
        g_q=mla_q_norm.reshape(1, -1), g_kv=mla_kv_norm.reshape(1, -1),
        w_uq2=jnp.concatenate(plain + swapped, axis=1).astype(BF16),
        w_ukv=jnp.concatenate([w_mla_uk, w_mla_uv], axis=1).astype(BF16),
        lams=lams, subln=jnp.tile(diff_subln, 2).reshape(1, LANE),
        w_out=w_out.astype(BF16),
        g_mix_pre=norm_mix_pre.reshape(1, -1), g_mix_post=norm_mix_post.reshape(1, -1),
        g_ffn_pre=norm_ffn_pre.reshape(1, -1), g_ffn_post=norm_ffn_post.reshape(1, -1),
        g_ple_pre=norm_ple_pre.reshape(1, -1), g_ple_post=norm_ple_post.reshape(1, -1),
        w_up=w_ffn_up.astype(BF16), conv_w=ffn_conv_w, conv_b=ffn_conv_b.reshape(1, -1),
        w_down=w_ffn_down.astype(BF16), w_ple_gate=w_ple_gate.astype(BF16), w_ple_proj=w_ple_proj.astype(BF16))


def _rope_tables(pos):
    half = MLA_ROPE_DIM // 2
    inv_freq = ROPE_THETA ** (-jnp.arange(half, dtype=F32) / half)
    ang = pos.astype(F32)[:, None] * inv_freq[None, :]
    cos, sin = jnp.cos(ang), jnp.sin(ang)
    t = pos.shape[0]
    pad = jnp.zeros((t, MLA_PAIR_W - LANE - 2 * MLA_ROPE_DIM), F32)
    cos_t = jnp.concatenate([jnp.ones((t, LANE), F32), cos, cos, cos, cos, pad], axis=1)
    sin_t = jnp.concatenate([jnp.zeros((t, LANE), F32), -sin, sin, -sin, sin, pad], axis=1)
    return cos_t, sin_t


def _alibi_slopes():
    s = 2.0 ** (-8.0 * np.arange(1, N_DIFF_HEADS + 1) / N_DIFF_HEADS)
    return jnp.asarray(np.broadcast_to(s.reshape(N_DIFF_HEADS // 2, 2, 1), (N_DIFF_HEADS // 2, 2, LANE)), dtype=F32)


def _tile_rows(n, pref):
    t = min(n, pref)
    assert n % t == 0
    return t


def _layer(h, p, lw, cos, sin, cache, conv_left, lam_init, prev_rows, *, batch, seq, past, tq, depth):
    n = h.shape[0]
    tm = _tile_rows(n, 512)
    layer = p[1]
    emit_vt = cache is None and tm == tq
    outs = _proj_call(h, lw, cos, sin, tm, layer, depth, prev_rows, seq if emit_vt else None)
    rows = tuple(outs[:N_ROW_OUTPUTS])
    fox_bf, diff_bf, mla_q, mla_k, mla_v = outs[N_ROW_OUTPUTS:N_ROW_OUTPUTS + 5]
    logf = rows[2][layer]
    b3 = lambda a: a.reshape(batch, seq, a.shape[-1])
    diff_extras = [_alibi_slopes(), lw["lams"], lw["subln"]]
    diff_specs = [pl.BlockSpec((1, 2, LANE), lambda bb, hp, i: (hp, 0, 0)),
                  pl.BlockSpec((4, DIFF_QK_DIM), lambda bb, hp, i: (0, 0)),
                  pl.BlockSpec((1, LANE), lambda bb, hp, i: (0, 0))]
    fb, db = b3(fox_bf), b3(diff_bf)
    if cache is None:
        common = dict(tq=tq, tkd=tq, qoff=0, blocked=True)
        if emit_vt:
            vt_fox, vt_mla, vt_diff = outs[N_ROW_OUTPUTS + 5:]
        else:
            vt_fox, vt_mla, vt_diff = (_values_t(fb[..., 2 * FOX_WIDTH:], tq), _values_t(b3(mla_v), tq),
                                       _values_t(db[..., 2 * DIFF_WIDTH:], tq))
        o_a = _attn_call(_fox_kernel, "fox_attn", fb, 0, fb, 3, _forget_bias_call(b3(logf)), vt_fox, [], [],
                         npairs=N_FOX_HEADS // 2, qw=LANE, chains=2, **common)
        o_b = _attn_call(_mla_kernel, "mla_attn", b3(mla_q), 0, b3(mla_k), 0, None, vt_mla, [], [],
                         npairs=N_MLA_PAIRS, qw=MLA_PAIR_W, chains=2, **common)
        o_c = _attn_call(_diff_kernel, "diff_attn", db, 0, db, 2, _alibi_bias_block(seq), vt_diff,
                         diff_extras, diff_specs,
                         npairs=N_DIFF_HEADS // 2, qw=LANE, chains=4, lam_init=lam_init, **common)
    else:
        c_fox_k, c_fox_v, c_logf, c_ckv, c_krope, c_diff_k, c_diff_v = cache
        tk = -(-(past + seq) // LANE) * LANE
        padk = lambda a: jnp.pad(a, ((0, 0), (0, tk - past - seq), (0, 0)))
        cat = lambda old, new: padk(jnp.concatenate([old.astype(new.dtype), new], axis=1))
        flat = lambda a: a.reshape(batch, past, -1)
        padq = lambda a: jnp.pad(a, ((0, 0), (0, tq - seq), (0, 0)))
        kf = cat(flat(c_fox_k), fb[..., FOX_WIDTH:2 * FOX_WIDTH])
        vf = cat(flat(c_fox_v), fb[..., 2 * FOX_WIDTH:])
        kd = cat(flat(c_diff_k), db[..., DIFF_WIDTH:2 * DIFF_WIDTH])
        vd = cat(flat(c_diff_v), db[..., 2 * DIFF_WIDTH:])
        kn_c, v_c = _kvup_call(c_ckv.reshape(batch * past, MLA_KV_RANK), lw["w_ukv"], _tile_rows(batch * past, 1024))
        kr_c = c_krope.astype(BF16)
        krp_c = jnp.concatenate([kr_c, kr_c, jnp.zeros((batch, past, LANE - 2 * MLA_ROPE_DIM), BF16)], axis=-1)
        km_c = jnp.concatenate([kn_c.reshape(batch, past, N_MLA_PAIRS, LANE),
                                jnp.broadcast_to(krp_c[:, :, None, :], (batch, past, N_MLA_PAIRS, LANE))],
                               axis=-1).reshape(batch, past, MLA_QK_W)
        km = cat(km_c, b3(mla_k))
        vm = cat(v_c.reshape(batch, past, MLA_WIDTH), b3(mla_v))
        common = dict(tq=tq, tkd=tk, qoff=past, blocked=False)
        o_a = _attn_call(_fox_kernel, "fox_attn_s", padq(fb[..., :FOX_WIDTH]), 0, kf, 0,
                         _forget_bias_call(cat(c_logf, b3(logf))), _values_t(vf, tk), [], [],
                         npairs=N_FOX_HEADS // 2, qw=LANE, chains=2, **common)
        o_b = _attn_call(_mla_kernel, "mla_attn_s", padq(b3(mla_q)), 0, km, 0, None, _values_t(vm, tk), [], [],
                         npairs=N_MLA_PAIRS, qw=MLA_PAIR_W, chains=2, **common)
        o_c = _attn_call(_diff_kernel, "diff_attn_s", padq(db[..., :DIFF_WIDTH]), 0, kd, 0, _alibi_bias_block(tk),
                         _values_t(vd, tk), diff_extras, diff_specs,
                         npairs=N_DIFF_HEADS // 2, qw=LANE, chains=4, lam_init=lam_init, **common)
        o_a, o_b, o_c = o_a[:, :seq], o_b[:, :seq], o_c[:, :seq]
    f2 = lambda a: a.reshape(n, a.shape[-1])
    h, conv_state = _ffn_call(f2(o_a), f2(o_b), f2(o_c), h, p, lw, conv_left, seq, tm, 2048)
    return h, rows, conv_state


def kernel(x_prompt, x_sample, cache_fox_k, cache_fox_v, cache_fox_logf, cache_mla_ckv, cache_mla_krope, cache_diff_k, cache_diff_v, state_ffn_conv, p_prompt, p_sample, w_in, b_forget, mla_q_norm, w_mla_uq, mla_kv_norm, w_mla_uk, w_mla_uv, diff_lambda_q1, diff_lambda_k1, diff_lambda_q2, diff_lambda_k2, diff_subln, w_out, norm_mix_pre, norm_mix_post, norm_ffn_pre, norm_ffn_post, norm_ple_pre, norm_ple_post, w_ffn_up, ffn_conv_w, ffn_conv_b, w_ffn_down, w_ple_gate, w_ple_proj):
    bp, sp, _ = x_prompt.shape
    bs, ts, _ = x_sample.shape
    depth, _, past = cache_fox_k.shape[:3]
    assert past % CHUNK == 0 and ts <= CHUNK and sp % LANE == 0
    tq = _tile_rows(sp, 512)
    tms = _tile_rows(bs * ts, 512)

    cos_p, sin_p = _rope_tables(jnp.arange(sp))
    cos_s, sin_s = _rope_tables(past + jnp.arange(ts))
    cos_s, sin_s = jnp.tile(cos_s, (tms // ts, 1)), jnp.tile(sin_s, (tms // ts, 1))

    hp = x_prompt.reshape(bp * sp, D_MODEL)
    hs = x_sample.reshape(bs * ts, D_MODEL)
    rows_p, rows_s, conv_p, conv_s = None, None, [], []
    for l in range(depth):
        lams = jnp.stack([diff_lambda_q1[l], diff_lambda_k1[l], diff_lambda_q2[l], diff_lambda_k2[l]], axis=0)
        lw = _pack_layer(w_in[l], b_forget[l], mla_q_norm[l], w_mla_uq[l], mla_kv_norm[l], w_mla_uk[l], w_mla_uv[l],
                         lams, diff_subln[l], w_out[l], norm_mix_pre[l], norm_mix_post[l], norm_ffn_pre[l],
                         norm_ffn_post[l], norm_ple_pre[l], norm_ple_post[l], w_ffn_up[l], ffn_conv_w[l],
                         ffn_conv_b[l], w_ffn_down[l], w_ple_gate[l], w_ple_proj[l])
        lam_init = 0.8 - 0.6 * math.exp(-0.3 * l)
        hp, rows_p, cp = _layer(hp, (p_prompt.reshape(depth, bp * sp, PLE_DIM), l), lw, cos_p, sin_p, None,
                                jnp.zeros((bp, CONV_WIDTH - 1, D_FF), F32), lam_init, rows_p,
                                batch=bp, seq=sp, past=0, tq=tq, depth=depth)
        cache_l = (cache_fox_k[l], cache_fox_v[l], cache_fox_logf[l], cache_mla_ckv[l], cache_mla_krope[l],
                   cache_diff_k[l], cache_diff_v[l])
        hs, rows_s, cs = _layer(hs, (p_sample.reshape(depth, bs * ts, PLE_DIM), l), lw, cos_s, sin_s, cache_l,
                                state_ffn_conv[l], lam_init, rows_s,
                                batch=bs, seq=ts, past=past, tq=LANE, depth=depth)
        conv_p.append(cp)
        conv_s.append(cs)

    def stack_rows(rows, batch, seq):
        fk, fv, lf, ckv, kr, dk, dv = rows
        lead = (depth, batch, seq)
        return (fk.reshape(lead + (N_FOX_HEADS, HEAD_DIM)), fv.reshape(lead + (N_FOX_HEADS, HEAD_DIM)),
                lf.reshape(lead + (N_FOX_HEADS,)), ckv.reshape(lead + (MLA_KV_RANK,)),
                kr.reshape(lead + (MLA_ROPE_DIM,)), dk.reshape(lead + (N_DIFF_HEADS, 2 * DIFF_QK_DIM)),
                dv.reshape(lead + (N_DIFF_HEADS, DIFF_V_DIM)))

    out_p = stack_rows(rows_p, bp, sp)
    out_s = stack_rows(rows_s, bs, ts)
    return ((hp.reshape(bp, sp, D_MODEL), hs.reshape(bs, ts, D_MODEL)) + out_p + (jnp.stack(conv_p, axis=0),)
            + out_s + (jnp.stack(conv_s, axis=0),))
```

```python
import functools
import math

import jax
import jax.numpy as jnp
import numpy as np
from jax import lax
from jax.experimental import pallas as pl
from jax.experimental.pallas import tpu as pltpu

F32 = jnp.float32
BF16 = jnp.bfloat16

D_MODEL = 1024
HEAD_DIM = 64
N_FOX_HEADS = 6
N_MLA_HEADS = 6
N_DIFF_HEADS = 4
MLA_Q_RANK = 384
MLA_KV_RANK = 256
MLA_NOPE_DIM = 64
MLA_ROPE_DIM = 32
MLA_V_DIM = 64
DIFF_QK_DIM = 32
DIFF_V_DIM = 64
FOX_WIDTH = N_FOX_HEADS * HEAD_DIM
MLA_WIDTH = N_MLA_HEADS * MLA_V_DIM
DIFF_WIDTH = N_DIFF_HEADS * DIFF_V_DIM
D_FF = 4 * D_MODEL
CONV_WIDTH = 3
PLE_DIM = 256
CHUNK = 64
ROPE_THETA = 10000.0
RMS_EPS = 1e-6
NEG_INF = -1e30
LOG2E = math.log2(math.e)

OFF_FOX_Q = 0
OFF_FOX_F = 3 * FOX_WIDTH
OFF_MLA_CQ = OFF_FOX_F + N_FOX_HEADS
OFF_MLA_CKV = OFF_MLA_CQ + MLA_Q_RANK
OFF_MLA_KR = OFF_MLA_CKV + MLA_KV_RANK
OFF_DIFF_Q = OFF_MLA_KR + MLA_ROPE_DIM
IN_WIDTH = OFF_DIFF_Q + 2 * N_DIFF_HEADS * 2 * DIFF_QK_DIM + DIFF_WIDTH

LANE = 128
V7X_VMEM_BYTES = 64 * 1024 * 1024
VMEM_LIMIT = (V7X_VMEM_BYTES * 7) // 8

C_FOX = 0
C_DIFF = C_FOX + 3 * FOX_WIDTH
C_CQ = C_DIFF + 3 * DIFF_WIDTH
C_CKV = C_CQ + MLA_Q_RANK
C_KR = C_CKV + MLA_KV_RANK
C_F = C_KR + 2 * LANE
W_ALL = C_F + LANE
N_MLA_PAIRS = N_MLA_HEADS // 2
MLA_PAIR_W = 2 * LANE
MLA_QK_W = N_MLA_PAIRS * MLA_PAIR_W

FOX_QSCALE = HEAD_DIM ** -0.5 * LOG2E
MLA_QSCALE = (MLA_NOPE_DIM + MLA_ROPE_DIM) ** -0.5 * LOG2E
DIFF_QSCALE = DIFF_QK_DIM ** -0.5 * LOG2E


def _cparams(sem):
    return pltpu.CompilerParams(dimension_semantics=sem, vmem_limit_bytes=VMEM_LIMIT)


def _rms(x, g):
    return x * lax.rsqrt(jnp.mean(x * x, axis=-1, keepdims=True) + RMS_EPS) * g


def _dot(a, b):
    return jnp.dot(a, b, preferred_element_type=F32)


def _dot_nt(a, b):
    return lax.dot_general(a, b, (((1,), (1,)), ((), ())), preferred_element_type=F32)


def _store_values_t(vt_ref, v):
    dv = LANE // 2
    ones = jnp.ones((ONES_ROWS, v.shape[0]), BF16)
    for p in range(v.shape[1] // LANE):
        t = v[:, p * LANE:(p + 1) * LANE].T.astype(BF16)
        for e in range(2):
            r0 = e * (dv + ONES_ROWS)
            vt_ref[0, p, 0, r0:r0 + dv, :] = t[e * dv:(e + 1) * dv, :]
            vt_ref[0, p, 0, r0 + dv:r0 + dv + ONES_ROWS, :] = ones


def _proj_kernel(h_ref, gpre_ref, w_ref, bf_ref, gq_ref, gkv_ref, wuq_ref, wukv_ref, cos_ref, sin_ref, *rest,
                 n_alias, emit_vt):
    (foxk_ref, foxv_ref, logf_ref, ckv_ref, krope_ref, diffk_ref, diffv_ref,
     foxbf_ref, diffbf_ref, mlaq_ref, mlak_ref, mlav_ref) = rest[n_alias:n_alias + 12]
    xn = _rms(h_ref[...], gpre_ref[...]).astype(BF16)
    cos = cos_ref[...]
    sin = sin_ref[...]

    z = _dot(xn, w_ref[:, C_FOX:C_FOX + 3 * FOX_WIDTH])
    foxbf_ref[:, :FOX_WIDTH] = (z[:, :FOX_WIDTH] * FOX_QSCALE).astype(BF16)
    foxbf_ref[:, FOX_WIDTH:] = z[:, FOX_WIDTH:].astype(BF16)
    foxk_ref[...] = z[:, FOX_WIDTH:2 * FOX_WIDTH]
    foxv_ref[...] = z[:, 2 * FOX_WIDTH:3 * FOX_WIDTH]
    if emit_vt:
        _store_values_t(rest[n_alias + 13], z[:, 2 * FOX_WIDTH:3 * FOX_WIDTH])

    z = _dot(xn, w_ref[:, C_DIFF:C_DIFF + 3 * DIFF_WIDTH])
    diffbf_ref[:, :DIFF_WIDTH] = (z[:, :DIFF_WIDTH] * DIFF_QSCALE).astype(BF16)
    diffbf_ref[:, DIFF_WIDTH:] = z[:, DIFF_WIDTH:].astype(BF16)
    diffk_ref[...] = z[:, DIFF_WIDTH:2 * DIFF_WIDTH]
    diffv_ref[...] = z[:, 2 * DIFF_WIDTH:3 * DIFF_WIDTH]
    if emit_vt:
        _store_values_t(rest[n_alias + 15], z[:, 2 * DIFF_WIDTH:3 * DIFF_WIDTH])

    cq = _rms(_dot(xn, w_ref[:, C_CQ:C_CQ + MLA_Q_RANK]), gq_ref[...]).astype(BF16)
    q2 = _dot(cq, wuq_ref[...])
    for p in range(N_MLA_PAIRS):
        lo = p * MLA_PAIR_W
        roped = q2[:, lo:lo + MLA_PAIR_W] * cos + q2[:, MLA_QK_W + lo:MLA_QK_W + lo + MLA_PAIR_W] * sin
        mlaq_ref[:, lo:lo + MLA_PAIR_W] = (roped * MLA_QSCALE).astype(BF16)

    ckv = _rms(_dot(xn, w_ref[:, C_CKV:C_CKV + MLA_KV_RANK]), gkv_ref[...])
    ckv_ref[...] = ckv
    kv = _dot(ckv.astype(BF16), wukv_ref[...])
    mlav_ref[...] = kv[:, MLA_WIDTH:2 * MLA_WIDTH].astype(BF16)
    if emit_vt:
        _store_values_t(rest[n_alias + 14], kv[:, MLA_WIDTH:2 * MLA_WIDTH])
    zkr =_dot(xn, w_ref[:, C_KR:C_KR + 2 * LANE])
    krp = zkr[:, :LANE] * cos[:, LANE:] + zkr[:, LANE:] * sin[:, LANE:]
    krope_ref[...] = krp[:, :MLA_ROPE_DIM]
    krp16 = krp.astype(BF16)
    for p in range(N_MLA_PAIRS):
        lo = p * MLA_PAIR_W
        mlak_ref[:, lo:lo + LANE] = kv[:, p * LANE:(p + 1) * LANE].astype(BF16)
        mlak_ref[:, lo + LANE:lo + 2 * LANE] = krp16

    zf = _dot(xn, w_ref[:, C_F:C_F + LANE]) + bf_ref[...]
    logf = -(jnp.maximum(-zf, 0.0) + jnp.log1p(jnp.exp(-jnp.abs(zf))))
    logf_ref[...] = logf[:, :N_FOX_HEADS]
    lane = lax.broadcasted_iota(jnp.int32, (1, LANE), 1)
    rest[n_alias + 12][...] = jnp.where(lane < N_FOX_HEADS, logf, 0.0)


N_ROW_OUTPUTS = 7


def _proj_call(h, lw, cos, sin, tm, layer, depth, prev_rows, vt_seq):
    n = h.shape[0]
    ntab = cos.shape[0] // tm
    row = lambda i: (i, 0)
    full = lambda i: (0, 0)
    tab = lambda i: (i % ntab, 0)
    in_specs = [
        pl.BlockSpec((tm, D_MODEL), row),
        pl.BlockSpec((1, D_MODEL), full),
        pl.BlockSpec((D_MODEL, W_ALL), full),
        pl.BlockSpec((1, LANE), full),
        pl.BlockSpec((1, MLA_Q_RANK), full),
        pl.BlockSpec((1, MLA_KV_RANK), full),
        pl.BlockSpec((MLA_Q_RANK, 2 * MLA_QK_W), full),
        pl.BlockSpec((MLA_KV_RANK, 2 * MLA_WIDTH), full),
        pl.BlockSpec((tm, MLA_PAIR_W), tab),
        pl.BlockSpec((tm, MLA_PAIR_W), tab),
    ]
    widths = [(FOX_WIDTH, F32), (FOX_WIDTH, F32), (N_FOX_HEADS, F32), (MLA_KV_RANK, F32), (MLA_ROPE_DIM, F32),
              (DIFF_WIDTH, F32), (DIFF_WIDTH, F32),
              (3 * FOX_WIDTH, BF16), (3 * DIFF_WIDTH, BF16), (MLA_QK_W, BF16), (MLA_QK_W, BF16), (MLA_WIDTH, BF16),
              (LANE, F32)]
    out_shape = ([jax.ShapeDtypeStruct((depth, n, w), dt) for w, dt in widths[:N_ROW_OUTPUTS]]
                 + [jax.ShapeDtypeStruct((n, w), dt) for w, dt in widths[N_ROW_OUTPUTS:]])
    out_specs = ([pl.BlockSpec((None, tm, w), lambda i: (layer, i, 0)) for w, _ in widths[:N_ROW_OUTPUTS]]
                 + [pl.BlockSpec((tm, w), row) for w, _ in widths[N_ROW_OUTPUTS:]])
    if vt_seq is not None:
        nblk = vt_seq // tm
        for width in (FOX_WIDTH, MLA_WIDTH, DIFF_WIDTH):
            shape = (n // vt_seq, width // LANE, nblk, 2 * (LANE // 2 + ONES_ROWS), tm)
            out_shape.append(jax.ShapeDtypeStruct(shape, BF16))
            out_specs.append(pl.BlockSpec((1,) + shape[1:2] + (1,) + shape[3:],
                                          lambda i: (i // nblk, 0, i % nblk, 0, 0)))
    args = [h, lw["g_mix_pre"], lw["w_all"], lw["b_f"], lw["g_q"], lw["g_kv"], lw["w_uq2"], lw["w_ukv"], cos, sin]
    aliases = {}
    if prev_rows is not None:
        aliases = {len(args) + k: k for k in range(N_ROW_OUTPUTS)}
        in_specs = in_specs + [pl.BlockSpec(memory_space=pl.ANY)] * N_ROW_OUTPUTS
        args = args + list(prev_rows)
    return pl.pallas_call(
        functools.partial(_proj_kernel, n_alias=len(aliases), emit_vt=vt_seq is not None), grid=(n // tm,),
        in_specs=in_specs,
        out_specs=out_specs, out_shape=out_shape, input_output_aliases=aliases,
        compiler_params=_cparams(("parallel",)), name="proj",
    )(*args)


def _kvup_kernel(ckv_ref, kr_ref, wukv_ref, place_ref, k_ref, v_ref):
    kv = _dot(ckv_ref[...].astype(BF16), wukv_ref[...])
    v_ref[...] = kv[:, MLA_WIDTH:].astype(BF16)
    krp = _dot(kr_ref[...].astype(BF16), place_ref[...]).astype(BF16)
    for p in range(N_MLA_PAIRS):
        lo = p * MLA_PAIR_W
        k_ref[:, lo:lo + LANE] = kv[:, p * LANE:(p + 1) * LANE].astype(BF16)
        k_ref[:, lo + LANE:lo + 2 * LANE] = krp


def _kvup_call(ckv_all, krope_all, layer, w_ukv, tm):
    n = ckv_all.shape[1]
    row = lambda i: (i, 0)
    lrow = lambda i: (layer, i, 0)
    full = lambda i: (0, 0)
    eye = jnp.eye(MLA_ROPE_DIM, dtype=BF16)
    place = jnp.concatenate([eye, eye, jnp.zeros((MLA_ROPE_DIM, LANE - 2 * MLA_ROPE_DIM), BF16)], axis=1)
    return pl.pallas_call(
        _kvup_kernel, grid=(n // tm,),
        in_specs=[pl.BlockSpec((None, tm, MLA_KV_RANK), lrow), pl.BlockSpec((None, tm, MLA_ROPE_DIM), lrow),
                  pl.BlockSpec((MLA_KV_RANK, 2 * MLA_WIDTH), full), pl.BlockSpec((MLA_ROPE_DIM, LANE), full)],
        out_specs=[pl.BlockSpec((tm, MLA_QK_W), row), pl.BlockSpec((tm, MLA_WIDTH), row)],
        out_shape=[jax.ShapeDtypeStruct((n, MLA_QK_W), BF16), jax.ShapeDtypeStruct((n, MLA_WIDTH), BF16)],
        compiler_params=_cparams(("parallel",)), name="kvup",
    )(ckv_all, krope_all, w_ukv, place)


AUG_STRIDE = 8


def _keep_bf16_bits(x):
    bits = lax.bitcast_convert_type(x, jnp.uint32) & jnp.uint32(0xFFFF0000)
    return lax.bitcast_convert_type(bits, F32)


def _split3(x):
    x1 = _keep_bf16_bits(x)
    r = x - x1
    x2 = _keep_bf16_bits(r)
    return x1, x2, r - x2


def _place3(x1, x2, x3):
    lane = lax.broadcasted_iota(jnp.int32, (1, LANE), 1)
    return jnp.where(lane < AUG_STRIDE, x1,
                     jnp.where(lane < 2 * AUG_STRIDE, pltpu.roll(x2, AUG_STRIDE, 1),
                               pltpu.roll(x3, 2 * AUG_STRIDE, 1))).astype(BF16)


def _aug_selector(h, rows):
    lane = lax.broadcasted_iota(jnp.int32, (rows, LANE), 1)
    hit = (lane == h) | (lane == AUG_STRIDE + h) | (lane == 2 * AUG_STRIDE + h)
    return jnp.where(hit, 1.0, 0.0).astype(BF16)


def _forget_bias_kernel(x_ref, tri_ref, o_ref, carry_sc, *, tb):
    @pl.when(pl.program_id(1) == 0)
    def _():
        carry_sc[...] = jnp.zeros_like(carry_sc)

    terms = jnp.concatenate(_split3(x_ref[0]), axis=1).astype(BF16)
    sums = _dot(tri_ref[...], terms)
    cs = carry_sc[0:1, :] + sums[:, :LANE] + sums[:, LANE:2 * LANE] + sums[:, 2 * LANE:]
    carry_sc[...] = jnp.broadcast_to(cs[tb - 1:tb, :], carry_sc.shape)
    o_ref[0] = _place3(*_split3(cs * (-LOG2E)))


def _forget_bias_call(logf):
    b, t, _ = logf.shape
    tb = max(d for d in range(LANE, 4 * LANE + 1, LANE) if t % d == 0)
    x = logf
    tri = jnp.tril(jnp.ones((tb, tb), BF16))
    blk = pl.BlockSpec((1, tb, LANE), lambda i, j: (i, j, 0))
    return pl.pallas_call(
        functools.partial(_forget_bias_kernel, tb=tb), grid=(b, t // tb),
        in_specs=[blk, pl.BlockSpec((tb, tb), lambda i, j: (0, 0))], out_specs=blk,
        out_shape=jax.ShapeDtypeStruct((b, t, LANE), BF16),
        scratch_shapes=[pltpu.VMEM((8, LANE), F32)],
        compiler_params=_cparams(("parallel", "arbitrary")), name="forget_bias",
    )(x, tri)


def _alibi_bias_block(t_k):
    slopes = 2.0 ** (-8.0 * np.arange(1, N_DIFF_HEADS + 1) / N_DIFF_HEADS)
    b = jnp.zeros((t_k, LANE), F32).at[:, :N_DIFF_HEADS].set(
        jnp.arange(t_k, dtype=F32)[:, None] * jnp.asarray(slopes * LOG2E, F32)[None, :])
    x1, x2, x3 = _split3(b)
    lane = jnp.arange(LANE)[None, :]
    placed = jnp.where(lane < AUG_STRIDE, x1, jnp.where(lane < 2 * AUG_STRIDE, jnp.roll(x2, AUG_STRIDE, 1),
                                                        jnp.roll(x3, 2 * AUG_STRIDE, 1)))
    return placed.astype(BF16)[None]


ROW_CHUNK = CHUNK
STRIP = 4 * LANE
ONES_ROWS = 16


def _masked_chunks(s_ref, c, mask, tk, tq):
    chunks = [(r0, "all" if mask is None else mask[0](r0, 0, tq)) for r0 in range(0, tk, ROW_CHUNK)]

    def logits(r0, vis):
        x = s_ref[c, r0:r0 + ROW_CHUNK, :]
        return x if vis == "all" else mask[1](r0, 0, x)
    return chunks, logits


def _softmax_max(s_ref, m_ref, c, mask, tk, tq):
    chunks, logits = _masked_chunks(s_ref, c, mask, tk, tq)
    mrun = jnp.full((8, tq), NEG_INF, F32)
    for r0, vis in chunks:
        if vis != "none":
            mrun = jnp.maximum(mrun, jnp.max(logits(r0, vis).reshape(ROW_CHUNK // 8, 8, tq), axis=0))
    m_old = m_ref[c]
    return m_old, jnp.maximum(m_old, jnp.max(mrun, axis=0, keepdims=True))


def _softmax_exp(s_ref, p_ref, m_ref, alpha_ref, c, mask, tk, tq, m_old, m_new):
    chunks, logits = _masked_chunks(s_ref, c, mask, tk, tq)
    for r0, vis in chunks:
        if vis == "none":
            p_ref[c, r0:r0 + ROW_CHUNK, :] = jnp.zeros((ROW_CHUNK, tq), BF16)
        else:
            p_ref[c, r0:r0 + ROW_CHUNK, :] = jnp.exp2(logits(r0, vis) - m_new).astype(BF16)
    m_ref[c] = m_new
    alpha_ref[c] = jnp.exp2(m_old - m_new)


def _apply_values(p_ref, alpha_ref, acc_ref, c, vt):
    acc_ref[c] = alpha_ref[c] * acc_ref[c] + _dot(vt, p_ref[c])


def _causal_mask(qoff):
    def visibility(r0, c0, sw):
        if r0 + ROW_CHUNK - 1 <= c0 + qoff:
            return "all"
        return "none" if r0 > c0 + sw - 1 + qoff else "some"

    def apply(r0, c0, x):
        kk, qq = _key_query_iota(*x.shape)
        return jnp.where(kk - qq <= c0 + qoff - r0, x, NEG_INF)
    return visibility, apply


def _chunk_visibility(r0, c0, sw, qoff):
    if r0 // CHUNK <= (c0 + qoff) // CHUNK:
        return "all"
    return "none" if r0 // CHUNK > (c0 + sw - 1 + qoff) // CHUNK else "some"


def _chunk_mask(r0, c0, sw, qoff):
    qq = lax.broadcasted_iota(jnp.int32, (1, sw), 1)
    return r0 // CHUNK <= (qq + (c0 + qoff)) // CHUNK


def _init_states(m_ref, acc_ref):
    m_ref[...] = jnp.full(m_ref.shape, NEG_INF, F32)
    acc_ref[...] = jnp.zeros(acc_ref.shape, F32)


def _normalized(acc_ref, c, dv):
    return acc_ref[c, :dv, :] / acc_ref[c, dv:dv + 1, :]


def _key_query_iota(tk, tq):
    return (lax.broadcasted_iota(jnp.int32, (tk, tq), 0), lax.broadcasted_iota(jnp.int32, (tk, tq), 1))


def _store_pair(o_ref, o0, o1, row_scale=None):
    out = jnp.concatenate([o0, o1], axis=0).T
    if row_scale is not None:
        out = out * row_scale
    o_ref[0] = out.astype(BF16)


def _block_start(j, size):
    return j * size if isinstance(j, int) else pl.multiple_of(j * size, size)


N_BUF = 3


def _attend(i, blocked, chains, qk_fn, max_fn, exp_fn, pv_fn):
    def step(qk, pv, buf, diag):
        if qk is not None:
            all_chains(qk_fn, *qk)
        if pv is not None:
            all_chains(pv_fn, *pv)
        for c in range(chains):
            exp_fn(buf, c, diag, *max_fn(buf, c, diag))

    def all_chains(fn, *args):
        for c in range(chains):
            fn(*args, c)

    if not blocked:
        all_chains(qk_fn, 0, 0)
        step(None, None, 0, True)
        all_chains(pv_fn, 0, 0)
        return
    last = N_BUF - 1
    lead = i % N_BUF

    def single(j, carry):
        all_chains(qk_fn, last, j)
        step(None, None, last, False)
        all_chains(pv_fn, last, j)
        return carry

    lax.fori_loop(0, lead, single, 0)
    all_chains(qk_fn, 0, lead)
    all_chains(qk_fn, 1, jnp.minimum(lead + 1, i))

    def rotate(t, carry):
        b0 = lead + N_BUF * t
        for u in range(N_BUF):
            step(((u + 2) % N_BUF, jnp.minimum(b0 + u + 2, i)), (u - 1, b0 + u - 1) if u > 0 else None, u, False)
        all_chains(pv_fn, last, b0 + last)
        return carry

    lax.fori_loop(0, (i - lead) // N_BUF, rotate, 0)
    step(None, None, 0, True)
    all_chains(pv_fn, 0, i)


def _vt_rows(vt_ref, j, e, dv):
    return vt_ref[0, 0, j, e * (dv + ONES_ROWS):(e + 1) * (dv + ONES_ROWS), :]


def _split_scratch(scratch):
    nbuf = (len(scratch) - 3) // 2
    return (scratch[:nbuf], scratch[nbuf:2 * nbuf]) + tuple(scratch[2 * nbuf:])


def _chain_fns(scratch, vt_ref, mask_of, dv, head_of, tkd, tq):
    s_sc, p_sc, m_sc, alpha_sc, acc_sc = _split_scratch(scratch)

    def max_fn(buf, c, diag):
        return _softmax_max(s_sc[buf], m_sc, c, mask_of(c) if diag else None, tkd, tq)

    def exp_fn(buf, c, diag, m_old, m_new):
        _softmax_exp(s_sc[buf], p_sc[buf], m_sc, alpha_sc, c, mask_of(c) if diag else None, tkd, tq, m_old, m_new)

    def pv_fn(buf, j, c):
        _apply_values(p_sc[buf], alpha_sc, acc_sc, c, _vt_rows(vt_ref, j, head_of(c), dv))
    return max_fn, exp_fn, pv_fn


def _fox_kernel(q_ref, k_ref, aug_ref, vt_ref, o_ref, *scratch, tq, tkd, qoff, blocked):
    s_sc, p_sc, m_sc, alpha_sc, acc_sc = _split_scratch(scratch)
    hp, i = pl.program_id(1), pl.program_id(2)
    _init_states(m_sc, acc_sc)
    q = q_ref[0]
    lane = lax.broadcasted_iota(jnp.int32, (1, LANE), 1)
    qs = [jnp.concatenate([jnp.where((lane >= e * HEAD_DIM) & (lane < (e + 1) * HEAD_DIM), q, jnp.zeros_like(q)),
                           _aug_selector(2 * hp + e, tq)], axis=1) for e in range(2)]

    def qk_fn(buf, j, e):
        rows = pl.ds(_block_start(j, tq), tkd)
        k = jnp.concatenate([k_ref[0, rows, :], aug_ref[0, rows, :]], axis=1)
        s_sc[buf][e] = _dot_nt(k, qs[e])

    _attend(i, blocked, 2, qk_fn, *_chain_fns(scratch, vt_ref, lambda e: _causal_mask(qoff), HEAD_DIM,
                                              lambda e: e, tkd, tq))
    _store_pair(o_ref, *(_normalized(acc_sc, e, HEAD_DIM) for e in range(2)))


def _mla_kernel(q_ref, k_ref, vt_ref, o_ref, *scratch, tq, tkd, qoff, blocked):
    s_sc, p_sc, m_sc, alpha_sc, acc_sc = _split_scratch(scratch)
    i = pl.program_id(2)
    _init_states(m_sc, acc_sc)
    q = q_ref[0]
    lane2 = lax.broadcasted_iota(jnp.int32, (1, MLA_PAIR_W), 1)
    sel = []
    for e in range(2):
        nope = (lane2 >= e * MLA_NOPE_DIM) & (lane2 < (e + 1) * MLA_NOPE_DIM)
        rope = (lane2 >= LANE + e * MLA_ROPE_DIM) & (lane2 < LANE + (e + 1) * MLA_ROPE_DIM)
        sel.append(jnp.where(nope | rope, q, jnp.zeros_like(q)))

    def qk_fn(buf, j, e):
        s_sc[buf][e] = _dot_nt(k_ref[0, pl.ds(_block_start(j, tq), tkd), :], sel[e])

    mask = (lambda r0, c0, sw: _chunk_visibility(r0, c0, sw, qoff),
            lambda r0, c0, x: jnp.where(_chunk_mask(r0, c0, x.shape[1], qoff), x, NEG_INF))
    _attend(i, blocked, 2, qk_fn, *_chain_fns(scratch, vt_ref, lambda e: mask, MLA_V_DIM, lambda e: e, tkd, tq))
    _store_pair(o_ref, *(_normalized(acc_sc, e, MLA_V_DIM) for e in range(2)))


def _diff_kernel(q_ref, k_ref, aug_ref, vt_ref, slope_ref, lam_ref, subln_ref, o_ref, *scratch,
                 tq, tkd, qoff, blocked, lam_init):
    s_sc, p_sc, m_sc, alpha_sc, acc_sc = _split_scratch(scratch)
    hp, i = pl.program_id(1), pl.program_id(2)
    _init_states(m_sc, acc_sc)
    q = q_ref[0]
    lane = lax.broadcasted_iota(jnp.int32, (1, LANE), 1)
    sel = [jnp.concatenate([jnp.where((lane >= (2 * e + t) * DIFF_QK_DIM) & (lane < (2 * e + t + 1) * DIFF_QK_DIM),
                                      q, jnp.zeros_like(q)), _aug_selector(2 * hp + e, tq)], axis=1)
           for e in range(2) for t in range(2)]
    slopes = [slope_ref[0, e:e + 1, 0:1] * LOG2E for e in range(2)]

    def qk_fn(buf, j, c):
        rows = pl.ds(_block_start(j, tq), tkd)
        k = jnp.concatenate([k_ref[0, rows, :], aug_ref[0, rows, :]], axis=1)
        s_sc[buf][c] = _dot_nt(k, sel[c])

    def mask_of(c):
        def visibility(r0, c0, sw):
            vis = _chunk_visibility(r0, c0, sw, qoff)
            return "some" if vis == "all" and r0 + ROW_CHUNK - 1 > c0 + qoff else vis

        def apply(r0, c0, x):
            kk, qq = _key_query_iota(*x.shape)
            ahead = jnp.maximum(kk - qq + (r0 - c0 - qoff), 0).astype(F32)
            return jnp.where(_chunk_mask(r0, c0, x.shape[1], qoff), x - (2.0 * slopes[c // 2]) * ahead, NEG_INF)
        return visibility, apply

    _attend(i, blocked, 4, qk_fn, *_chain_fns(scratch, vt_ref, mask_of, DIFF_V_DIM, lambda c: c // 2, tkd, tq))
    res = [_normalized(acc_sc, c, DIFF_V_DIM) for c in range(4)]
    lq = lam_ref[...]
    lam = (jnp.exp(jnp.sum(lq[0:1] * lq[1:2], axis=-1, keepdims=True))
           - jnp.exp(jnp.sum(lq[2:3] * lq[3:4], axis=-1, keepdims=True)) + lam_init)
    outs = []
    for e in range(2):
        o = res[2 * e] - lam * res[2 * e + 1]
        outs.append(o * lax.rsqrt(jnp.mean(o * o, axis=0, keepdims=True) + RMS_EPS))
    _store_pair(o_ref, outs[0], outs[1], subln_ref[...] * (1.0 - lam_init))


def _attn_call(kernel, name, q, q_cb, k, k_cb, aug, vt, extras, extra_specs, *, npairs, qw, tq, tkd, qoff, blocked,
               chains, **kw):
    b, t_q = q.shape[0], q.shape[1]
    t_k = k.shape[1]
    acc_rows = LANE // 2 + ONES_ROWS
    nbuf = N_BUF if blocked else 1
    scratch = ([pltpu.VMEM((chains, tkd, tq), F32)] * nbuf + [pltpu.VMEM((chains, tkd, tq), BF16)] * nbuf
               + [pltpu.VMEM((chains, 1, tq), F32), pltpu.VMEM((chains, 1, tq), F32),
                  pltpu.VMEM((chains, acc_rows, tq), F32)])
    in_specs = [pl.BlockSpec((1, tq, qw), lambda bb, hp, i: (bb, i, q_cb + hp)),
                pl.BlockSpec((1, t_k, qw), lambda bb, hp, i: (bb, 0, k_cb + hp))]
    args = [q, k]
    if aug is not None:
        per_batch = aug.shape[0] > 1
        in_specs.append(pl.BlockSpec((1, t_k, LANE), lambda bb, hp, i: (bb if per_batch else 0, 0, 0)))
        args.append(aug)
    in_specs.append(pl.BlockSpec((1, 1) + vt.shape[2:], lambda bb, hp, i: (bb, hp, 0, 0, 0)))
    return pl.pallas_call(
        functools.partial(kernel, tq=tq, tkd=tkd, qoff=qoff, blocked=blocked, **kw),
        grid=(b, npairs, t_q // tq), in_specs=in_specs + extra_specs,
        out_specs=pl.BlockSpec((1, tq, LANE), lambda bb, hp, i: (bb, i, hp)),
        out_shape=jax.ShapeDtypeStruct((b, t_q, npairs * LANE), BF16), scratch_shapes=scratch,
        compiler_params=_cparams(("parallel", "parallel", "arbitrary")), name=name,
    )(*args, vt, *extras)


def _values_t(v, tkb):
    b, t, w = v.shape
    dv = LANE // 2
    vt = v.reshape(b, t // tkb, tkb, w // LANE, 2, dv).transpose(0, 3, 1, 4, 5, 2)
    ones = jnp.ones(vt.shape[:4] + (ONES_ROWS, tkb), v.dtype)
    return jnp.concatenate([vt, ones], axis=4).reshape(b, w // LANE, t // tkb, 2 * (dv + ONES_ROWS), tkb)


FFN_HALVES = 4


def _ffn_kernel(oa_ref, ob_ref, oc_ref, h_ref, wout_ref, gmix_ref,
                gpre_ref, wg_ref, wv_ref, cw_ref, cb_ref, wd_ref, gpost_ref, left_ref,
                p_ref, gple_ref, wgate_ref, wproj_ref, gplepost_ref,
                o_ref, st_ref, h1_sc, xn_sc, acc_sc, carry_sc, *, tm, tf, nsb, seq_blocks):
    i = pl.program_id(0)
    f = pl.program_id(1)
    nf = pl.num_programs(1)
    tb = tm // nsb

    @pl.when(f == 0)
    def _():
        y = (_dot(oa_ref[...], wout_ref[0:FOX_WIDTH, :])
             + _dot(ob_ref[...], wout_ref[FOX_WIDTH:FOX_WIDTH + MLA_WIDTH, :])
             + _dot(oc_ref[...], wout_ref[FOX_WIDTH + MLA_WIDTH:, :]))
        h1 = h_ref[...] + _rms(y, gmix_ref[...])
        h1_sc[...] = h1
        xn_sc[...] = _rms(h1, gpre_ref[...]).astype(BF16)
        acc_sc[...] = jnp.zeros_like(acc_sc)

    xn = xn_sc[...]
    th = tf // FFN_HALVES
    halves = [slice(a * th, (a + 1) * th) for a in range(FFN_HALVES)]
    ups = [(_dot(xn, wg_ref[:, cols]), _dot(xn, wv_ref[:, cols])) for cols in halves]
    rin = lax.broadcasted_iota(jnp.int32, (tm, 1), 0) & (tb - 1)

    def spread(rows):
        return jnp.broadcast_to(rows, (nsb, tb, th)).reshape(tm, th)

    for cols, (gate, val) in zip(halves, ups):
        left = left_ref[:, :, cols]
        if seq_blocks > 1:
            left = jnp.where(i % seq_blocks == 0, left, carry_sc[f, 0:CONV_WIDTH - 1, cols][None])
            carry_sc[f, 0:CONV_WIDTH - 1, cols] = gate[tm - (CONV_WIDTH - 1):, :]
        st_ref[f, pl.ds((i // seq_blocks) * nsb, nsb), :, cols] = (
            gate.reshape(nsb, tb, th)[:, tb - (CONV_WIDTH - 1):, :])
        l0 = spread(left[:, 0:1, :])
        l1 = spread(left[:, 1:2, :])
        g1 = jnp.where(rin == 0, l1, pltpu.roll(gate, 1, 0))
        g2 = jnp.where(rin == 0, l0, jnp.where(rin == 1, l1, pltpu.roll(gate, 2, 0)))
        conv = cw_ref[0:1, cols] * g2 + cw_ref[1:2, cols] * g1 + cw_ref[2:3, cols] * gate + cb_ref[:, cols]
        gelu = 0.5 * conv * (1.0 + jnp.tanh(math.sqrt(2.0 / math.pi) * (conv + 0.044715 * (conv * conv * conv))))
        acc_sc[...] += _dot((gelu * val).astype(BF16), wd_ref[cols, :])

    @pl.when(f == nf - 1)
    def _():
        h2 = h1_sc[...] + _rms(acc_sc[...], gpost_ref[...])
        gate = jax.nn.sigmoid(_dot(_rms(h2, gple_ref[...]).astype(BF16), wgate_ref[...]))
        proj = _dot(p_ref[...].astype(BF16), wproj_ref[...])
        o_ref[...] = h2 + _rms(proj * gate, gplepost_ref[...])


def _ffn_call(oa, ob, oc, h, p, lw, left, seq_len, tm, tf):
    n = h.shape[0]
    p_all, p_layer = p
    nseq = left.shape[0]
    nf = D_FF // tf
    if seq_len >= tm:
        nsb, seq_blocks = 1, seq_len // tm
    else:
        nsb, seq_blocks = tm // seq_len, 1
    assert (tm // nsb) & (tm // nsb - 1) == 0
    row = lambda i, f: (i, 0)
    full = lambda i, f: (0, 0)
    out, state = pl.pallas_call(
        functools.partial(_ffn_kernel, tm=tm, tf=tf, nsb=nsb, seq_blocks=seq_blocks),
        grid=(n // tm, nf),
        in_specs=[pl.BlockSpec((tm, FOX_WIDTH), row), pl.BlockSpec((tm, MLA_WIDTH), row),
                  pl.BlockSpec((tm, DIFF_WIDTH), row), pl.BlockSpec((tm, D_MODEL), row),
                  pl.BlockSpec((D_MODEL, D_MODEL), full), pl.BlockSpec((1, D_MODEL), full),
                  pl.BlockSpec((1, D_MODEL), full),
                  pl.BlockSpec((D_MODEL, tf), lambda i, f: (0, f)),
                  pl.BlockSpec((D_MODEL, tf), lambda i, f: (0, nf + f)),
                  pl.BlockSpec((CONV_WIDTH, tf), lambda i, f: (0, f)),
                  pl.BlockSpec((1, tf), lambda i, f: (0, f)),
                  pl.BlockSpec((tf, D_MODEL), lambda i, f: (f, 0)),
                  pl.BlockSpec((1, D_MODEL), full),
                  pl.BlockSpec((nsb, CONV_WIDTH - 1, tf), lambda i, f: (i // seq_blocks, 0, f)),
                  pl.BlockSpec((None, tm, PLE_DIM), lambda i, f: (p_layer, i, 0)), pl.BlockSpec((1, D_MODEL), full),
                  pl.BlockSpec((D_MODEL, D_MODEL), full), pl.BlockSpec((PLE_DIM, D_MODEL), full),
                  pl.BlockSpec((1, D_MODEL), full)],
        out_specs=[pl.BlockSpec((tm, D_MODEL), row),
                   pl.BlockSpec((nf, nseq, CONV_WIDTH - 1, tf), lambda i, f: (0, 0, 0, 0))],
        out_shape=[jax.ShapeDtypeStruct((n, D_MODEL), F32),
                   jax.ShapeDtypeStruct((nf, nseq, CONV_WIDTH - 1, tf), F32)],
        scratch_shapes=[pltpu.VMEM((tm, D_MODEL), F32), pltpu.VMEM((tm, D_MODEL), BF16),
                        pltpu.VMEM((tm, D_MODEL), F32), pltpu.VMEM((nf, 8, tf), F32)],
        compiler_params=_cparams(("arbitrary", "arbitrary")), name="ffn",
    )(oa, ob, oc, h, lw["w_out"], lw["g_mix_post"],
      lw["g_ffn_pre"], lw["w_up"], lw["w_up"], lw["conv_w"], lw["conv_b"], lw["w_down"], lw["g_ffn_post"], left,
      p_all, lw["g_ple_pre"], lw["w_ple_gate"], lw["w_ple_proj"], lw["g_ple_post"])
    return out, state.transpose(1, 2, 0, 3).reshape(nseq, CONV_WIDTH - 1, D_FF)


def _swap_halves(w):
    half = MLA_ROPE_DIM // 2
    return jnp.concatenate([w[..., half:], w[..., :half]], axis=-1)


def _pack_layer(w_in, b_forget, mla_q_norm, w_mla_uq, mla_kv_norm, w_mla_uk, w_mla_uv, lams, diff_subln, w_out,
                norm_mix_pre, norm_mix_post, norm_ffn_pre, norm_ffn_post, norm_ple_pre, norm_ple_post,
                w_ffn_up, ffn_conv_w, ffn_conv_b, w_ffn_down, w_ple_gate, w_ple_proj):
    zeros = lambda r, c: jnp.zeros((r, c), F32)
    w_kr = w_in[:, OFF_MLA_KR:OFF_DIFF_Q]
    w_all = jnp.concatenate([
        w_in[:, OFF_FOX_Q:OFF_FOX_F], w_in[:, OFF_DIFF_Q:IN_WIDTH],
        w_in[:, OFF_MLA_CQ:OFF_MLA_CKV], w_in[:, OFF_MLA_CKV:OFF_MLA_KR],
        w_kr, w_kr, zeros(D_MODEL, LANE - 2 * MLA_ROPE_DIM),
        _swap_halves(w_kr), _swap_halves(w_kr), zeros(D_MODEL, LANE - 2 * MLA_ROPE_DIM),
        w_in[:, OFF_FOX_F:OFF_MLA_CQ], zeros(D_MODEL, LANE - N_FOX_HEADS)], axis=1).astype(BF16)
    wq = w_mla_uq.reshape(MLA_Q_RANK, N_MLA_HEADS, MLA_NOPE_DIM + MLA_ROPE_DIM)
    plain, swapped = [], []
    pad = zeros(MLA_Q_RANK, MLA_PAIR_W - 2 * (MLA_NOPE_DIM + MLA_ROPE_DIM))
    for p in range(N_MLA_PAIRS):
        a, b = 2 * p, 2 * p + 1
        plain += [wq[:, a, :MLA_NOPE_DIM], wq[:, b, :MLA_NOPE_DIM], wq[:, a, MLA_NOPE_DIM:], wq[:, b, MLA_NOPE_DIM:], pad]
        swapped += [zeros(MLA_Q_RANK, 2 * MLA_NOPE_DIM), _swap_halves(wq[:, a, MLA_NOPE_DIM:]),
                    _swap_halves(wq[:, b, MLA_NOPE_DIM:]), pad]
    return dict(
        w_all=w_all,
        b_f=jnp.pad(b_forget, (0, LANE - N_FOX_HEADS)).reshape(1, LANE),
        g_q=mla_q_norm.reshape(1, -1), g_kv=mla_kv_norm.reshape(1, -1),
        w_uq2=jnp.concatenate(plain + swapped, axis=1).astype(BF16),
        w_ukv=jnp.concatenate([w_mla_uk, w_mla_uv], axis=1).astype(BF16),
        lams=lams, subln=jnp.tile(diff_subln, 2).reshape(1, LANE),
        w_out=w_out.astype(BF16),
        g_mix_pre=norm_mix_pre.reshape(1, -1), g_mix_post=norm_mix_post.reshape(1, -1),
        g_ffn_pre=norm_ffn_pre.reshape(1, -1), g_ffn_post=norm_ffn_post.reshape(1, -1),
        g_ple_pre=norm_ple_pre.reshape(1, -1), g_ple_post=norm_ple_post.reshape(1, -1),
        w_up=w_ffn_up.astype(BF16), conv_w=ffn_conv_w, conv_b=ffn_conv_b.reshape(1, -1),
        w_down=w_ffn_down.astype(BF16), w_ple_gate=w_ple_gate.astype(BF16), w_ple_proj=w_ple_proj.astype(BF16))


def _rope_tables(pos):
    half = MLA_ROPE_DIM // 2
    inv_freq = ROPE_THETA ** (-jnp.arange(half, dtype=F32) / half)
    ang = pos.astype(F32)[:, None] * inv_freq[None, :]
    cos, sin = jnp.cos(ang), jnp.sin(ang)
    t = pos.shape[0]
    pad = jnp.zeros((t, MLA_PAIR_W - LANE - 2 * MLA_ROPE_DIM), F32)
    cos_t = jnp.concatenate([jnp.ones((t, LANE), F32), cos, cos, cos, cos, pad], axis=1)
    sin_t = jnp.concatenate([jnp.zeros((t, LANE), F32), -sin, sin, -sin, sin, pad], axis=1)
    return cos_t, sin_t


def _alibi_slopes():
    s = 2.0 ** (-8.0 * np.arange(1, N_DIFF_HEADS + 1) / N_DIFF_HEADS)
    return jnp.asarray(np.broadcast_to(s.reshape(N_DIFF_HEADS // 2, 2, 1), (N_DIFF_HEADS // 2, 2, LANE)), dtype=F32)


def _tile_rows(n, pref):
    t = min(n, pref)
    assert n % t == 0
    return t


def _layer(h, p, lw, cos, sin, cache, conv_left, lam_init, prev_rows, *, batch, seq, past, tq, depth):
    n = h.shape[0]
    tm = _tile_rows(n, 512)
    layer = p[1]
    emit_vt = cache is None and tm == tq
    outs = _proj_call(h, lw, cos, sin, tm, layer, depth, prev_rows, seq if emit_vt else None)
    rows = tuple(outs[:N_ROW_OUTPUTS])
    fox_bf, diff_bf, mla_q, mla_k, mla_v, logf = outs[N_ROW_OUTPUTS:N_ROW_OUTPUTS + 6]
    b3 = lambda a: a.reshape(batch, seq, a.shape[-1])
    diff_extras = [_alibi_slopes(), lw["lams"], lw["subln"]]
    diff_specs = [pl.BlockSpec((1, 2, LANE), lambda bb, hp, i: (hp, 0, 0)),
                  pl.BlockSpec((4, DIFF_QK_DIM), lambda bb, hp, i: (0, 0)),
                  pl.BlockSpec((1, LANE), lambda bb, hp, i: (0, 0))]
    fb, db = b3(fox_bf), b3(diff_bf)
    if cache is None:
        common = dict(tq=tq, tkd=tq, qoff=0, blocked=True)
        if emit_vt:
            vt_fox, vt_mla, vt_diff = outs[N_ROW_OUTPUTS + 6:]
        else:
            vt_fox, vt_mla, vt_diff = (_values_t(fb[..., 2 * FOX_WIDTH:], tq), _values_t(b3(mla_v), tq),
                                       _values_t(db[..., 2 * DIFF_WIDTH:], tq))
        o_a = _attn_call(_fox_kernel, "fox_attn", fb, 0, fb, 3, _forget_bias_call(b3(logf)), vt_fox, [], [],
                         npairs=N_FOX_HEADS // 2, qw=LANE, chains=2, **common)
        o_b = _attn_call(_mla_kernel, "mla_attn", b3(mla_q), 0, b3(mla_k), 0, None, vt_mla, [], [],
                         npairs=N_MLA_PAIRS, qw=MLA_PAIR_W, chains=2, **common)
        o_c = _attn_call(_diff_kernel, "diff_attn", db, 0, db, 2, _alibi_bias_block(seq), vt_diff,
                         diff_extras, diff_specs,
                         npairs=N_DIFF_HEADS // 2, qw=LANE, chains=4, lam_init=lam_init, **common)
    else:
        c_fox_k, c_fox_v, c_logf, c_ckv, c_krope, c_diff_k, c_diff_v = cache
        tk = -(-(past + seq) // LANE) * LANE
        padk = lambda a: jnp.pad(a, ((0, 0), (0, tk - past - seq), (0, 0)))
        cat = lambda old, new: padk(jnp.concatenate([old.astype(new.dtype), new], axis=1))
        flat = lambda a: a.reshape(batch, past, -1)
        padq = lambda a: jnp.pad(a, ((0, 0), (0, tq - seq), (0, 0)))
        kf = cat(flat(c_fox_k), fb[..., FOX_WIDTH:2 * FOX_WIDTH])
        vf = cat(flat(c_fox_v), fb[..., 2 * FOX_WIDTH:])
        kd = cat(flat(c_diff_k), db[..., DIFF_WIDTH:2 * DIFF_WIDTH])
        vd = cat(flat(c_diff_v), db[..., 2 * DIFF_WIDTH:])
        km_c, v_c = _kvup_call(c_ckv, c_krope, layer, lw["w_ukv"], _tile_rows(batch * past, 1024))
        km = cat(km_c.reshape(batch, past, MLA_QK_W), b3(mla_k))
        vm = cat(v_c.reshape(batch, past, MLA_WIDTH), b3(mla_v))
        common = dict(tq=tq, tkd=tk, qoff=past, blocked=False)
        o_a = _attn_call(_fox_kernel, "fox_attn_s", padq(fb[..., :FOX_WIDTH]), 0, kf, 0,
                         _forget_bias_call(cat(jnp.pad(c_logf, ((0, 0), (0, 0), (0, LANE - c_logf.shape[-1]))),
                                               b3(logf))), _values_t(vf, tk), [], [],
                         npairs=N_FOX_HEADS // 2, qw=LANE, chains=2, **common)
        o_b = _attn_call(_mla_kernel, "mla_attn_s", padq(b3(mla_q)), 0, km, 0, None, _values_t(vm, tk), [], [],
                         npairs=N_MLA_PAIRS, qw=MLA_PAIR_W, chains=2, **common)
        o_c = _attn_call(_diff_kernel, "diff_attn_s", padq(db[..., :DIFF_WIDTH]), 0, kd, 0, _alibi_bias_block(tk),
                         _values_t(vd, tk), diff_extras, diff_specs,
                         npairs=N_DIFF_HEADS // 2, qw=LANE, chains=4, lam_init=lam_init, **common)
        o_a, o_b, o_c = o_a[:, :seq], o_b[:, :seq], o_c[:, :seq]
    f2 = lambda a: a.reshape(n, a.shape[-1])
    h, conv_state = _ffn_call(f2(o_a), f2(o_b), f2(o_c), h, p, lw, conv_left, seq, tm, 2048)
    return h, rows, conv_state


def kernel(x_prompt, x_sample, cache_fox_k, cache_fox_v, cache_fox_logf, cache_mla_ckv, cache_mla_krope, cache_diff_k, cache_diff_v, state_ffn_conv, p_prompt, p_sample, w_in, b_forget, mla_q_norm, w_mla_uq, mla_kv_norm, w_mla_uk, w_mla_uv, diff_lambda_q1, diff_lambda_k1, diff_lambda_q2, diff_lambda_k2, diff_subln, w_out, norm_mix_pre, norm_mix_post, norm_ffn_pre, norm_ffn_post, norm_ple_pre, norm_ple_post, w_ffn_up, ffn_conv_w, ffn_conv_b, w_ffn_down, w_ple_gate, w_ple_proj):
    bp, sp, _ = x_prompt.shape
    bs, ts, _ = x_sample.shape
    depth, _, past = cache_fox_k.shape[:3]
    assert past % CHUNK == 0 and ts <= CHUNK and sp % LANE == 0
    tq = _tile_rows(sp, 512)
    tms = _tile_rows(bs * ts, 512)

    cos_p, sin_p = _rope_tables(jnp.arange(sp))
    cos_s, sin_s = _rope_tables(past + jnp.arange(ts))
    cos_s, sin_s = jnp.tile(cos_s, (tms // ts, 1)), jnp.tile(sin_s, (tms // ts, 1))

    hp = x_prompt.reshape(bp * sp, D_MODEL)
    hs = x_sample.reshape(bs * ts, D_MODEL)
    rows_p, rows_s, conv_p, conv_s = None, None, [], []
    for l in range(depth):
        lams = jnp.stack([diff_lambda_q1[l], diff_lambda_k1[l], diff_lambda_q2[l], diff_lambda_k2[l]], axis=0)
        lw = _pack_layer(w_in[l], b_forget[l], mla_q_norm[l], w_mla_uq[l], mla_kv_norm[l], w_mla_uk[l], w_mla_uv[l],
                         lams, diff_subln[l], w_out[l], norm_mix_pre[l], norm_mix_post[l], norm_ffn_pre[l],
                         norm_ffn_post[l], norm_ple_pre[l], norm_ple_post[l], w_ffn_up[l], ffn_conv_w[l],
                         ffn_conv_b[l], w_ffn_down[l], w_ple_gate[l], w_ple_proj[l])
        lam_init = 0.8 - 0.6 * math.exp(-0.3 * l)
        hp, rows_p, cp = _layer(hp, (p_prompt.reshape(depth, bp * sp, PLE_DIM), l), lw, cos_p, sin_p, None,
                                jnp.zeros((bp, CONV_WIDTH - 1, D_FF), F32), lam_init, rows_p,
                                batch=bp, seq=sp, past=0, tq=tq, depth=depth)
        cache_l = (cache_fox_k[l], cache_fox_v[l], cache_fox_logf[l],
                   cache_mla_ckv.reshape(depth, bs * past, MLA_KV_RANK),
                   cache_mla_krope.reshape(depth, bs * past, MLA_ROPE_DIM), cache_diff_k[l], cache_diff_v[l])
        hs, rows_s, cs = _layer(hs, (p_sample.reshape(depth, bs * ts, PLE_DIM), l), lw, cos_s, sin_s, cache_l,
                                state_ffn_conv[l], lam_init, rows_s,
                                batch=bs, seq=ts, past=past, tq=LANE, depth=depth)
        conv_p.append(cp)
        conv_s.append(cs)

    def stack_rows(rows, batch, seq):
        fk, fv, lf, ckv, kr, dk, dv = rows
        lead = (depth, batch, seq)
        return (fk.reshape(lead + (N_FOX_HEADS, HEAD_DIM)), fv.reshape(lead + (N_FOX_HEADS, HEAD_DIM)),
                lf.reshape(lead + (N_FOX_HEADS,)), ckv.reshape(lead + (MLA_KV_RANK,)),
                kr.reshape(lead + (MLA_ROPE_DIM,)), dk.reshape(lead + (N_DIFF_HEADS, 2 * DIFF_QK_DIM)),
                dv.reshape(lead + (N_DIFF_HEADS, DIFF_V_DIM)))

    out_p = stack_rows(rows_p, bp, sp)
    out_s = stack_rows(rows_s, bs, ts)
    return ((hp.reshape(bp, sp, D_MODEL), hs.reshape(bs, ts, D_MODEL)) + out_p + (jnp.stack(conv_p, axis=0),)
            + out_s + (jnp.stack(conv_s, axis=0),))
```

```python
import functools
import math

import jax
import jax.numpy as jnp
import numpy as np
from jax import lax
from jax.experimental import pallas as pl
from jax.experimental.pallas import tpu as pltpu

F32 = jnp.float32
BF16 = jnp.bfloat16

D_MODEL = 1024
HEAD_DIM = 64
N_FOX_HEADS = 6
N_MLA_HEADS = 6
N_DIFF_HEADS = 4
MLA_Q_RANK = 384
MLA_KV_RANK = 256
MLA_NOPE_DIM = 64
MLA_ROPE_DIM = 32
MLA_V_DIM = 64
DIFF_QK_DIM = 32
DIFF_V_DIM = 64
FOX_WIDTH = N_FOX_HEADS * HEAD_DIM
MLA_WIDTH = N_MLA_HEADS * MLA_V_DIM
DIFF_WIDTH = N_DIFF_HEADS * DIFF_V_DIM
D_FF = 4 * D_MODEL
CONV_WIDTH = 3
PLE_DIM = 256
CHUNK = 64
ROPE_THETA = 10000.0
RMS_EPS = 1e-6
NEG_INF = -1e30
LOG2E = math.log2(math.e)

OFF_FOX_Q = 0
OFF_FOX_F = 3 * FOX_WIDTH
OFF_MLA_CQ = OFF_FOX_F + N_FOX_HEADS
OFF_MLA_CKV = OFF_MLA_CQ + MLA_Q_RANK
OFF_MLA_KR = OFF_MLA_CKV + MLA_KV_RANK
OFF_DIFF_Q = OFF_MLA_KR + MLA_ROPE_DIM
IN_WIDTH = OFF_DIFF_Q + 2 * N_DIFF_HEADS * 2 * DIFF_QK_DIM + DIFF_WIDTH

LANE = 128
V7X_VMEM_BYTES = 64 * 1024 * 1024
VMEM_LIMIT = (V7X_VMEM_BYTES * 7) // 8

C_FOX = 0
C_DIFF = C_FOX + 3 * FOX_WIDTH
C_CQ = C_DIFF + 3 * DIFF_WIDTH
C_CKV = C_CQ + MLA_Q_RANK
C_KR = C_CKV + MLA_KV_RANK
C_F = C_KR + 2 * LANE
W_ALL = C_F + LANE
N_MLA_PAIRS = N_MLA_HEADS // 2
MLA_PAIR_W = 2 * LANE
MLA_QK_W = N_MLA_PAIRS * MLA_PAIR_W

FOX_QSCALE = HEAD_DIM ** -0.5 * LOG2E
MLA_QSCALE = (MLA_NOPE_DIM + MLA_ROPE_DIM) ** -0.5 * LOG2E
DIFF_QSCALE = DIFF_QK_DIM ** -0.5 * LOG2E


def _cparams(sem):
    return pltpu.CompilerParams(dimension_semantics=sem, vmem_limit_bytes=VMEM_LIMIT)


def _rms(x, g):
    return x * lax.rsqrt(jnp.mean(x * x, axis=-1, keepdims=True) + RMS_EPS) * g


def _dot(a, b):
    return jnp.dot(a, b, preferred_element_type=F32)


def _dot_nt(a, b):
    return lax.dot_general(a, b, (((1,), (1,)), ((), ())), preferred_element_type=F32)


def _store_values_t(vt_ref, v):
    dv = LANE // 2
    ones = jnp.ones((ONES_ROWS, v.shape[0]), BF16)
    for p in range(v.shape[1] // LANE):
        t = v[:, p * LANE:(p + 1) * LANE].T.astype(BF16)
        for e in range(2):
            r0 = e * (dv + ONES_ROWS)
            vt_ref[0, p, 0, r0:r0 + dv, :] = t[e * dv:(e + 1) * dv, :]
            vt_ref[0, p, 0, r0 + dv:r0 + dv + ONES_ROWS, :] = ones


def _proj_kernel(h_ref, gpre_ref, w_ref, bf_ref, gq_ref, gkv_ref, wuq_ref, wukv_ref, cos_ref, sin_ref, *rest,
                 n_alias, emit_vt):
    (foxk_ref, foxv_ref, logf_ref, ckv_ref, krope_ref, diffk_ref, diffv_ref,
     foxbf_ref, diffbf_ref, mlaq_ref, mlak_ref, mlav_ref) = rest[n_alias:n_alias + 12]
    xn = _rms(h_ref[...], gpre_ref[...]).astype(BF16)
    cos = cos_ref[...]
    sin = sin_ref[...]

    z = _dot(xn, w_ref[:, C_FOX:C_FOX + 3 * FOX_WIDTH])
    foxbf_ref[:, :FOX_WIDTH] = (z[:, :FOX_WIDTH] * FOX_QSCALE).astype(BF16)
    foxbf_ref[:, FOX_WIDTH:] = z[:, FOX_WIDTH:].astype(BF16)
    foxk_ref[...] = z[:, FOX_WIDTH:2 * FOX_WIDTH]
    foxv_ref[...] = z[:, 2 * FOX_WIDTH:3 * FOX_WIDTH]
    if emit_vt:
        _store_values_t(rest[n_alias + 13], z[:, 2 * FOX_WIDTH:3 * FOX_WIDTH])

    z = _dot(xn, w_ref[:, C_DIFF:C_DIFF + 3 * DIFF_WIDTH])
    diffbf_ref[:, :DIFF_WIDTH] = (z[:, :DIFF_WIDTH] * DIFF_QSCALE).astype(BF16)
    diffbf_ref[:, DIFF_WIDTH:] = z[:, DIFF_WIDTH:].astype(BF16)
    diffk_ref[...] = z[:, DIFF_WIDTH:2 * DIFF_WIDTH]
    diffv_ref[...] = z[:, 2 * DIFF_WIDTH:3 * DIFF_WIDTH]
    if emit_vt:
        _store_values_t(rest[n_alias + 15], z[:, 2 * DIFF_WIDTH:3 * DIFF_WIDTH])

    cq = _rms(_dot(xn, w_ref[:, C_CQ:C_CQ + MLA_Q_RANK]), gq_ref[...]).astype(BF16)
    q2 = _dot(cq, wuq_ref[...])
    for p in range(N_MLA_PAIRS):
        lo = p * MLA_PAIR_W
        roped = q2[:, lo:lo + MLA_PAIR_W] * cos + q2[:, MLA_QK_W + lo:MLA_QK_W + lo + MLA_PAIR_W] * sin
        mlaq_ref[:, lo:lo + MLA_PAIR_W] = (roped * MLA_QSCALE).astype(BF16)

    ckv = _rms(_dot(xn, w_ref[:, C_CKV:C_CKV + MLA_KV_RANK]), gkv_ref[...])
    ckv_ref[...] = ckv
    kv = _dot(ckv.astype(BF16), wukv_ref[...])
    mlav_ref[...] = kv[:, MLA_WIDTH:2 * MLA_WIDTH].astype(BF16)
    if emit_vt:
        _store_values_t(rest[n_alias + 14], kv[:, MLA_WIDTH:2 * MLA_WIDTH])
    zkr =_dot(xn, w_ref[:, C_KR:C_KR + 2 * LANE])
    krp = zkr[:, :LANE] * cos[:, LANE:] + zkr[:, LANE:] * sin[:, LANE:]
    krope_ref[...] = krp[:, :MLA_ROPE_DIM]
    krp16 = krp.astype(BF16)
    for p in range(N_MLA_PAIRS):
        lo = p * MLA_PAIR_W
        mlak_ref[:, lo:lo + LANE] = kv[:, p * LANE:(p + 1) * LANE].astype(BF16)
        mlak_ref[:, lo + LANE:lo + 2 * LANE] = krp16

    zf = _dot(xn, w_ref[:, C_F:C_F + LANE]) + bf_ref[...]
    logf = -(jnp.maximum(-zf, 0.0) + jnp.log1p(jnp.exp(-jnp.abs(zf))))
    logf_ref[...] = logf[:, :N_FOX_HEADS]
    lane = lax.broadcasted_iota(jnp.int32, (1, LANE), 1)
    rest[n_alias + 12][...] = jnp.where(lane < N_FOX_HEADS, logf, 0.0)


N_ROW_OUTPUTS = 7


def _proj_call(h, lw, cos, sin, tm, layer, depth, prev_rows, vt_seq):
    n = h.shape[0]
    ntab = cos.shape[0] // tm
    row = lambda i: (i, 0)
    full = lambda i: (0, 0)
    tab = lambda i: (i % ntab, 0)
    in_specs = [
        pl.BlockSpec((tm, D_MODEL), row),
        pl.BlockSpec((1, D_MODEL), full),
        pl.BlockSpec((D_MODEL, W_ALL), full),
        pl.BlockSpec((1, LANE), full),
        pl.BlockSpec((1, MLA_Q_RANK), full),
        pl.BlockSpec((1, MLA_KV_RANK), full),
        pl.BlockSpec((MLA_Q_RANK, 2 * MLA_QK_W), full),
        pl.BlockSpec((MLA_KV_RANK, 2 * MLA_WIDTH), full),
        pl.BlockSpec((tm, MLA_PAIR_W), tab),
        pl.BlockSpec((tm, MLA_PAIR_W), tab),
    ]
    widths = [(FOX_WIDTH, F32), (FOX_WIDTH, F32), (N_FOX_HEADS, F32), (MLA_KV_RANK, F32), (MLA_ROPE_DIM, F32),
              (DIFF_WIDTH, F32), (DIFF_WIDTH, F32),
              (3 * FOX_WIDTH, BF16), (3 * DIFF_WIDTH, BF16), (MLA_QK_W, BF16), (MLA_QK_W, BF16), (MLA_WIDTH, BF16),
              (LANE, F32)]
    out_shape = ([jax.ShapeDtypeStruct((depth, n, w), dt) for w, dt in widths[:N_ROW_OUTPUTS]]
                 + [jax.ShapeDtypeStruct((n, w), dt) for w, dt in widths[N_ROW_OUTPUTS:]])
    out_specs = ([pl.BlockSpec((None, tm, w), lambda i: (layer, i, 0)) for w, _ in widths[:N_ROW_OUTPUTS]]
                 + [pl.BlockSpec((tm, w), row) for w, _ in widths[N_ROW_OUTPUTS:]])
    if vt_seq is not None:
        nblk = vt_seq // tm
        for width in (FOX_WIDTH, MLA_WIDTH, DIFF_WIDTH):
            shape = (n // vt_seq, width // LANE, nblk, 2 * (LANE // 2 + ONES_ROWS), tm)
            out_shape.append(jax.ShapeDtypeStruct(shape, BF16))
            out_specs.append(pl.BlockSpec((1,) + shape[1:2] + (1,) + shape[3:],
                                          lambda i: (i // nblk, 0, i % nblk, 0, 0)))
    args = [h, lw["g_mix_pre"], lw["w_all"], lw["b_f"], lw["g_q"], lw["g_kv"], lw["w_uq2"], lw["w_ukv"], cos, sin]
    aliases = {}
    if prev_rows is not None:
        aliases = {len(args) + k: k for k in range(N_ROW_OUTPUTS)}
        in_specs = in_specs + [pl.BlockSpec(memory_space=pl.ANY)] * N_ROW_OUTPUTS
        args = args + list(prev_rows)
    return pl.pallas_call(
        functools.partial(_proj_kernel, n_alias=len(aliases), emit_vt=vt_seq is not None), grid=(n // tm,),
        in_specs=in_specs,
        out_specs=out_specs, out_shape=out_shape, input_output_aliases=aliases,
        compiler_params=_cparams(("parallel",)), name="proj",
    )(*args)


def _kvup_kernel(ckv_ref, kr_ref, wukv_ref, place_ref, k_ref, v_ref):
    kv = _dot(ckv_ref[...].astype(BF16), wukv_ref[...])
    v_ref[...] = kv[:, MLA_WIDTH:].astype(BF16)
    krp = _dot(kr_ref[...].astype(BF16), place_ref[...]).astype(BF16)
    for p in range(N_MLA_PAIRS):
        lo = p * MLA_PAIR_W
        k_ref[:, lo:lo + LANE] = kv[:, p * LANE:(p + 1) * LANE].astype(BF16)
        k_ref[:, lo + LANE:lo + 2 * LANE] = krp


def _kvup_call(ckv_all, krope_all, layer, w_ukv, tm):
    n = ckv_all.shape[1]
    row = lambda i: (i, 0)
    lrow = lambda i: (layer, i, 0)
    full = lambda i: (0, 0)
    eye = jnp.eye(MLA_ROPE_DIM, dtype=BF16)
    place = jnp.concatenate([eye, eye, jnp.zeros((MLA_ROPE_DIM, LANE - 2 * MLA_ROPE_DIM), BF16)], axis=1)
    return pl.pallas_call(
        _kvup_kernel, grid=(n // tm,),
        in_specs=[pl.BlockSpec((None, tm, MLA_KV_RANK), lrow), pl.BlockSpec((None, tm, MLA_ROPE_DIM), lrow),
                  pl.BlockSpec((MLA_KV_RANK, 2 * MLA_WIDTH), full), pl.BlockSpec((MLA_ROPE_DIM, LANE), full)],
        out_specs=[pl.BlockSpec((tm, MLA_QK_W), row), pl.BlockSpec((tm, MLA_WIDTH), row)],
        out_shape=[jax.ShapeDtypeStruct((n, MLA_QK_W), BF16), jax.ShapeDtypeStruct((n, MLA_WIDTH), BF16)],
        compiler_params=_cparams(("parallel",)), name="kvup",
    )(ckv_all, krope_all, w_ukv, place)


AUG_STRIDE = 8


def _keep_bf16_bits(x):
    bits = lax.bitcast_convert_type(x, jnp.uint32) & jnp.uint32(0xFFFF0000)
    return lax.bitcast_convert_type(bits, F32)


def _split3(x):
    x1 = _keep_bf16_bits(x)
    r = x - x1
    x2 = _keep_bf16_bits(r)
    return x1, x2, r - x2


def _place3(x1, x2, x3):
    lane = lax.broadcasted_iota(jnp.int32, (1, LANE), 1)
    return jnp.where(lane < AUG_STRIDE, x1,
                     jnp.where(lane < 2 * AUG_STRIDE, pltpu.roll(x2, AUG_STRIDE, 1),
                               pltpu.roll(x3, 2 * AUG_STRIDE, 1))).astype(BF16)


def _aug_selector(h, rows):
    lane = lax.broadcasted_iota(jnp.int32, (rows, LANE), 1)
    hit = (lane == h) | (lane == AUG_STRIDE + h) | (lane == 2 * AUG_STRIDE + h)
    return jnp.where(hit, 1.0, 0.0).astype(BF16)


def _forget_bias_kernel(x_ref, tri_ref, o_ref, carry_sc, *, tb):
    @pl.when(pl.program_id(1) == 0)
    def _():
        carry_sc[...] = jnp.zeros_like(carry_sc)

    terms = jnp.concatenate(_split3(x_ref[0]), axis=1).astype(BF16)
    sums = _dot(tri_ref[...], terms)
    cs = carry_sc[0:1, :] + sums[:, :LANE] + sums[:, LANE:2 * LANE] + sums[:, 2 * LANE:]
    carry_sc[...] = jnp.broadcast_to(cs[tb - 1:tb, :], carry_sc.shape)
    o_ref[0] = _place3(*_split3(cs * (-LOG2E)))


def _forget_bias_call(logf):
    b, t, _ = logf.shape
    tb = max(d for d in range(LANE, 4 * LANE + 1, LANE) if t % d == 0)
    x = logf
    tri = jnp.tril(jnp.ones((tb, tb), BF16))
    blk = pl.BlockSpec((1, tb, LANE), lambda i, j: (i, j, 0))
    return pl.pallas_call(
        functools.partial(_forget_bias_kernel, tb=tb), grid=(b, t // tb),
        in_specs=[blk, pl.BlockSpec((tb, tb), lambda i, j: (0, 0))], out_specs=blk,
        out_shape=jax.ShapeDtypeStruct((b, t, LANE), BF16),
        scratch_shapes=[pltpu.VMEM((8, LANE), F32)],
        compiler_params=_cparams(("parallel", "arbitrary")), name="forget_bias",
    )(x, tri)


def _alibi_bias_block(t_k):
    slopes = 2.0 ** (-8.0 * np.arange(1, N_DIFF_HEADS + 1) / N_DIFF_HEADS)
    b = jnp.zeros((t_k, LANE), F32).at[:, :N_DIFF_HEADS].set(
        jnp.arange(t_k, dtype=F32)[:, None] * jnp.asarray(slopes * LOG2E, F32)[None, :])
    x1, x2, x3 = _split3(b)
    lane = jnp.arange(LANE)[None, :]
    placed = jnp.where(lane < AUG_STRIDE, x1, jnp.where(lane < 2 * AUG_STRIDE, jnp.roll(x2, AUG_STRIDE, 1),
                                                        jnp.roll(x3, 2 * AUG_STRIDE, 1)))
    return placed.astype(BF16)[None]


ROW_CHUNK = CHUNK
STRIP = 4 * LANE
ONES_ROWS = 16


def _masked_chunks(s_ref, c, mask, tk, tq):
    chunks = [(r0, "all" if mask is None else mask[0](r0, 0, tq)) for r0 in range(0, tk, ROW_CHUNK)]

    def logits(r0, vis):
        x = s_ref[c, r0:r0 + ROW_CHUNK, :]
        return x if vis == "all" else mask[1](r0, 0, x)
    return chunks, logits


def _softmax_max(s_ref, m_ref, c, mask, tk, tq):
    chunks, logits = _masked_chunks(s_ref, c, mask, tk, tq)
    mrun = jnp.full((8, tq), NEG_INF, F32)
    for r0, vis in chunks:
        if vis != "none":
            mrun = jnp.maximum(mrun, jnp.max(logits(r0, vis).reshape(ROW_CHUNK // 8, 8, tq), axis=0))
    m_old = m_ref[c]
    return m_old, jnp.maximum(m_old, jnp.max(mrun, axis=0, keepdims=True))


def _softmax_exp(s_ref, p_ref, m_ref, alpha_ref, c, mask, tk, tq, m_old, m_new):
    chunks, logits = _masked_chunks(s_ref, c, mask, tk, tq)
    for r0, vis in chunks:
        if vis == "none":
            p_ref[c, r0:r0 + ROW_CHUNK, :] = jnp.zeros((ROW_CHUNK, tq), BF16)
        else:
            p_ref[c, r0:r0 + ROW_CHUNK, :] = jnp.exp2(logits(r0, vis) - m_new).astype(BF16)
    m_ref[c] = m_new
    alpha_ref[c] = jnp.exp2(m_old - m_new)


def _apply_values(p_ref, alpha_ref, acc_ref, c, vt):
    acc_ref[c] = alpha_ref[c] * acc_ref[c] + _dot(vt, p_ref[c])


def _causal_mask(qoff):
    def visibility(r0, c0, sw):
        if r0 + ROW_CHUNK - 1 <= c0 + qoff:
            return "all"
        return "none" if r0 > c0 + sw - 1 + qoff else "some"

    def apply(r0, c0, x):
        kk, qq = _key_query_iota(*x.shape)
        return jnp.where(kk - qq <= c0 + qoff - r0, x, NEG_INF)
    return visibility, apply


def _chunk_visibility(r0, c0, sw, qoff):
    if r0 // CHUNK <= (c0 + qoff) // CHUNK:
        return "all"
    return "none" if r0 // CHUNK > (c0 + sw - 1 + qoff) // CHUNK else "some"


def _chunk_mask(r0, c0, sw, qoff):
    qq = lax.broadcasted_iota(jnp.int32, (1, sw), 1)
    return r0 // CHUNK <= (qq + (c0 + qoff)) // CHUNK


def _init_states(m_ref, acc_ref):
    m_ref[...] = jnp.full(m_ref.shape, NEG_INF, F32)
    acc_ref[...] = jnp.zeros(acc_ref.shape, F32)


def _normalized(acc_ref, c, dv):
    return acc_ref[c, :dv, :] / acc_ref[c, dv:dv + 1, :]


def _key_query_iota(tk, tq):
    return (lax.broadcasted_iota(jnp.int32, (tk, tq), 0), lax.broadcasted_iota(jnp.int32, (tk, tq), 1))


def _store_pair(o_ref, o0, o1, row_scale=None):
    out = jnp.concatenate([o0, o1], axis=0).T
    if row_scale is not None:
        out = out * row_scale
    o_ref[0] = out.astype(BF16)


def _block_start(j, size):
    return j * size if isinstance(j, int) else pl.multiple_of(j * size, size)


N_BUF = 3


def _attend(i, blocked, chains, qk_fn, max_fn, exp_fn, pv_fn):
    def step(qk, pv, buf, diag):
        if qk is not None:
            all_chains(qk_fn, *qk)
        if pv is not None:
            all_chains(pv_fn, *pv)
        for c in range(chains):
            exp_fn(buf, c, diag, *max_fn(buf, c, diag))

    def all_chains(fn, *args):
        for c in range(chains):
            fn(*args, c)

    if not blocked:
        all_chains(qk_fn, 0, 0)
        step(None, None, 0, True)
        all_chains(pv_fn, 0, 0)
        return
    last = N_BUF - 1
    lead = i % N_BUF

    def single(j, carry):
        all_chains(qk_fn, last, j)
        step(None, None, last, False)
        all_chains(pv_fn, last, j)
        return carry

    lax.fori_loop(0, lead, single, 0)
    all_chains(qk_fn, 0, lead)
    all_chains(qk_fn, 1, jnp.minimum(lead + 1, i))

    def rotate(t, carry):
        b0 = lead + N_BUF * t
        for u in range(N_BUF):
            step(((u + 2) % N_BUF, jnp.minimum(b0 + u + 2, i)), (u - 1, b0 + u - 1) if u > 0 else None, u, False)
        all_chains(pv_fn, last, b0 + last)
        return carry

    lax.fori_loop(0, (i - lead) // N_BUF, rotate, 0)
    step(None, None, 0, True)
    all_chains(pv_fn, 0, i)


def _vt_rows(vt_ref, j, e, dv):
    return vt_ref[0, 0, j, e * (dv + ONES_ROWS):(e + 1) * (dv + ONES_ROWS), :]


def _split_scratch(scratch):
    nbuf = (len(scratch) - 3) // 2
    return (scratch[:nbuf], scratch[nbuf:2 * nbuf]) + tuple(scratch[2 * nbuf:])


def _chain_fns(scratch, vt_ref, mask_of, dv, head_of, tkd, tq):
    s_sc, p_sc, m_sc, alpha_sc, acc_sc = _split_scratch(scratch)

    def max_fn(buf, c, diag):
        return _softmax_max(s_sc[buf], m_sc, c, mask_of(c) if diag else None, tkd, tq)

    def exp_fn(buf, c, diag, m_old, m_new):
        _softmax_exp(s_sc[buf], p_sc[buf], m_sc, alpha_sc, c, mask_of(c) if diag else None, tkd, tq, m_old, m_new)

    def pv_fn(buf, j, c):
        _apply_values(p_sc[buf], alpha_sc, acc_sc, c, _vt_rows(vt_ref, j, head_of(c), dv))
    return max_fn, exp_fn, pv_fn


def _fox_kernel(q_ref, k_ref, aug_ref, vt_ref, o_ref, *scratch, tq, tkd, qoff, blocked):
    s_sc, p_sc, m_sc, alpha_sc, acc_sc = _split_scratch(scratch)
    hp, i = pl.program_id(1), pl.program_id(2)
    _init_states(m_sc, acc_sc)
    q = q_ref[0]
    lane = lax.broadcasted_iota(jnp.int32, (1, LANE), 1)
    qs = [jnp.concatenate([jnp.where((lane >= e * HEAD_DIM) & (lane < (e + 1) * HEAD_DIM), q, jnp.zeros_like(q)),
                           _aug_selector(2 * hp + e, tq)], axis=1) for e in range(2)]

    def qk_fn(buf, j, e):
        rows = pl.ds(_block_start(j, tq), tkd)
        k = jnp.concatenate([k_ref[0, rows, :], aug_ref[0, rows, :]], axis=1)
        s_sc[buf][e] = _dot_nt(k, qs[e])

    _attend(i, blocked, 2, qk_fn, *_chain_fns(scratch, vt_ref, lambda e: _causal_mask(qoff), HEAD_DIM,
                                              lambda e: e, tkd, tq))
    _store_pair(o_ref, *(_normalized(acc_sc, e, HEAD_DIM) for e in range(2)))


def _mla_kernel(q_ref, k_ref, vt_ref, o_ref, *scratch, tq, tkd, qoff, blocked):
    s_sc, p_sc, m_sc, alpha_sc, acc_sc = _split_scratch(scratch)
    i = pl.program_id(2)
    _init_states(m_sc, acc_sc)
    q = q_ref[0]
    lane2 = lax.broadcasted_iota(jnp.int32, (1, MLA_PAIR_W), 1)
    sel = []
    for e in range(2):
        nope = (lane2 >= e * MLA_NOPE_DIM) & (lane2 < (e + 1) * MLA_NOPE_DIM)
        rope = (lane2 >= LANE + e * MLA_ROPE_DIM) & (lane2 < LANE + (e + 1) * MLA_ROPE_DIM)
        sel.append(jnp.where(nope | rope, q, jnp.zeros_like(q)))

    def qk_fn(buf, j, e):
        s_sc[buf][e] = _dot_nt(k_ref[0, pl.ds(_block_start(j, tq), tkd), :], sel[e])

    mask = (lambda r0, c0, sw: _chunk_visibility(r0, c0, sw, qoff),
            lambda r0, c0, x: jnp.where(_chunk_mask(r0, c0, x.shape[1], qoff), x, NEG_INF))
    _attend(i, blocked, 2, qk_fn, *_chain_fns(scratch, vt_ref, lambda e: mask, MLA_V_DIM, lambda e: e, tkd, tq))
    _store_pair(o_ref, *(_normalized(acc_sc, e, MLA_V_DIM) for e in range(2)))


def _diff_kernel(q_ref, k_ref, aug_ref, vt_ref, slope_ref, lam_ref, subln_ref, o_ref, *scratch,
                 tq, tkd, qoff, blocked, lam_init):
    s_sc, p_sc, m_sc, alpha_sc, acc_sc = _split_scratch(scratch)
    hp, i = pl.program_id(1), pl.program_id(2)
    _init_states(m_sc, acc_sc)
    q = q_ref[0]
    lane = lax.broadcasted_iota(jnp.int32, (1, LANE), 1)
    sel = [jnp.concatenate([jnp.where((lane >= (2 * e + t) * DIFF_QK_DIM) & (lane < (2 * e + t + 1) * DIFF_QK_DIM),
                                      q, jnp.zeros_like(q)), _aug_selector(2 * hp + e, tq)], axis=1)
           for e in range(2) for t in range(2)]
    slopes = [slope_ref[0, e:e + 1, 0:1] * LOG2E for e in range(2)]

    def qk_fn(buf, j, c):
        rows = pl.ds(_block_start(j, tq), tkd)
        k = jnp.concatenate([k_ref[0, rows, :], aug_ref[0, rows, :]], axis=1)
        s_sc[buf][c] = _dot_nt(k, sel[c])

    def mask_of(c):
        def visibility(r0, c0, sw):
            vis = _chunk_visibility(r0, c0, sw, qoff)
            return "some" if vis == "all" and r0 + ROW_CHUNK - 1 > c0 + qoff else vis

        def apply(r0, c0, x):
            kk, qq = _key_query_iota(*x.shape)
            ahead = jnp.maximum(kk - qq + (r0 - c0 - qoff), 0).astype(F32)
            return jnp.where(_chunk_mask(r0, c0, x.shape[1], qoff), x - (2.0 * slopes[c // 2]) * ahead, NEG_INF)
        return visibility, apply

    _attend(i, blocked, 4, qk_fn, *_chain_fns(scratch, vt_ref, mask_of, DIFF_V_DIM, lambda c: c // 2, tkd, tq))
    res = [_normalized(acc_sc, c, DIFF_V_DIM) for c in range(4)]
    lq = lam_ref[...]
    lam = (jnp.exp(jnp.sum(lq[0:1] * lq[1:2], axis=-1, keepdims=True))
           - jnp.exp(jnp.sum(lq[2:3] * lq[3:4], axis=-1, keepdims=True)) + lam_init)
    outs = []
    for e in range(2):
        o = res[2 * e] - lam * res[2 * e + 1]
        outs.append(o * lax.rsqrt(jnp.mean(o * o, axis=0, keepdims=True) + RMS_EPS))
    _store_pair(o_ref, outs[0], outs[1], subln_ref[...] * (1.0 - lam_init))


def _attn_call(kernel, name, q, q_cb, k, k_cb, aug, vt, extras, extra_specs, *, npairs, qw, tq, tkd, qoff, blocked,
               chains, **kw):
    b, t_q = q.shape[0], q.shape[1]
    t_k = k.shape[1]
    acc_rows = LANE // 2 + ONES_ROWS
    nbuf = N_BUF if blocked else 1
    scratch = ([pltpu.VMEM((chains, tkd, tq), F32)] * nbuf + [pltpu.VMEM((chains, tkd, tq), BF16)] * nbuf
               + [pltpu.VMEM((chains, 1, tq), F32), pltpu.VMEM((chains, 1, tq), F32),
                  pltpu.VMEM((chains, acc_rows, tq), F32)])
    in_specs = [pl.BlockSpec((1, tq, qw), lambda bb, hp, i: (bb, i, q_cb + hp)),
                pl.BlockSpec((1, t_k, qw), lambda bb, hp, i: (bb, 0, k_cb + hp))]
    args = [q, k]
    if aug is not None:
        per_batch = aug.shape[0] > 1
        in_specs.append(pl.BlockSpec((1, t_k, LANE), lambda bb, hp, i: (bb if per_batch else 0, 0, 0)))
        args.append(aug)
    in_specs.append(pl.BlockSpec((1, 1) + vt.shape[2:], lambda bb, hp, i: (bb, hp, 0, 0, 0)))
    return pl.pallas_call(
        functools.partial(kernel, tq=tq, tkd=tkd, qoff=qoff, blocked=blocked, **kw),
        grid=(b, npairs, t_q // tq), in_specs=in_specs + extra_specs,
        out_specs=pl.BlockSpec((1, tq, LANE), lambda bb, hp, i: (bb, i, hp)),
        out_shape=jax.ShapeDtypeStruct((b, t_q, npairs * LANE), BF16), scratch_shapes=scratch,
        compiler_params=_cparams(("parallel", "parallel", "arbitrary")), name=name,
    )(*args, vt, *extras)


def _values_t(v, tkb):
    b, t, w = v.shape
    dv = LANE // 2
    vt = v.reshape(b, t // tkb, tkb, w // LANE, 2, dv).transpose(0, 3, 1, 4, 5, 2)
    ones = jnp.ones(vt.shape[:4] + (ONES_ROWS, tkb), v.dtype)
    return jnp.concatenate([vt, ones], axis=4).reshape(b, w // LANE, t // tkb, 2 * (dv + ONES_ROWS), tkb)


def _values_t_cached(cached, new, tk):
    b, p, hh, dv = cached.shape
    t = new.shape[1]
    vt = jnp.concatenate([jnp.transpose(cached, (0, 2, 3, 1)).astype(BF16),
                          jnp.transpose(new.reshape(b, t, hh, dv), (0, 2, 3, 1))], axis=-1)
    vt = jnp.pad(vt, ((0, 0), (0, 0), (0, 0), (0, tk - p - t)))
    ones = jnp.ones((b, hh, ONES_ROWS, tk), BF16)
    return jnp.concatenate([vt, ones], axis=2).reshape(b, hh // 2, 1, 2 * (dv + ONES_ROWS), tk)


FFN_HALVES = 4


def _ffn_kernel(oa_ref, ob_ref, oc_ref, h_ref, wout_ref, gmix_ref,
                gpre_ref, wg_ref, wv_ref, cw_ref, cb_ref, wd_ref, gpost_ref, left_ref,
                p_ref, gple_ref, wgate_ref, wproj_ref, gplepost_ref,
                o_ref, st_ref, h1_sc, xn_sc, acc_sc, carry_sc, *, tm, tf, nsb, seq_blocks):
    i = pl.program_id(0)
    f = pl.program_id(1)
    nf = pl.num_programs(1)
    tb = tm // nsb

    @pl.when(f == 0)
    def _():
        y = (_dot(oa_ref[...], wout_ref[0:FOX_WIDTH, :])
             + _dot(ob_ref[...], wout_ref[FOX_WIDTH:FOX_WIDTH + MLA_WIDTH, :])
             + _dot(oc_ref[...], wout_ref[FOX_WIDTH + MLA_WIDTH:, :]))
        h1 = h_ref[...] + _rms(y, gmix_ref[...])
        h1_sc[...] = h1
        xn_sc[...] = _rms(h1, gpre_ref[...]).astype(BF16)
        acc_sc[...] = jnp.zeros_like(acc_sc)

    xn = xn_sc[...]
    th = tf // FFN_HALVES
    halves = [slice(a * th, (a + 1) * th) for a in range(FFN_HALVES)]
    ups = [(_dot(xn, wg_ref[:, cols]), _dot(xn, wv_ref[:, cols])) for cols in halves]
    rin = lax.broadcasted_iota(jnp.int32, (tm, 1), 0) & (tb - 1)

    def spread(rows):
        return jnp.broadcast_to(rows, (nsb, tb, th)).reshape(tm, th)

    for cols, (gate, val) in zip(halves, ups):
        left = left_ref[:, :, cols]
        if seq_blocks > 1:
            left = jnp.where(i % seq_blocks == 0, left, carry_sc[f, 0:CONV_WIDTH - 1, cols][None])
            carry_sc[f, 0:CONV_WIDTH - 1, cols] = gate[tm - (CONV_WIDTH - 1):, :]
        st_ref[f, pl.ds((i // seq_blocks) * nsb, nsb), :, cols] = (
            gate.reshape(nsb, tb, th)[:, tb - (CONV_WIDTH - 1):, :])
        l0 = spread(left[:, 0:1, :])
        l1 = spread(left[:, 1:2, :])
        g1 = jnp.where(rin == 0, l1, pltpu.roll(gate, 1, 0))
        g2 = jnp.where(rin == 0, l0, jnp.where(rin == 1, l1, pltpu.roll(gate, 2, 0)))
        conv = cw_ref[0:1, cols] * g2 + cw_ref[1:2, cols] * g1 + cw_ref[2:3, cols] * gate + cb_ref[:, cols]
        gelu = 0.5 * conv * (1.0 + jnp.tanh(math.sqrt(2.0 / math.pi) * (conv + 0.044715 * (conv * conv * conv))))
        acc_sc[...] += _dot((gelu * val).astype(BF16), wd_ref[cols, :])

    @pl.when(f == nf - 1)
    def _():
        h2 = h1_sc[...] + _rms(acc_sc[...], gpost_ref[...])
        gate = jax.nn.sigmoid(_dot(_rms(h2, gple_ref[...]).astype(BF16), wgate_ref[...]))
        proj = _dot(p_ref[...].astype(BF16), wproj_ref[...])
        o_ref[...] = h2 + _rms(proj * gate, gplepost_ref[...])


def _ffn_call(oa, ob, oc, h, p, lw, left, seq_len, tm, tf):
    n = h.shape[0]
    p_all, p_layer = p
    nseq = left.shape[0]
    nf = D_FF // tf
    if seq_len >= tm:
        nsb, seq_blocks = 1, seq_len // tm
    else:
        nsb, seq_blocks = tm // seq_len, 1
    assert (tm // nsb) & (tm // nsb - 1) == 0
    row = lambda i, f: (i, 0)
    full = lambda i, f: (0, 0)
    out, state = pl.pallas_call(
        functools.partial(_ffn_kernel, tm=tm, tf=tf, nsb=nsb, seq_blocks=seq_blocks),
        grid=(n // tm, nf),
        in_specs=[pl.BlockSpec((tm, FOX_WIDTH), row), pl.BlockSpec((tm, MLA_WIDTH), row),
                  pl.BlockSpec((tm, DIFF_WIDTH), row), pl.BlockSpec((tm, D_MODEL), row),
                  pl.BlockSpec((D_MODEL, D_MODEL), full), pl.BlockSpec((1, D_MODEL), full),
                  pl.BlockSpec((1, D_MODEL), full),
                  pl.BlockSpec((D_MODEL, tf), lambda i, f: (0, f)),
                  pl.BlockSpec((D_MODEL, tf), lambda i, f: (0, nf + f)),
                  pl.BlockSpec((CONV_WIDTH, tf), lambda i, f: (0, f)),
                  pl.BlockSpec((1, tf), lambda i, f: (0, f)),
                  pl.BlockSpec((tf, D_MODEL), lambda i, f: (f, 0)),
                  pl.BlockSpec((1, D_MODEL), full),
                  pl.BlockSpec((nsb, CONV_WIDTH - 1, tf), lambda i, f: (i // seq_blocks, 0, f)),
                  pl.BlockSpec((None, tm, PLE_DIM), lambda i, f: (p_layer, i, 0)), pl.BlockSpec((1, D_MODEL), full),
                  pl.BlockSpec((D_MODEL, D_MODEL), full), pl.BlockSpec((PLE_DIM, D_MODEL), full),
                  pl.BlockSpec((1, D_MODEL), full)],
        out_specs=[pl.BlockSpec((tm, D_MODEL), row),
                   pl.BlockSpec((nf, nseq, CONV_WIDTH - 1, tf), lambda i, f: (0, 0, 0, 0))],
        out_shape=[jax.ShapeDtypeStruct((n, D_MODEL), F32),
                   jax.ShapeDtypeStruct((nf, nseq, CONV_WIDTH - 1, tf), F32)],
        scratch_shapes=[pltpu.VMEM((tm, D_MODEL), F32), pltpu.VMEM((tm, D_MODEL), BF16),
                        pltpu.VMEM((tm, D_MODEL), F32), pltpu.VMEM((nf, 8, tf), F32)],
        compiler_params=_cparams(("arbitrary", "arbitrary")), name="ffn",
    )(oa, ob, oc, h, lw["w_out"], lw["g_mix_post"],
      lw["g_ffn_pre"], lw["w_up"], lw["w_up"], lw["conv_w"], lw["conv_b"], lw["w_down"], lw["g_ffn_post"], left,
      p_all, lw["g_ple_pre"], lw["w_ple_gate"], lw["w_ple_proj"], lw["g_ple_post"])
    return out, state.transpose(1, 2, 0, 3).reshape(nseq, CONV_WIDTH - 1, D_FF)


def _swap_halves(w):
    half = MLA_ROPE_DIM // 2
    return jnp.concatenate([w[..., half:], w[..., :half]], axis=-1)


def _pack_layer(w_in, b_forget, mla_q_norm, w_mla_uq, mla_kv_norm, w_mla_uk, w_mla_uv, lams, diff_subln, w_out,
                norm_mix_pre, norm_mix_post, norm_ffn_pre, norm_ffn_post, norm_ple_pre, norm_ple_post,
                w_ffn_up, ffn_conv_w, ffn_conv_b, w_ffn_down, w_ple_gate, w_ple_proj):
    zeros = lambda r, c: jnp.zeros((r, c), F32)
    w_kr = w_in[:, OFF_MLA_KR:OFF_DIFF_Q]
    w_all = jnp.concatenate([
        w_in[:, OFF_FOX_Q:OFF_FOX_F], w_in[:, OFF_DIFF_Q:IN_WIDTH],
        w_in[:, OFF_MLA_CQ:OFF_MLA_CKV], w_in[:, OFF_MLA_CKV:OFF_MLA_KR],
        w_kr, w_kr, zeros(D_MODEL, LANE - 2 * MLA_ROPE_DIM),
        _swap_halves(w_kr), _swap_halves(w_kr), zeros(D_MODEL, LANE - 2 * MLA_ROPE_DIM),
        w_in[:, OFF_FOX_F:OFF_MLA_CQ], zeros(D_MODEL, LANE - N_FOX_HEADS)], axis=1).astype(BF16)
    wq = w_mla_uq.reshape(MLA_Q_RANK, N_MLA_HEADS, MLA_NOPE_DIM + MLA_ROPE_DIM)
    plain, swapped = [], []
    pad = zeros(MLA_Q_RANK, MLA_PAIR_W - 2 * (MLA_NOPE_DIM + MLA_ROPE_DIM))
    for p in range(N_MLA_PAIRS):
        a, b = 2 * p, 2 * p + 1
        plain += [wq[:, a, :MLA_NOPE_DIM], wq[:, b, :MLA_NOPE_DIM], wq[:, a, MLA_NOPE_DIM:], wq[:, b, MLA_NOPE_DIM:], pad]
        swapped += [zeros(MLA_Q_RANK, 2 * MLA_NOPE_DIM), _swap_halves(wq[:, a, MLA_NOPE_DIM:]),
                    _swap_halves(wq[:, b, MLA_NOPE_DIM:]), pad]
    return dict(
        w_all=w_all,
        b_f=jnp.pad(b_forget, (0, LANE - N_FOX_HEADS)).reshape(1, LANE),
        g_q=mla_q_norm.reshape(1, -1), g_kv=mla_kv_norm.reshape(1, -1),
        w_uq2=jnp.concatenate(plain + swapped, axis=1).astype(BF16),
        w_ukv=jnp.concatenate([w_mla_uk, w_mla_uv], axis=1).astype(BF16),
        lams=lams, subln=jnp.tile(diff_subln, 2).reshape(1, LANE),
        w_out=w_out.astype(BF16),
        g_mix_pre=norm_mix_pre.reshape(1, -1), g_mix_post=norm_mix_post.reshape(1, -1),
        g_ffn_pre=norm_ffn_pre.reshape(1, -1), g_ffn_post=norm_ffn_post.reshape(1, -1),
        g_ple_pre=norm_ple_pre.reshape(1, -1), g_ple_post=norm_ple_post.reshape(1, -1),
        w_up=w_ffn_up.astype(BF16), conv_w=ffn_conv_w, conv_b=ffn_conv_b.reshape(1, -1),
        w_down=w_ffn_down.astype(BF16), w_ple_gate=w_ple_gate.astype(BF16), w_ple_proj=w_ple_proj.astype(BF16))


def _rope_tables(pos):
    half = MLA_ROPE_DIM // 2
    inv_freq = ROPE_THETA ** (-jnp.arange(half, dtype=F32) / half)
    ang = pos.astype(F32)[:, None] * inv_freq[None, :]
    cos, sin = jnp.cos(ang), jnp.sin(ang)
    t = pos.shape[0]
    pad = jnp.zeros((t, MLA_PAIR_W - LANE - 2 * MLA_ROPE_DIM), F32)
    cos_t = jnp.concatenate([jnp.ones((t, LANE), F32), cos, cos, cos, cos, pad], axis=1)
    sin_t = jnp.concatenate([jnp.zeros((t, LANE), F32), -sin, sin, -sin, sin, pad], axis=1)
    return cos_t, sin_t


def _alibi_slopes():
    s = 2.0 ** (-8.0 * np.arange(1, N_DIFF_HEADS + 1) / N_DIFF_HEADS)
    return jnp.asarray(np.broadcast_to(s.reshape(N_DIFF_HEADS // 2, 2, 1), (N_DIFF_HEADS // 2, 2, LANE)), dtype=F32)


def _tile_rows(n, pref):
    t = min(n, pref)
    assert n % t == 0
    return t


def _layer(h, p, lw, cos, sin, cache, conv_left, lam_init, prev_rows, *, batch, seq, past, tq, depth):
    n = h.shape[0]
    tm = _tile_rows(n, 512)
    layer = p[1]
    emit_vt = cache is None and tm == tq
    outs = _proj_call(h, lw, cos, sin, tm, layer, depth, prev_rows, seq if emit_vt else None)
    rows = tuple(outs[:N_ROW_OUTPUTS])
    fox_bf, diff_bf, mla_q, mla_k, mla_v, logf = outs[N_ROW_OUTPUTS:N_ROW_OUTPUTS + 6]
    b3 = lambda a: a.reshape(batch, seq, a.shape[-1])
    diff_extras = [_alibi_slopes(), lw["lams"], lw["subln"]]
    diff_specs = [pl.BlockSpec((1, 2, LANE), lambda bb, hp, i: (hp, 0, 0)),
                  pl.BlockSpec((4, DIFF_QK_DIM), lambda bb, hp, i: (0, 0)),
                  pl.BlockSpec((1, LANE), lambda bb, hp, i: (0, 0))]
    fb, db = b3(fox_bf), b3(diff_bf)
    if cache is None:
        common = dict(tq=tq, tkd=tq, qoff=0, blocked=True)
        if emit_vt:
            vt_fox, vt_mla, vt_diff = outs[N_ROW_OUTPUTS + 6:]
        else:
            vt_fox, vt_mla, vt_diff = (_values_t(fb[..., 2 * FOX_WIDTH:], tq), _values_t(b3(mla_v), tq),
                                       _values_t(db[..., 2 * DIFF_WIDTH:], tq))
        o_a = _attn_call(_fox_kernel, "fox_attn", fb, 0, fb, 3, _forget_bias_call(b3(logf)), vt_fox, [], [],
                         npairs=N_FOX_HEADS // 2, qw=LANE, chains=2, **common)
        o_b = _attn_call(_mla_kernel, "mla_attn", b3(mla_q), 0, b3(mla_k), 0, None, vt_mla, [], [],
                         npairs=N_MLA_PAIRS, qw=MLA_PAIR_W, chains=2, **common)
        o_c = _attn_call(_diff_kernel, "diff_attn", db, 0, db, 2, _alibi_bias_block(seq), vt_diff,
                         diff_extras, diff_specs,
                         npairs=N_DIFF_HEADS // 2, qw=LANE, chains=4, lam_init=lam_init, **common)
    else:
        c_fox_k, c_fox_v, c_logf, c_ckv, c_krope, c_diff_k, c_diff_v = cache
        tk = -(-(past + seq) // LANE) * LANE
        padk = lambda a: jnp.pad(a, ((0, 0), (0, tk - past - seq), (0, 0)))
        cat = lambda old, new: padk(jnp.concatenate([old.astype(new.dtype), new], axis=1))
        flat = lambda a: lax.optimization_barrier(a.reshape(batch, past, -1))
        padq = lambda a: jnp.pad(a, ((0, 0), (0, tq - seq), (0, 0)))
        kf = cat(flat(c_fox_k), fb[..., FOX_WIDTH:2 * FOX_WIDTH])
        vf = _values_t_cached(c_fox_v, fb[..., 2 * FOX_WIDTH:], tk)
        kd = cat(flat(c_diff_k), db[..., DIFF_WIDTH:2 * DIFF_WIDTH])
        vd = _values_t_cached(c_diff_v, db[..., 2 * DIFF_WIDTH:], tk)
        km_c, v_c = _kvup_call(c_ckv, c_krope, layer, lw["w_ukv"], _tile_rows(batch * past, 1024))
        km = cat(km_c.reshape(batch, past, MLA_QK_W), b3(mla_k))
        vm = cat(v_c.reshape(batch, past, MLA_WIDTH), b3(mla_v))
        common = dict(tq=tq, tkd=tk, qoff=past, blocked=False)
        o_a = _attn_call(_fox_kernel, "fox_attn_s", padq(fb[..., :FOX_WIDTH]), 0, kf, 0,
                         _forget_bias_call(cat(jnp.pad(c_logf, ((0, 0), (0, 0), (0, LANE - c_logf.shape[-1]))),
                                               b3(logf))), vf, [], [],
                         npairs=N_FOX_HEADS // 2, qw=LANE, chains=2, **common)
        o_b = _attn_call(_mla_kernel, "mla_attn_s", padq(b3(mla_q)), 0, km, 0, None, _values_t(vm, tk), [], [],
                         npairs=N_MLA_PAIRS, qw=MLA_PAIR_W, chains=2, **common)
        o_c = _attn_call(_diff_kernel, "diff_attn_s", padq(db[..., :DIFF_WIDTH]), 0, kd, 0, _alibi_bias_block(tk),
                         vd, diff_extras, diff_specs,
                         npairs=N_DIFF_HEADS // 2, qw=LANE, chains=4, lam_init=lam_init, **common)
        o_a, o_b, o_c = o_a[:, :seq], o_b[:, :seq], o_c[:, :seq]
    f2 = lambda a: a.reshape(n, a.shape[-1])
    h, conv_state = _ffn_call(f2(o_a), f2(o_b), f2(o_c), h, p, lw, conv_left, seq, tm, 2048)
    return h, rows, conv_state


def kernel(x_prompt, x_sample, cache_fox_k, cache_fox_v, cache_fox_logf, cache_mla_ckv, cache_mla_krope, cache_diff_k, cache_diff_v, state_ffn_conv, p_prompt, p_sample, w_in, b_forget, mla_q_norm, w_mla_uq, mla_kv_norm, w_mla_uk, w_mla_uv, diff_lambda_q1, diff_lambda_k1, diff_lambda_q2, diff_lambda_k2, diff_subln, w_out, norm_mix_pre, norm_mix_post, norm_ffn_pre, norm_ffn_post, norm_ple_pre, norm_ple_post, w_ffn_up, ffn_conv_w, ffn_conv_b, w_ffn_down, w_ple_gate, w_ple_proj):
    bp, sp, _ = x_prompt.shape
    bs, ts, _ = x_sample.shape
    depth, _, past = cache_fox_k.shape[:3]
    assert past % CHUNK == 0 and ts <= CHUNK and sp % LANE == 0
    tq = _tile_rows(sp, 512)
    tms = _tile_rows(bs * ts, 512)

    cos_p, sin_p = _rope_tables(jnp.arange(sp))
    cos_s, sin_s = _rope_tables(past + jnp.arange(ts))
    cos_s, sin_s = jnp.tile(cos_s, (tms // ts, 1)), jnp.tile(sin_s, (tms // ts, 1))

    hp = x_prompt.reshape(bp * sp, D_MODEL)
    hs = x_sample.reshape(bs * ts, D_MODEL)
    rows_p, rows_s, conv_p, conv_s = None, None, [], []
    for l in range(depth):
        lams = jnp.stack([diff_lambda_q1[l], diff_lambda_k1[l], diff_lambda_q2[l], diff_lambda_k2[l]], axis=0)
        lw = _pack_layer(w_in[l], b_forget[l], mla_q_norm[l], w_mla_uq[l], mla_kv_norm[l], w_mla_uk[l], w_mla_uv[l],
                         lams, diff_subln[l], w_out[l], norm_mix_pre[l], norm_mix_post[l], norm_ffn_pre[l],
                         norm_ffn_post[l], norm_ple_pre[l], norm_ple_post[l], w_ffn_up[l], ffn_conv_w[l],
                         ffn_conv_b[l], w_ffn_down[l], w_ple_gate[l], w_ple_proj[l])
        lam_init = 0.8 - 0.6 * math.exp(-0.3 * l)
        hp, rows_p, cp = _layer(hp, (p_prompt.reshape(depth, bp * sp, PLE_DIM), l), lw, cos_p, sin_p, None,
                                jnp.zeros((bp, CONV_WIDTH - 1, D_FF), F32), lam_init, rows_p,
                                batch=bp, seq=sp, past=0, tq=tq, depth=depth)
        cache_l = (cache_fox_k[l], cache_fox_v[l], cache_fox_logf[l],
                   cache_mla_ckv.reshape(depth, bs * past, MLA_KV_RANK),
                   cache_mla_krope.reshape(depth, bs * past, MLA_ROPE_DIM), cache_diff_k[l], cache_diff_v[l])
        hs, rows_s, cs = _layer(hs, (p_sample.reshape(depth, bs * ts, PLE_DIM), l), lw, cos_s, sin_s, cache_l,
                                state_ffn_conv[l], lam_init, rows_s,
                                batch=bs, seq=ts, past=past, tq=LANE, depth=depth)
        conv_p.append(cp)
        conv_s.append(cs)

    def stack_rows(rows, batch, seq):
        fk, fv, lf, ckv, kr, dk, dv = rows
        lead = (depth, batch, seq)
        return (fk.reshape(lead + (N_FOX_HEADS, HEAD_DIM)), fv.reshape(lead + (N_FOX_HEADS, HEAD_DIM)),
                lf.reshape(lead + (N_FOX_HEADS,)), ckv.reshape(lead + (MLA_KV_RANK,)),
                kr.reshape(lead + (MLA_ROPE_DIM,)), dk.reshape(lead + (N_DIFF_HEADS, 2 * DIFF_QK_DIM)),
                dv.reshape(lead + (N_DIFF_HEADS, DIFF_V_DIM)))

    out_p = stack_rows(rows_p, bp, sp)
    out_s = stack_rows(rows_s, bs, ts)
    return ((hp.reshape(bp, sp, D_MODEL), hs.reshape(bs, ts, D_MODEL)) + out_p + (jnp.stack(conv_p, axis=0),)
            + out_s + (jnp.stack(conv_s, axis=0),))
```

```python
import functools
import math

import jax
import jax.numpy as jnp
import numpy as np
from jax import lax
from jax.experimental import pallas as pl
from jax.experimental.pallas import tpu as pltpu

F32 = jnp.float32
BF16 = jnp.bfloat16

D_MODEL = 1024
HEAD_DIM = 64
N_FOX_HEADS = 6
N_MLA_HEADS = 6
N_DIFF_HEADS = 4
MLA_Q_RANK = 384
MLA_KV_RANK = 256
MLA_NOPE_DIM = 64
MLA_ROPE_DIM = 32
MLA_V_DIM = 64
DIFF_QK_DIM = 32
DIFF_V_DIM = 64
FOX_WIDTH = N_FOX_HEADS * HEAD_DIM
MLA_WIDTH = N_MLA_HEADS * MLA_V_DIM
DIFF_WIDTH = N_DIFF_HEADS * DIFF_V_DIM
D_FF = 4 * D_MODEL
CONV_WIDTH = 3
PLE_DIM = 256
CHUNK = 64
ROPE_THETA = 10000.0
RMS_EPS = 1e-6
NEG_INF = -1e30
LOG2E = math.log2(math.e)

OFF_FOX_Q = 0
OFF_FOX_F = 3 * FOX_WIDTH
OFF_MLA_CQ = OFF_FOX_F + N_FOX_HEADS
OFF_MLA_CKV = OFF_MLA_CQ + MLA_Q_RANK
OFF_MLA_KR = OFF_MLA_CKV + MLA_KV_RANK
OFF_DIFF_Q = OFF_MLA_KR + MLA_ROPE_DIM
IN_WIDTH = OFF_DIFF_Q + 2 * N_DIFF_HEADS * 2 * DIFF_QK_DIM + DIFF_WIDTH

LANE = 128
V7X_VMEM_BYTES = 64 * 1024 * 1024
VMEM_LIMIT = (V7X_VMEM_BYTES * 7) // 8

C_FOX = 0
C_DIFF = C_FOX + 3 * FOX_WIDTH
C_CQ = C_DIFF + 3 * DIFF_WIDTH
C_CKV = C_CQ + MLA_Q_RANK
C_KR = C_CKV + MLA_KV_RANK
C_F = C_KR + 2 * LANE
W_ALL = C_F + LANE
N_MLA_PAIRS = N_MLA_HEADS // 2
MLA_PAIR_W = 2 * LANE
MLA_QK_W = N_MLA_PAIRS * MLA_PAIR_W

FOX_QSCALE = HEAD_DIM ** -0.5 * LOG2E
MLA_QSCALE = (MLA_NOPE_DIM + MLA_ROPE_DIM) ** -0.5 * LOG2E
DIFF_QSCALE = DIFF_QK_DIM ** -0.5 * LOG2E


def _cparams(sem):
    return pltpu.CompilerParams(dimension_semantics=sem, vmem_limit_bytes=VMEM_LIMIT)


def _rms(x, g):
    return x * lax.rsqrt(jnp.mean(x * x, axis=-1, keepdims=True) + RMS_EPS) * g


def _dot(a, b):
    return jnp.dot(a, b, preferred_element_type=F32)


def _dot_nt(a, b):
    return lax.dot_general(a, b, (((1,), (1,)), ((), ())), preferred_element_type=F32)


def _store_values_t(vt_ref, v):
    dv = LANE // 2
    ones = jnp.ones((ONES_ROWS, v.shape[0]), BF16)
    for p in range(v.shape[1] // LANE):
        t = v[:, p * LANE:(p + 1) * LANE].T.astype(BF16)
        for e in range(2):
            r0 = e * (dv + ONES_ROWS)
            vt_ref[0, p, 0, r0:r0 + dv, :] = t[e * dv:(e + 1) * dv, :]
            vt_ref[0, p, 0, r0 + dv:r0 + dv + ONES_ROWS, :] = ones


def _proj_kernel(h_ref, gpre_ref, w_ref, bf_ref, gq_ref, gkv_ref, wuq_ref, wukv_ref, cos_ref, sin_ref, *rest,
                 n_alias, emit_vt):
    (foxk_ref, foxv_ref, logf_ref, ckv_ref, krope_ref, diffk_ref, diffv_ref,
     foxbf_ref, diffbf_ref, mlaq_ref, mlak_ref, mlav_ref) = rest[n_alias:n_alias + 12]
    xn = _rms(h_ref[...], gpre_ref[...]).astype(BF16)
    cos = cos_ref[...]
    sin = sin_ref[...]

    z = _dot(xn, w_ref[:, C_FOX:C_FOX + 3 * FOX_WIDTH])
    foxbf_ref[:, :FOX_WIDTH] = (z[:, :FOX_WIDTH] * FOX_QSCALE).astype(BF16)
    foxbf_ref[:, FOX_WIDTH:] = z[:, FOX_WIDTH:].astype(BF16)
    foxk_ref[...] = z[:, FOX_WIDTH:2 * FOX_WIDTH]
    foxv_ref[...] = z[:, 2 * FOX_WIDTH:3 * FOX_WIDTH]
    if emit_vt:
        _store_values_t(rest[n_alias + 13], z[:, 2 * FOX_WIDTH:3 * FOX_WIDTH])

    z = _dot(xn, w_ref[:, C_DIFF:C_DIFF + 3 * DIFF_WIDTH])
    diffbf_ref[:, :DIFF_WIDTH] = (z[:, :DIFF_WIDTH] * DIFF_QSCALE).astype(BF16)
    diffbf_ref[:, DIFF_WIDTH:] = z[:, DIFF_WIDTH:].astype(BF16)
    diffk_ref[...] = z[:, DIFF_WIDTH:2 * DIFF_WIDTH]
    diffv_ref[...] = z[:, 2 * DIFF_WIDTH:3 * DIFF_WIDTH]
    if emit_vt:
        _store_values_t(rest[n_alias + 15], z[:, 2 * DIFF_WIDTH:3 * DIFF_WIDTH])

    cq = _rms(_dot(xn, w_ref[:, C_CQ:C_CQ + MLA_Q_RANK]), gq_ref[...]).astype(BF16)
    q2 = _dot(cq, wuq_ref[...])
    for p in range(N_MLA_PAIRS):
        lo = p * MLA_PAIR_W
        roped = q2[:, lo:lo + MLA_PAIR_W] * cos + q2[:, MLA_QK_W + lo:MLA_QK_W + lo + MLA_PAIR_W] * sin
        mlaq_ref[:, lo:lo + MLA_PAIR_W] = (roped * MLA_QSCALE).astype(BF16)

    ckv = _rms(_dot(xn, w_ref[:, C_CKV:C_CKV + MLA_KV_RANK]), gkv_ref[...])
    ckv_ref[...] = ckv
    kv = _dot(ckv.astype(BF16), wukv_ref[...])
    mlav_ref[...] = kv[:, MLA_WIDTH:2 * MLA_WIDTH].astype(BF16)
    if emit_vt:
        _store_values_t(rest[n_alias + 14], kv[:, MLA_WIDTH:2 * MLA_WIDTH])
    zkr =_dot(xn, w_ref[:, C_KR:C_KR + 2 * LANE])
    krp = zkr[:, :LANE] * cos[:, LANE:] + zkr[:, LANE:] * sin[:, LANE:]
    krope_ref[...] = krp[:, :MLA_ROPE_DIM]
    krp16 = krp.astype(BF16)
    for p in range(N_MLA_PAIRS):
        lo = p * MLA_PAIR_W
        mlak_ref[:, lo:lo + LANE] = kv[:, p * LANE:(p + 1) * LANE].astype(BF16)
        mlak_ref[:, lo + LANE:lo + 2 * LANE] = krp16

    zf = _dot(xn, w_ref[:, C_F:C_F + LANE]) + bf_ref[...]
    logf = -(jnp.maximum(-zf, 0.0) + jnp.log1p(jnp.exp(-jnp.abs(zf))))
    logf_ref[...] = logf[:, :N_FOX_HEADS]
    lane = lax.broadcasted_iota(jnp.int32, (1, LANE), 1)
    rest[n_alias + 12][...] = jnp.where(lane < N_FOX_HEADS, logf, 0.0)


N_ROW_OUTPUTS = 7


def _proj_call(h, lw, cos, sin, tm, layer, depth, prev_rows, vt_seq):
    n = h.shape[0]
    ntab = cos.shape[0] // tm
    row = lambda i: (i, 0)
    full = lambda i: (0, 0)
    tab = lambda i: (i % ntab, 0)
    in_specs = [
        pl.BlockSpec((tm, D_MODEL), row),
        pl.BlockSpec((1, D_MODEL), full),
        pl.BlockSpec((D_MODEL, W_ALL), full),
        pl.BlockSpec((1, LANE), full),
        pl.BlockSpec((1, MLA_Q_RANK), full),
        pl.BlockSpec((1, MLA_KV_RANK), full),
        pl.BlockSpec((MLA_Q_RANK, 2 * MLA_QK_W), full),
        pl.BlockSpec((MLA_KV_RANK, 2 * MLA_WIDTH), full),
        pl.BlockSpec((tm, MLA_PAIR_W), tab),
        pl.BlockSpec((tm, MLA_PAIR_W), tab),
    ]
    widths = [(FOX_WIDTH, F32), (FOX_WIDTH, F32), (N_FOX_HEADS, F32), (MLA_KV_RANK, F32), (MLA_ROPE_DIM, F32),
              (DIFF_WIDTH, F32), (DIFF_WIDTH, F32),
              (3 * FOX_WIDTH, BF16), (3 * DIFF_WIDTH, BF16), (MLA_QK_W, BF16), (MLA_QK_W, BF16), (MLA_WIDTH, BF16),
              (LANE, F32)]
    out_shape = ([jax.ShapeDtypeStruct((depth, n, w), dt) for w, dt in widths[:N_ROW_OUTPUTS]]
                 + [jax.ShapeDtypeStruct((n, w), dt) for w, dt in widths[N_ROW_OUTPUTS:]])
    out_specs = ([pl.BlockSpec((None, tm, w), lambda i: (layer, i, 0)) for w, _ in widths[:N_ROW_OUTPUTS]]
                 + [pl.BlockSpec((tm, w), row) for w, _ in widths[N_ROW_OUTPUTS:]])
    if vt_seq is not None:
        nblk = vt_seq // tm
        for width in (FOX_WIDTH, MLA_WIDTH, DIFF_WIDTH):
            shape = (n // vt_seq, width // LANE, nblk, 2 * (LANE // 2 + ONES_ROWS), tm)
            out_shape.append(jax.ShapeDtypeStruct(shape, BF16))
            out_specs.append(pl.BlockSpec((1,) + shape[1:2] + (1,) + shape[3:],
                                          lambda i: (i // nblk, 0, i % nblk, 0, 0)))
    args = [h, lw["g_mix_pre"], lw["w_all"], lw["b_f"], lw["g_q"], lw["g_kv"], lw["w_uq2"], lw["w_ukv"], cos, sin]
    aliases = {}
    if prev_rows is not None:
        aliases = {len(args) + k: k for k in range(N_ROW_OUTPUTS)}
        in_specs = in_specs + [pl.BlockSpec(memory_space=pl.ANY)] * N_ROW_OUTPUTS
        args = args + list(prev_rows)
    return pl.pallas_call(
        functools.partial(_proj_kernel, n_alias=len(aliases), emit_vt=vt_seq is not None), grid=(n // tm,),
        in_specs=in_specs,
        out_specs=out_specs, out_shape=out_shape, input_output_aliases=aliases,
        compiler_params=_cparams(("parallel",)), name="proj",
    )(*args)


def _kvup_kernel(ckv_ref, kr_ref, wukv_ref, place_ref, k_ref, v_ref):
    kv = _dot(ckv_ref[...].astype(BF16), wukv_ref[...])
    v_ref[...] = kv[:, MLA_WIDTH:].astype(BF16)
    krp = _dot(kr_ref[...].astype(BF16), place_ref[...]).astype(BF16)
    for p in range(N_MLA_PAIRS):
        lo = p * MLA_PAIR_W
        k_ref[:, lo:lo + LANE] = kv[:, p * LANE:(p + 1) * LANE].astype(BF16)
        k_ref[:, lo + LANE:lo + 2 * LANE] = krp


def _kvup_call(ckv_all, krope_all, layer, w_ukv, tm):
    n = ckv_all.shape[1]
    row = lambda i: (i, 0)
    lrow = lambda i: (layer, i, 0)
    full = lambda i: (0, 0)
    eye = jnp.eye(MLA_ROPE_DIM, dtype=BF16)
    place = jnp.concatenate([eye, eye, jnp.zeros((MLA_ROPE_DIM, LANE - 2 * MLA_ROPE_DIM), BF16)], axis=1)
    return pl.pallas_call(
        _kvup_kernel, grid=(n // tm,),
        in_specs=[pl.BlockSpec((None, tm, MLA_KV_RANK), lrow), pl.BlockSpec((None, tm, MLA_ROPE_DIM), lrow),
                  pl.BlockSpec((MLA_KV_RANK, 2 * MLA_WIDTH), full), pl.BlockSpec((MLA_ROPE_DIM, LANE), full)],
        out_specs=[pl.BlockSpec((tm, MLA_QK_W), row), pl.BlockSpec((tm, MLA_WIDTH), row)],
        out_shape=[jax.ShapeDtypeStruct((n, MLA_QK_W), BF16), jax.ShapeDtypeStruct((n, MLA_WIDTH), BF16)],
        compiler_params=_cparams(("parallel",)), name="kvup",
    )(ckv_all, krope_all, w_ukv, place)


AUG_STRIDE = 8


def _keep_bf16_bits(x):
    bits = lax.bitcast_convert_type(x, jnp.uint32) & jnp.uint32(0xFFFF0000)
    return lax.bitcast_convert_type(bits, F32)


def _split3(x):
    x1 = _keep_bf16_bits(x)
    r = x - x1
    x2 = _keep_bf16_bits(r)
    return x1, x2, r - x2


def _place3(x1, x2, x3):
    lane = lax.broadcasted_iota(jnp.int32, (1, LANE), 1)
    return jnp.where(lane < AUG_STRIDE, x1,
                     jnp.where(lane < 2 * AUG_STRIDE, pltpu.roll(x2, AUG_STRIDE, 1),
                               pltpu.roll(x3, 2 * AUG_STRIDE, 1))).astype(BF16)


def _aug_selector(h, rows):
    lane = lax.broadcasted_iota(jnp.int32, (rows, LANE), 1)
    hit = (lane == h) | (lane == AUG_STRIDE + h) | (lane == 2 * AUG_STRIDE + h)
    return jnp.where(hit, 1.0, 0.0).astype(BF16)


def _forget_bias_kernel(x_ref, tri_ref, o_ref, carry_sc, *, tb):
    @pl.when(pl.program_id(1) == 0)
    def _():
        carry_sc[...] = jnp.zeros_like(carry_sc)

    terms = jnp.concatenate(_split3(x_ref[0]), axis=1).astype(BF16)
    sums = _dot(tri_ref[...], terms)
    cs = carry_sc[0:1, :] + sums[:, :LANE] + sums[:, LANE:2 * LANE] + sums[:, 2 * LANE:]
    carry_sc[...] = jnp.broadcast_to(cs[tb - 1:tb, :], carry_sc.shape)
    o_ref[0] = _place3(*_split3(cs * (-LOG2E)))


def _forget_bias_call(logf):
    b, t, _ = logf.shape
    tb = max(d for d in range(LANE, 4 * LANE + 1, LANE) if t % d == 0)
    x = logf
    tri = jnp.tril(jnp.ones((tb, tb), BF16))
    blk = pl.BlockSpec((1, tb, LANE), lambda i, j: (i, j, 0))
    return pl.pallas_call(
        functools.partial(_forget_bias_kernel, tb=tb), grid=(b, t // tb),
        in_specs=[blk, pl.BlockSpec((tb, tb), lambda i, j: (0, 0))], out_specs=blk,
        out_shape=jax.ShapeDtypeStruct((b, t, LANE), BF16),
        scratch_shapes=[pltpu.VMEM((8, LANE), F32)],
        compiler_params=_cparams(("parallel", "arbitrary")), name="forget_bias",
    )(x, tri)


def _alibi_bias_block(t_k):
    slopes = 2.0 ** (-8.0 * np.arange(1, N_DIFF_HEADS + 1) / N_DIFF_HEADS)
    b = jnp.zeros((t_k, LANE), F32).at[:, :N_DIFF_HEADS].set(
        jnp.arange(t_k, dtype=F32)[:, None] * jnp.asarray(slopes * LOG2E, F32)[None, :])
    x1, x2, x3 = _split3(b)
    lane = jnp.arange(LANE)[None, :]
    placed = jnp.where(lane < AUG_STRIDE, x1, jnp.where(lane < 2 * AUG_STRIDE, jnp.roll(x2, AUG_STRIDE, 1),
                                                        jnp.roll(x3, 2 * AUG_STRIDE, 1)))
    return placed.astype(BF16)[None]


ROW_CHUNK = CHUNK
STRIP = 4 * LANE
ONES_ROWS = 16


def _masked_chunks(s_ref, c, mask, tk, tq):
    chunks = [(r0, "all" if mask is None else mask[0](r0, 0, tq)) for r0 in range(0, tk, ROW_CHUNK)]

    def logits(r0, vis):
        x = s_ref[c, r0:r0 + ROW_CHUNK, :]
        return x if vis == "all" else mask[1](r0, 0, x)
    return chunks, logits


def _softmax_max(s_ref, m_ref, c, mask, tk, tq):
    chunks, logits = _masked_chunks(s_ref, c, mask, tk, tq)
    mrun = jnp.full((8, tq), NEG_INF, F32)
    for r0, vis in chunks:
        if vis != "none":
            mrun = jnp.maximum(mrun, jnp.max(logits(r0, vis).reshape(ROW_CHUNK // 8, 8, tq), axis=0))
    m_old = m_ref[c]
    return m_old, jnp.maximum(m_old, jnp.max(mrun, axis=0, keepdims=True))


def _softmax_exp(s_ref, p_ref, m_ref, alpha_ref, c, mask, tk, tq, m_old, m_new):
    chunks, logits = _masked_chunks(s_ref, c, mask, tk, tq)
    for r0, vis in chunks:
        if vis == "none":
            p_ref[c, r0:r0 + ROW_CHUNK, :] = jnp.zeros((ROW_CHUNK, tq), BF16)
        else:
            p_ref[c, r0:r0 + ROW_CHUNK, :] = jnp.exp2(logits(r0, vis) - m_new).astype(BF16)
    m_ref[c] = m_new
    alpha_ref[c] = jnp.exp2(m_old - m_new)


def _apply_values(p_ref, alpha_ref, acc_ref, c, vt):
    acc_ref[c] = alpha_ref[c] * acc_ref[c] + _dot(vt, p_ref[c])


def _causal_mask(qoff):
    def visibility(r0, c0, sw):
        if r0 + ROW_CHUNK - 1 <= c0 + qoff:
            return "all"
        return "none" if r0 > c0 + sw - 1 + qoff else "some"

    def apply(r0, c0, x):
        kk, qq = _key_query_iota(*x.shape)
        return jnp.where(kk - qq <= c0 + qoff - r0, x, NEG_INF)
    return visibility, apply


def _chunk_visibility(r0, c0, sw, qoff):
    if r0 // CHUNK <= (c0 + qoff) // CHUNK:
        return "all"
    return "none" if r0 // CHUNK > (c0 + sw - 1 + qoff) // CHUNK else "some"


def _chunk_mask(r0, c0, sw, qoff):
    qq = lax.broadcasted_iota(jnp.int32, (1, sw), 1)
    return r0 // CHUNK <= (qq + (c0 + qoff)) // CHUNK


def _init_states(m_ref, acc_ref):
    m_ref[...] = jnp.full(m_ref.shape, NEG_INF, F32)
    acc_ref[...] = jnp.zeros(acc_ref.shape, F32)


def _normalized(acc_ref, c, dv):
    return acc_ref[c, :dv, :] / acc_ref[c, dv:dv + 1, :]


def _key_query_iota(tk, tq):
    return (lax.broadcasted_iota(jnp.int32, (tk, tq), 0), lax.broadcasted_iota(jnp.int32, (tk, tq), 1))


def _store_pair(o_ref, o0, o1, row_scale=None):
    out = jnp.concatenate([o0, o1], axis=0).T
    if row_scale is not None:
        out = out * row_scale
    o_ref[0] = out.astype(BF16)


def _block_start(j, size):
    return j * size if isinstance(j, int) else pl.multiple_of(j * size, size)


N_BUF = 3


def _attend(i, blocked, chains, qk_fn, max_fn, exp_fn, pv_fn):
    def step(qk, pv, buf, diag):
        if qk is not None:
            all_chains(qk_fn, *qk)
        if pv is not None:
            all_chains(pv_fn, *pv)
        for c in range(chains):
            exp_fn(buf, c, diag, *max_fn(buf, c, diag))

    def all_chains(fn, *args):
        for c in range(chains):
            fn(*args, c)

    if not blocked:
        all_chains(qk_fn, 0, 0)
        step(None, None, 0, True)
        all_chains(pv_fn, 0, 0)
        return
    last = N_BUF - 1
    lead = i % N_BUF

    def single(j, carry):
        all_chains(qk_fn, last, j)
        step(None, None, last, False)
        all_chains(pv_fn, last, j)
        return carry

    lax.fori_loop(0, lead, single, 0)
    all_chains(qk_fn, 0, lead)
    all_chains(qk_fn, 1, jnp.minimum(lead + 1, i))

    def rotate(t, carry):
        b0 = lead + N_BUF * t
        for u in range(N_BUF):
            step(((u + 2) % N_BUF, jnp.minimum(b0 + u + 2, i)), (u - 1, b0 + u - 1) if u > 0 else None, u, False)
        all_chains(pv_fn, last, b0 + last)
        return carry

    lax.fori_loop(0, (i - lead) // N_BUF, rotate, 0)
    step(None, None, 0, True)
    all_chains(pv_fn, 0, i)


def _vt_rows(vt_ref, j, e, dv):
    return vt_ref[0, 0, j, e * (dv + ONES_ROWS):(e + 1) * (dv + ONES_ROWS), :]


def _split_scratch(scratch):
    s_all, p_all = scratch[:2]
    nbuf = s_all.shape[0]
    return ([s_all.at[b] for b in range(nbuf)], [p_all.at[b] for b in range(nbuf)]) + tuple(scratch[2:])


def _chain_fns(scratch, vt_ref, mask_of, dv, head_of, tkd, tq):
    s_sc, p_sc, m_sc, alpha_sc, acc_sc = _split_scratch(scratch)

    def max_fn(buf, c, diag):
        return _softmax_max(s_sc[buf], m_sc, c, mask_of(c) if diag else None, tkd, tq)

    def exp_fn(buf, c, diag, m_old, m_new):
        _softmax_exp(s_sc[buf], p_sc[buf], m_sc, alpha_sc, c, mask_of(c) if diag else None, tkd, tq, m_old, m_new)

    def pv_fn(buf, j, c):
        _apply_values(p_sc[buf], alpha_sc, acc_sc, c, _vt_rows(vt_ref, j, head_of(c), dv))
    return max_fn, exp_fn, pv_fn


def _fox_kernel(q_ref, k_ref, aug_ref, vt_ref, o_ref, *scratch, tq, tkd, qoff, blocked):
    s_sc, p_sc, m_sc, alpha_sc, acc_sc = _split_scratch(scratch)
    hp, i = pl.program_id(1), pl.program_id(2)
    _init_states(m_sc, acc_sc)
    q = q_ref[0]
    lane = lax.broadcasted_iota(jnp.int32, (1, LANE), 1)
    qs = [jnp.concatenate([jnp.where((lane >= e * HEAD_DIM) & (lane < (e + 1) * HEAD_DIM), q, jnp.zeros_like(q)),
                           _aug_selector(2 * hp + e, tq)], axis=1) for e in range(2)]

    def qk_fn(buf, j, e):
        rows = pl.ds(_block_start(j, tq), tkd)
        k = jnp.concatenate([k_ref[0, rows, :], aug_ref[0, rows, :]], axis=1)
        s_sc[buf][e] = _dot_nt(k, qs[e])

    _attend(i, blocked, 2, qk_fn, *_chain_fns(scratch, vt_ref, lambda e: _causal_mask(qoff), HEAD_DIM,
                                              lambda e: e, tkd, tq))
    _store_pair(o_ref, *(_normalized(acc_sc, e, HEAD_DIM) for e in range(2)))


def _mla_kernel(q_ref, k_ref, vt_ref, o_ref, *scratch, tq, tkd, qoff, blocked):
    s_sc, p_sc, m_sc, alpha_sc, acc_sc = _split_scratch(scratch)
    i = pl.program_id(2)
    _init_states(m_sc, acc_sc)
    q = q_ref[0]
    lane2 = lax.broadcasted_iota(jnp.int32, (1, MLA_PAIR_W), 1)
    sel = []
    for e in range(2):
        nope = (lane2 >= e * MLA_NOPE_DIM) & (lane2 < (e + 1) * MLA_NOPE_DIM)
        rope = (lane2 >= LANE + e * MLA_ROPE_DIM) & (lane2 < LANE + (e + 1) * MLA_ROPE_DIM)
        sel.append(jnp.where(nope | rope, q, jnp.zeros_like(q)))

    def qk_fn(buf, j, e):
        s_sc[buf][e] = _dot_nt(k_ref[0, pl.ds(_block_start(j, tq), tkd), :], sel[e])

    mask = (lambda r0, c0, sw: _chunk_visibility(r0, c0, sw, qoff),
            lambda r0, c0, x: jnp.where(_chunk_mask(r0, c0, x.shape[1], qoff), x, NEG_INF))
    _attend(i, blocked, 2, qk_fn, *_chain_fns(scratch, vt_ref, lambda e: mask, MLA_V_DIM, lambda e: e, tkd, tq))
    _store_pair(o_ref, *(_normalized(acc_sc, e, MLA_V_DIM) for e in range(2)))


def _diff_kernel(q_ref, k_ref, aug_ref, vt_ref, slope_ref, lam_ref, subln_ref, o_ref, *scratch,
                 tq, tkd, qoff, blocked, lam_init):
    s_sc, p_sc, m_sc, alpha_sc, acc_sc = _split_scratch(scratch)
    hp, i = pl.program_id(1), pl.program_id(2)
    _init_states(m_sc, acc_sc)
    q = q_ref[0]
    lane = lax.broadcasted_iota(jnp.int32, (1, LANE), 1)
    sel = [jnp.concatenate([jnp.where((lane >= (2 * e + t) * DIFF_QK_DIM) & (lane < (2 * e + t + 1) * DIFF_QK_DIM),
                                      q, jnp.zeros_like(q)), _aug_selector(2 * hp + e, tq)], axis=1)
           for e in range(2) for t in range(2)]
    slopes = [slope_ref[0, e:e + 1, 0:1] * LOG2E for e in range(2)]

    def qk_fn(buf, j, c):
        rows = pl.ds(_block_start(j, tq), tkd)
        k = jnp.concatenate([k_ref[0, rows, :], aug_ref[0, rows, :]], axis=1)
        s_sc[buf][c] = _dot_nt(k, sel[c])

    def mask_of(c):
        def visibility(r0, c0, sw):
            vis = _chunk_visibility(r0, c0, sw, qoff)
            return "some" if vis == "all" and r0 + ROW_CHUNK - 1 > c0 + qoff else vis

        def apply(r0, c0, x):
            kk, qq = _key_query_iota(*x.shape)
            ahead = jnp.maximum(kk - qq + (r0 - c0 - qoff), 0).astype(F32)
            return jnp.where(_chunk_mask(r0, c0, x.shape[1], qoff), x - (2.0 * slopes[c // 2]) * ahead, NEG_INF)
        return visibility, apply

    _attend(i, blocked, 4, qk_fn, *_chain_fns(scratch, vt_ref, mask_of, DIFF_V_DIM, lambda c: c // 2, tkd, tq))
    res = [_normalized(acc_sc, c, DIFF_V_DIM) for c in range(4)]
    lq = lam_ref[...]
    lam = (jnp.exp(jnp.sum(lq[0:1] * lq[1:2], axis=-1, keepdims=True))
           - jnp.exp(jnp.sum(lq[2:3] * lq[3:4], axis=-1, keepdims=True)) + lam_init)
    outs = []
    for e in range(2):
        o = res[2 * e] - lam * res[2 * e + 1]
        outs.append(o * lax.rsqrt(jnp.mean(o * o, axis=0, keepdims=True) + RMS_EPS))
    _store_pair(o_ref, outs[0], outs[1], subln_ref[...] * (1.0 - lam_init))


def _attn_call(kernel, name, q, q_cb, k, k_cb, aug, vt, extras, extra_specs, *, npairs, qw, tq, tkd, qoff, blocked,
               chains, **kw):
    b, t_q = q.shape[0], q.shape[1]
    t_k = k.shape[1]
    acc_rows = LANE // 2 + ONES_ROWS
    nbuf = N_BUF if blocked else 1
    scratch = [pltpu.VMEM((nbuf, chains, tkd, tq), F32), pltpu.VMEM((nbuf, chains, tkd, tq), BF16),
               pltpu.VMEM((chains, 1, tq), F32), pltpu.VMEM((chains, 1, tq), F32),
               pltpu.VMEM((chains, acc_rows, tq), F32)]
    in_specs = [pl.BlockSpec((1, tq, qw), lambda bb, hp, i: (bb, i, q_cb + hp)),
                pl.BlockSpec((1, t_k, qw), lambda bb, hp, i: (bb, 0, k_cb + hp))]
    args = [q, k]
    if aug is not None:
        per_batch = aug.shape[0] > 1
        in_specs.append(pl.BlockSpec((1, t_k, LANE), lambda bb, hp, i: (bb if per_batch else 0, 0, 0)))
        args.append(aug)
    in_specs.append(pl.BlockSpec((1, 1) + vt.shape[2:], lambda bb, hp, i: (bb, hp, 0, 0, 0)))
    return pl.pallas_call(
        functools.partial(kernel, tq=tq, tkd=tkd, qoff=qoff, blocked=blocked, **kw),
        grid=(b, npairs, t_q // tq), in_specs=in_specs + extra_specs,
        out_specs=pl.BlockSpec((1, tq, LANE), lambda bb, hp, i: (bb, i, hp)),
        out_shape=jax.ShapeDtypeStruct((b, t_q, npairs * LANE), BF16), scratch_shapes=scratch,
        compiler_params=_cparams(("parallel", "parallel", "arbitrary")), name=name,
    )(*args, vt, *extras)


def _values_t(v, tkb):
    b, t, w = v.shape
    dv = LANE // 2
    vt = v.reshape(b, t // tkb, tkb, w // LANE, 2, dv).transpose(0, 3, 1, 4, 5, 2)
    ones = jnp.ones(vt.shape[:4] + (ONES_ROWS, tkb), v.dtype)
    return jnp.concatenate([vt, ones], axis=4).reshape(b, w // LANE, t // tkb, 2 * (dv + ONES_ROWS), tkb)


def _values_t_cached(cached, new, tk):
    b, p, hh, dv = cached.shape
    t = new.shape[1]
    vt = jnp.concatenate([jnp.transpose(cached, (0, 2, 3, 1)).astype(BF16),
                          jnp.transpose(new.reshape(b, t, hh, dv), (0, 2, 3, 1))], axis=-1)
    vt = jnp.pad(vt, ((0, 0), (0, 0), (0, 0), (0, tk - p - t)))
    ones = jnp.ones((b, hh, ONES_ROWS, tk), BF16)
    return jnp.concatenate([vt, ones], axis=2).reshape(b, hh // 2, 1, 2 * (dv + ONES_ROWS), tk)


FFN_HALVES = 4


def _ffn_kernel(oa_ref, ob_ref, oc_ref, h_ref, wout_ref, gmix_ref,
                gpre_ref, wg_ref, wv_ref, cw_ref, cb_ref, wd_ref, gpost_ref, left_ref,
                p_ref, gple_ref, wgate_ref, wproj_ref, gplepost_ref,
                o_ref, st_ref, h1_sc, xn_sc, acc_sc, carry_sc, *, tm, tf, nsb, seq_blocks):
    i = pl.program_id(0)
    f = pl.program_id(1)
    nf = pl.num_programs(1)
    tb = tm // nsb

    @pl.when(f == 0)
    def _():
        y = (_dot(oa_ref[...], wout_ref[0:FOX_WIDTH, :])
             + _dot(ob_ref[...], wout_ref[FOX_WIDTH:FOX_WIDTH + MLA_WIDTH, :])
             + _dot(oc_ref[...], wout_ref[FOX_WIDTH + MLA_WIDTH:, :]))
        h1 = h_ref[...] + _rms(y, gmix_ref[...])
        h1_sc[...] = h1
        xn_sc[...] = _rms(h1, gpre_ref[...]).astype(BF16)
        acc_sc[...] = jnp.zeros_like(acc_sc)

    xn = xn_sc[...]
    th = tf // FFN_HALVES
    halves = [slice(a * th, (a + 1) * th) for a in range(FFN_HALVES)]
    ups = [(_dot(xn, wg_ref[:, cols]), _dot(xn, wv_ref[:, cols])) for cols in halves]
    rin = lax.broadcasted_iota(jnp.int32, (tm, 1), 0) & (tb - 1)

    def spread(rows):
        return jnp.broadcast_to(rows, (nsb, tb, th)).reshape(tm, th)

    for cols, (gate, val) in zip(halves, ups):
        left = left_ref[:, :, cols]
        if seq_blocks > 1:
            left = jnp.where(i % seq_blocks == 0, left, carry_sc[f, 0:CONV_WIDTH - 1, cols][None])
            carry_sc[f, 0:CONV_WIDTH - 1, cols] = gate[tm - (CONV_WIDTH - 1):, :]
        st_ref[f, pl.ds((i // seq_blocks) * nsb, nsb), :, cols] = (
            gate.reshape(nsb, tb, th)[:, tb - (CONV_WIDTH - 1):, :])
        l0 = spread(left[:, 0:1, :])
        l1 = spread(left[:, 1:2, :])
        g1 = jnp.where(rin == 0, l1, pltpu.roll(gate, 1, 0))
        g2 = jnp.where(rin == 0, l0, jnp.where(rin == 1, l1, pltpu.roll(gate, 2, 0)))
        conv = cw_ref[0:1, cols] * g2 + cw_ref[1:2, cols] * g1 + cw_ref[2:3, cols] * gate + cb_ref[:, cols]
        gelu = 0.5 * conv * (1.0 + jnp.tanh(math.sqrt(2.0 / math.pi) * (conv + 0.044715 * (conv * conv * conv))))
        acc_sc[...] += _dot((gelu * val).astype(BF16), wd_ref[cols, :])

    @pl.when(f == nf - 1)
    def _():
        h2 = h1_sc[...] + _rms(acc_sc[...], gpost_ref[...])
        gate = jax.nn.sigmoid(_dot(_rms(h2, gple_ref[...]).astype(BF16), wgate_ref[...]))
        proj = _dot(p_ref[...].astype(BF16), wproj_ref[...])
        o_ref[...] = h2 + _rms(proj * gate, gplepost_ref[...])


def _ffn_call(oa, ob, oc, h, p, lw, left, seq_len, tm, tf):
    n = h.shape[0]
    p_all, p_layer = p
    nseq = left.shape[0]
    nf = D_FF // tf
    if seq_len >= tm:
        nsb, seq_blocks = 1, seq_len // tm
    else:
        nsb, seq_blocks = tm // seq_len, 1
    assert (tm // nsb) & (tm // nsb - 1) == 0
    row = lambda i, f: (i, 0)
    full = lambda i, f: (0, 0)
    out, state = pl.pallas_call(
        functools.partial(_ffn_kernel, tm=tm, tf=tf, nsb=nsb, seq_blocks=seq_blocks),
        grid=(n // tm, nf),
        in_specs=[pl.BlockSpec((tm, FOX_WIDTH), row), pl.BlockSpec((tm, MLA_WIDTH), row),
                  pl.BlockSpec((tm, DIFF_WIDTH), row), pl.BlockSpec((tm, D_MODEL), row),
                  pl.BlockSpec((D_MODEL, D_MODEL), full), pl.BlockSpec((1, D_MODEL), full),
                  pl.BlockSpec((1, D_MODEL), full),
                  pl.BlockSpec((D_MODEL, tf), lambda i, f: (0, f)),
                  pl.BlockSpec((D_MODEL, tf), lambda i, f: (0, nf + f)),
                  pl.BlockSpec((CONV_WIDTH, tf), lambda i, f: (0, f)),
                  pl.BlockSpec((1, tf), lambda i, f: (0, f)),
                  pl.BlockSpec((tf, D_MODEL), lambda i, f: (f, 0)),
                  pl.BlockSpec((1, D_MODEL), full),
                  pl.BlockSpec((nsb, CONV_WIDTH - 1, tf), lambda i, f: (i // seq_blocks, 0, f)),
                  pl.BlockSpec((None, tm, PLE_DIM), lambda i, f: (p_layer, i, 0)), pl.BlockSpec((1, D_MODEL), full),
                  pl.BlockSpec((D_MODEL, D_MODEL), full), pl.BlockSpec((PLE_DIM, D_MODEL), full),
                  pl.BlockSpec((1, D_MODEL), full)],
        out_specs=[pl.BlockSpec((tm, D_MODEL), row),
                   pl.BlockSpec((nf, nseq, CONV_WIDTH - 1, tf), lambda i, f: (0, 0, 0, 0))],
        out_shape=[jax.ShapeDtypeStruct((n, D_MODEL), F32),
                   jax.ShapeDtypeStruct((nf, nseq, CONV_WIDTH - 1, tf), F32)],
        scratch_shapes=[pltpu.VMEM((tm, D_MODEL), F32), pltpu.VMEM((tm, D_MODEL), BF16),
                        pltpu.VMEM((tm, D_MODEL), F32), pltpu.VMEM((nf, 8, tf), F32)],
        compiler_params=_cparams(("arbitrary", "arbitrary")), name="ffn",
    )(oa, ob, oc, h, lw["w_out"], lw["g_mix_post"],
      lw["g_ffn_pre"], lw["w_up"], lw["w_up"], lw["conv_w"], lw["conv_b"], lw["w_down"], lw["g_ffn_post"], left,
      p_all, lw["g_ple_pre"], lw["w_ple_gate"], lw["w_ple_proj"], lw["g_ple_post"])
    return out, state.transpose(1, 2, 0, 3).reshape(nseq, CONV_WIDTH - 1, D_FF)


def _swap_halves(w):
    half = MLA_ROPE_DIM // 2
    return jnp.concatenate([w[..., half:], w[..., :half]], axis=-1)


def _pack_layer(w_in, b_forget, mla_q_norm, w_mla_uq, mla_kv_norm, w_mla_uk, w_mla_uv, lams, diff_subln, w_out,
                norm_mix_pre, norm_mix_post, norm_ffn_pre, norm_ffn_post, norm_ple_pre, norm_ple_post,
                w_ffn_up, ffn_conv_w, ffn_conv_b, w_ffn_down, w_ple_gate, w_ple_proj):
    zeros = lambda r, c: jnp.zeros((r, c), F32)
    w_kr = w_in[:, OFF_MLA_KR:OFF_DIFF_Q]
    w_all = jnp.concatenate([
        w_in[:, OFF_FOX_Q:OFF_FOX_F], w_in[:, OFF_DIFF_Q:IN_WIDTH],
        w_in[:, OFF_MLA_CQ:OFF_MLA_CKV], w_in[:, OFF_MLA_CKV:OFF_MLA_KR],
        w_kr, w_kr, zeros(D_MODEL, LANE - 2 * MLA_ROPE_DIM),
        _swap_halves(w_kr), _swap_halves(w_kr), zeros(D_MODEL, LANE - 2 * MLA_ROPE_DIM),
        w_in[:, OFF_FOX_F:OFF_MLA_CQ], zeros(D_MODEL, LANE - N_FOX_HEADS)], axis=1).astype(BF16)
    wq = w_mla_uq.reshape(MLA_Q_RANK, N_MLA_HEADS, MLA_NOPE_DIM + MLA_ROPE_DIM)
    plain, swapped = [], []
    pad = zeros(MLA_Q_RANK, MLA_PAIR_W - 2 * (MLA_NOPE_DIM + MLA_ROPE_DIM))
    for p in range(N_MLA_PAIRS):
        a, b = 2 * p, 2 * p + 1
        plain += [wq[:, a, :MLA_NOPE_DIM], wq[:, b, :MLA_NOPE_DIM], wq[:, a, MLA_NOPE_DIM:], wq[:, b, MLA_NOPE_DIM:], pad]
        swapped += [zeros(MLA_Q_RANK, 2 * MLA_NOPE_DIM), _swap_halves(wq[:, a, MLA_NOPE_DIM:]),
                    _swap_halves(wq[:, b, MLA_NOPE_DIM:]), pad]
    return dict(
        w_all=w_all,
        b_f=jnp.pad(b_forget, (0, LANE - N_FOX_HEADS)).reshape(1, LANE),
        g_q=mla_q_norm.reshape(1, -1), g_kv=mla_kv_norm.reshape(1, -1),
        w_uq2=jnp.concatenate(plain + swapped, axis=1).astype(BF16),
        w_ukv=jnp.concatenate([w_mla_uk, w_mla_uv], axis=1).astype(BF16),
        lams=lams, subln=jnp.tile(diff_subln, 2).reshape(1, LANE),
        w_out=w_out.astype(BF16),
        g_mix_pre=norm_mix_pre.reshape(1, -1), g_mix_post=norm_mix_post.reshape(1, -1),
        g_ffn_pre=norm_ffn_pre.reshape(1, -1), g_ffn_post=norm_ffn_post.reshape(1, -1),
        g_ple_pre=norm_ple_pre.reshape(1, -1), g_ple_post=norm_ple_post.reshape(1, -1),
        w_up=w_ffn_up.astype(BF16), conv_w=ffn_conv_w, conv_b=ffn_conv_b.reshape(1, -1),
        w_down=w_ffn_down.astype(BF16), w_ple_gate=w_ple_gate.astype(BF16), w_ple_proj=w_ple_proj.astype(BF16))


def _rope_tables(pos):
    half = MLA_ROPE_DIM // 2
    inv_freq = ROPE_THETA ** (-jnp.arange(half, dtype=F32) / half)
    ang = pos.astype(F32)[:, None] * inv_freq[None, :]
    cos, sin = jnp.cos(ang), jnp.sin(ang)
    t = pos.shape[0]
    pad = jnp.zeros((t, MLA_PAIR_W - LANE - 2 * MLA_ROPE_DIM), F32)
    cos_t = jnp.concatenate([jnp.ones((t, LANE), F32), cos, cos, cos, cos, pad], axis=1)
    sin_t = jnp.concatenate([jnp.zeros((t, LANE), F32), -sin, sin, -sin, sin, pad], axis=1)
    return cos_t, sin_t


def _alibi_slopes():
    s = 2.0 ** (-8.0 * np.arange(1, N_DIFF_HEADS + 1) / N_DIFF_HEADS)
    return jnp.asarray(np.broadcast_to(s.reshape(N_DIFF_HEADS // 2, 2, 1), (N_DIFF_HEADS // 2, 2, LANE)), dtype=F32)


def _tile_rows(n, pref):
    t = min(n, pref)
    assert n % t == 0
    return t


def _layer(h, p, lw, cos, sin, cache, conv_left, lam_init, prev_rows, *, batch, seq, past, tq, depth):
    n = h.shape[0]
    tm = _tile_rows(n, 512)
    layer = p[1]
    emit_vt = cache is None and tm == tq
    outs = _proj_call(h, lw, cos, sin, tm, layer, depth, prev_rows, seq if emit_vt else None)
    rows = tuple(outs[:N_ROW_OUTPUTS])
    fox_bf, diff_bf, mla_q, mla_k, mla_v, logf = outs[N_ROW_OUTPUTS:N_ROW_OUTPUTS + 6]
    b3 = lambda a: a.reshape(batch, seq, a.shape[-1])
    diff_extras = [_alibi_slopes(), lw["lams"], lw["subln"]]
    diff_specs = [pl.BlockSpec((1, 2, LANE), lambda bb, hp, i: (hp, 0, 0)),
                  pl.BlockSpec((4, DIFF_QK_DIM), lambda bb, hp, i: (0, 0)),
                  pl.BlockSpec((1, LANE), lambda bb, hp, i: (0, 0))]
    fb, db = b3(fox_bf), b3(diff_bf)
    if cache is None:
        common = dict(tq=tq, tkd=tq, qoff=0, blocked=True)
        if emit_vt:
            vt_fox, vt_mla, vt_diff = outs[N_ROW_OUTPUTS + 6:]
        else:
            vt_fox, vt_mla, vt_diff = (_values_t(fb[..., 2 * FOX_WIDTH:], tq), _values_t(b3(mla_v), tq),
                                       _values_t(db[..., 2 * DIFF_WIDTH:], tq))
        o_a = _attn_call(_fox_kernel, "fox_attn", fb, 0, fb, 3, _forget_bias_call(b3(logf)), vt_fox, [], [],
                         npairs=N_FOX_HEADS // 2, qw=LANE, chains=2, **common)
        o_b = _attn_call(_mla_kernel, "mla_attn", b3(mla_q), 0, b3(mla_k), 0, None, vt_mla, [], [],
                         npairs=N_MLA_PAIRS, qw=MLA_PAIR_W, chains=2, **common)
        o_c = _attn_call(_diff_kernel, "diff_attn", db, 0, db, 2, _alibi_bias_block(seq), vt_diff,
                         diff_extras, diff_specs,
                         npairs=N_DIFF_HEADS // 2, qw=LANE, chains=4, lam_init=lam_init, **common)
    else:
        c_fox_k, c_fox_v, c_logf, c_ckv, c_krope, c_diff_k, c_diff_v = cache
        tk = -(-(past + seq) // LANE) * LANE
        padk = lambda a: jnp.pad(a, ((0, 0), (0, tk - past - seq), (0, 0)))
        cat = lambda old, new: padk(jnp.concatenate([old.astype(new.dtype), new], axis=1))
        flat = lambda a: lax.optimization_barrier(a.reshape(batch, past, -1))
        padq = lambda a: jnp.pad(a, ((0, 0), (0, tq - seq), (0, 0)))
        kf = cat(flat(c_fox_k), fb[..., FOX_WIDTH:2 * FOX_WIDTH])
        vf = _values_t_cached(c_fox_v, fb[..., 2 * FOX_WIDTH:], tk)
        kd = cat(flat(c_diff_k), db[..., DIFF_WIDTH:2 * DIFF_WIDTH])
        vd = _values_t_cached(c_diff_v, db[..., 2 * DIFF_WIDTH:], tk)
        km_c, v_c = _kvup_call(c_ckv, c_krope, layer, lw["w_ukv"], _tile_rows(batch * past, 1024))
        km = cat(km_c.reshape(batch, past, MLA_QK_W), b3(mla_k))
        vm = cat(v_c.reshape(batch, past, MLA_WIDTH), b3(mla_v))
        common = dict(tq=tq, tkd=tk, qoff=past, blocked=False)
        o_a = _attn_call(_fox_kernel, "fox_attn_s", padq(fb[..., :FOX_WIDTH]), 0, kf, 0,
                         _forget_bias_call(cat(jnp.pad(c_logf, ((0, 0), (0, 0), (0, LANE - c_logf.shape[-1]))),
                                               b3(logf))), vf, [], [],
                         npairs=N_FOX_HEADS // 2, qw=LANE, chains=2, **common)
        o_b = _attn_call(_mla_kernel, "mla_attn_s", padq(b3(mla_q)), 0, km, 0, None, _values_t(vm, tk), [], [],
                         npairs=N_MLA_PAIRS, qw=MLA_PAIR_W, chains=2, **common)
        o_c = _attn_call(_diff_kernel, "diff_attn_s", padq(db[..., :DIFF_WIDTH]), 0, kd, 0, _alibi_bias_block(tk),
                         vd, diff_extras, diff_specs,
                         npairs=N_DIFF_HEADS // 2, qw=LANE, chains=4, lam_init=lam_init, **common)
        o_a, o_b, o_c = o_a[:, :seq], o_b[:, :seq], o_c[:, :seq]
    f2 = lambda a: a.reshape(n, a.shape[-1])
    h, conv_state = _ffn_call(f2(o_a), f2(o_b), f2(o_c), h, p, lw, conv_left, seq, tm, 2048)
    return h, rows, conv_state


def kernel(x_prompt, x_sample, cache_fox_k, cache_fox_v, cache_fox_logf, cache_mla_ckv, cache_mla_krope, cache_diff_k, cache_diff_v, state_ffn_conv, p_prompt, p_sample, w_in, b_forget, mla_q_norm, w_mla_uq, mla_kv_norm, w_mla_uk, w_mla_uv, diff_lambda_q1, diff_lambda_k1, diff_lambda_q2, diff_lambda_k2, diff_subln, w_out, norm_mix_pre, norm_mix_post, norm_ffn_pre, norm_ffn_post, norm_ple_pre, norm_ple_post, w_ffn_up, ffn_conv_w, ffn_conv_b, w_ffn_down, w_ple_gate, w_ple_proj):
    bp, sp, _ = x_prompt.shape
    bs, ts, _ = x_sample.shape
    depth, _, past = cache_fox_k.shape[:3]
    assert past % CHUNK == 0 and ts <= CHUNK and sp % LANE == 0
    tq = _tile_rows(sp, 512)
    tms = _tile_rows(bs * ts, 512)

    cos_p, sin_p = _rope_tables(jnp.arange(sp))
    cos_s, sin_s = _rope_tables(past + jnp.arange(ts))
    cos_s, sin_s = jnp.tile(cos_s, (tms // ts, 1)), jnp.tile(sin_s, (tms // ts, 1))

    hp = x_prompt.reshape(bp * sp, D_MODEL)
    hs = x_sample.reshape(bs * ts, D_MODEL)
    rows_p, rows_s, conv_p, conv_s = None, None, [], []
    for l in range(depth):
        lams = jnp.stack([diff_lambda_q1[l], diff_lambda_k1[l], diff_lambda_q2[l], diff_lambda_k2[l]], axis=0)
        lw = _pack_layer(w_in[l], b_forget[l], mla_q_norm[l], w_mla_uq[l], mla_kv_norm[l], w_mla_uk[l], w_mla_uv[l],
                         lams, diff_subln[l], w_out[l], norm_mix_pre[l], norm_mix_post[l], norm_ffn_pre[l],
                         norm_ffn_post[l], norm_ple_pre[l], norm_ple_post[l], w_ffn_up[l], ffn_conv_w[l],
                         ffn_conv_b[l], w_ffn_down[l], w_ple_gate[l], w_ple_proj[l])
        lam_init = 0.8 - 0.6 * math.exp(-0.3 * l)
        hp, rows_p, cp = _layer(hp, (p_prompt.reshape(depth, bp * sp, PLE_DIM), l), lw, cos_p, sin_p, None,
                                jnp.zeros((bp, CONV_WIDTH - 1, D_FF), F32), lam_init, rows_p,
                                batch=bp, seq=sp, past=0, tq=tq, depth=depth)
        cache_l = (cache_fox_k[l], cache_fox_v[l], cache_fox_logf[l],
                   cache_mla_ckv.reshape(depth, bs * past, MLA_KV_RANK),
                   cache_mla_krope.reshape(depth, bs * past, MLA_ROPE_DIM), cache_diff_k[l], cache_diff_v[l])
        hs, rows_s, cs = _layer(hs, (p_sample.reshape(depth, bs * ts, PLE_DIM), l), lw, cos_s, sin_s, cache_l,
                                state_ffn_conv[l], lam_init, rows_s,
                                batch=bs, seq=ts, past=past, tq=LANE, depth=depth)
        conv_p.append(cp)
        conv_s.append(cs)

    def stack_rows(rows, batch, seq):
        fk, fv, lf, ckv, kr, dk, dv = rows
        lead = (depth, batch, seq)
        return (fk.reshape(lead + (N_FOX_HEADS, HEAD_DIM)), fv.reshape(lead + (N_FOX_HEADS, HEAD_DIM)),
                lf.reshape(lead + (N_FOX_HEADS,)), ckv.reshape(lead + (MLA_KV_RANK,)),
                kr.reshape(lead + (MLA_ROPE_DIM,)), dk.reshape(lead + (N_DIFF_HEADS, 2 * DIFF_QK_DIM)),
                dv.reshape(lead + (N_DIFF_HEADS, DIFF_V_DIM)))

    out_p = stack_rows(rows_p, bp, sp)
    out_s = stack_rows(rows_s, bs, ts)
    return ((hp.reshape(bp, sp, D_MODEL), hs.reshape(bs, ts, D_MODEL)) + out_p + (jnp.stack(conv_p, axis=0),)
            + out_s + (jnp.stack(conv_s, axis=0),))
```

```python
import functools
import math

import jax
import jax.numpy as jnp
import numpy as np
from jax import lax
from jax.experimental import pallas as pl
from jax.experimental.pallas import tpu as pltpu

F32 = jnp.float32
BF16 = jnp.bfloat16

D_MODEL = 1024
HEAD_DIM = 64
N_FOX_HEADS = 6
N_MLA_HEADS = 6
N_DIFF_HEADS = 4
MLA_Q_RANK = 384
MLA_KV_RANK = 256
MLA_NOPE_DIM = 64
MLA_ROPE_DIM = 32
MLA_V_DIM = 64
DIFF_QK_DIM = 32
DIFF_V_DIM = 64
FOX_WIDTH = N_FOX_HEADS * HEAD_DIM
MLA_WIDTH = N_MLA_HEADS * MLA_V_DIM
DIFF_WIDTH = N_DIFF_HEADS * DIFF_V_DIM
D_FF = 4 * D_MODEL
CONV_WIDTH = 3
PLE_DIM = 256
CHUNK = 64
ROPE_THETA = 10000.0
RMS_EPS = 1e-6
NEG_INF = -1e30
LOG2E = math.log2(math.e)

OFF_FOX_Q = 0
OFF_FOX_F = 3 * FOX_WIDTH
OFF_MLA_CQ = OFF_FOX_F + N_FOX_HEADS
OFF_MLA_CKV = OFF_MLA_CQ + MLA_Q_RANK
OFF_MLA_KR = OFF_MLA_CKV + MLA_KV_RANK
OFF_DIFF_Q = OFF_MLA_KR + MLA_ROPE_DIM
IN_WIDTH = OFF_DIFF_Q + 2 * N_DIFF_HEADS * 2 * DIFF_QK_DIM + DIFF_WIDTH

LANE = 128
V7X_VMEM_BYTES = 64 * 1024 * 1024
VMEM_LIMIT = (V7X_VMEM_BYTES * 7) // 8

TOKEN_TILE = 512
CACHE_TILE = 1024
FFN_STEP = 2048
Q_BLOCK = 512

C_FOX = 0
C_DIFF = C_FOX + 3 * FOX_WIDTH
C_CQ = C_DIFF + 3 * DIFF_WIDTH
C_CKV = C_CQ + MLA_Q_RANK
C_KR = C_CKV + MLA_KV_RANK
C_F = C_KR + 2 * LANE
W_ALL = C_F + LANE
N_MLA_PAIRS = N_MLA_HEADS // 2
MLA_PAIR_W = 2 * LANE
MLA_QK_W = N_MLA_PAIRS * MLA_PAIR_W

FOX_QSCALE = HEAD_DIM ** -0.5 * LOG2E
MLA_QSCALE = (MLA_NOPE_DIM + MLA_ROPE_DIM) ** -0.5 * LOG2E
DIFF_QSCALE = DIFF_QK_DIM ** -0.5 * LOG2E


def _cparams(sem):
    return pltpu.CompilerParams(dimension_semantics=sem, vmem_limit_bytes=VMEM_LIMIT)


def _rms(x, g):
    return x * lax.rsqrt(jnp.mean(x * x, axis=-1, keepdims=True) + RMS_EPS) * g


def _dot(a, b):
    return jnp.dot(a, b, preferred_element_type=F32)


def _dot_nt(a, b):
    return lax.dot_general(a, b, (((1,), (1,)), ((), ())), preferred_element_type=F32)


def _store_values_t(vt_ref, v):
    dv = LANE // 2
    ones = jnp.ones((ONES_ROWS, v.shape[0]), BF16)
    for p in range(v.shape[1] // LANE):
        t = v[:, p * LANE:(p + 1) * LANE].T.astype(BF16)
        for e in range(2):
            r0 = e * (dv + ONES_ROWS)
            vt_ref[0, p, 0, r0:r0 + dv, :] = t[e * dv:(e + 1) * dv, :]
            vt_ref[0, p, 0, r0 + dv:r0 + dv + ONES_ROWS, :] = ones


def _proj_kernel(h_ref, gpre_ref, w_ref, bf_ref, gq_ref, gkv_ref, wuq_ref, wukv_ref, cos_ref, sin_ref, *rest,
                 n_alias, emit_vt):
    (foxk_ref, foxv_ref, logf_ref, ckv_ref, krope_ref, diffk_ref, diffv_ref,
     foxbf_ref, diffbf_ref, mlaq_ref, mlak_ref, mlav_ref) = rest[n_alias:n_alias + 12]
    xn = _rms(h_ref[...], gpre_ref[...]).astype(BF16)
    cos = cos_ref[...]
    sin = sin_ref[...]

    z = _dot(xn, w_ref[:, C_FOX:C_FOX + 3 * FOX_WIDTH])
    foxbf_ref[:, :FOX_WIDTH] = (z[:, :FOX_WIDTH] * FOX_QSCALE).astype(BF16)
    foxbf_ref[:, FOX_WIDTH:] = z[:, FOX_WIDTH:].astype(BF16)
    foxk_ref[...] = z[:, FOX_WIDTH:2 * FOX_WIDTH]
    foxv_ref[...] = z[:, 2 * FOX_WIDTH:3 * FOX_WIDTH]
    if emit_vt:
        _store_values_t(rest[n_alias + 13], z[:, 2 * FOX_WIDTH:3 * FOX_WIDTH])

    z = _dot(xn, w_ref[:, C_DIFF:C_DIFF + 3 * DIFF_WIDTH])
    diffbf_ref[:, :DIFF_WIDTH] = (z[:, :DIFF_WIDTH] * DIFF_QSCALE).astype(BF16)
    diffbf_ref[:, DIFF_WIDTH:] = z[:, DIFF_WIDTH:].astype(BF16)
    diffk_ref[...] = z[:, DIFF_WIDTH:2 * DIFF_WIDTH]
    diffv_ref[...] = z[:, 2 * DIFF_WIDTH:3 * DIFF_WIDTH]
    if emit_vt:
        _store_values_t(rest[n_alias + 15], z[:, 2 * DIFF_WIDTH:3 * DIFF_WIDTH])

    cq = _rms(_dot(xn, w_ref[:, C_CQ:C_CQ + MLA_Q_RANK]), gq_ref[...]).astype(BF16)
    q2 = _dot(cq, wuq_ref[...])
    for p in range(N_MLA_PAIRS):
        lo = p * MLA_PAIR_W
        roped = q2[:, lo:lo + MLA_PAIR_W] * cos + q2[:, MLA_QK_W + lo:MLA_QK_W + lo + MLA_PAIR_W] * sin
        mlaq_ref[:, lo:lo + MLA_PAIR_W] = (roped * MLA_QSCALE).astype(BF16)

    ckv = _rms(_dot(xn, w_ref[:, C_CKV:C_CKV + MLA_KV_RANK]), gkv_ref[...])
    ckv_ref[...] = ckv
    kv = _dot(ckv.astype(BF16), wukv_ref[...])
    mlav_ref[...] = kv[:, MLA_WIDTH:2 * MLA_WIDTH].astype(BF16)
    if emit_vt:
        _store_values_t(rest[n_alias + 14], kv[:, MLA_WIDTH:2 * MLA_WIDTH])
    zkr =_dot(xn, w_ref[:, C_KR:C_KR + 2 * LANE])
    krp = zkr[:, :LANE] * cos[:, LANE:] + zkr[:, LANE:] * sin[:, LANE:]
    krope_ref[...] = krp[:, :MLA_ROPE_DIM]
    krp16 = krp.astype(BF16)
    for p in range(N_MLA_PAIRS):
        lo = p * MLA_PAIR_W
        mlak_ref[:, lo:lo + LANE] = kv[:, p * LANE:(p + 1) * LANE].astype(BF16)
        mlak_ref[:, lo + LANE:lo + 2 * LANE] = krp16

    zf = _dot(xn, w_ref[:, C_F:C_F + LANE]) + bf_ref[...]
    logf = -(jnp.maximum(-zf, 0.0) + jnp.log1p(jnp.exp(-jnp.abs(zf))))
    logf_ref[...] = logf[:, :N_FOX_HEADS]
    lane = lax.broadcasted_iota(jnp.int32, (1, LANE), 1)
    rest[n_alias + 12][...] = jnp.where(lane < N_FOX_HEADS, logf, 0.0)


N_ROW_OUTPUTS = 7


def _proj_call(h, lw, cos, sin, tm, layer, depth, prev_rows, vt_seq):
    n = h.shape[0]
    ntab = cos.shape[0] // tm
    row = lambda i: (i, 0)
    full = lambda i: (0, 0)
    tab = lambda i: (i % ntab, 0)
    in_specs = [
        pl.BlockSpec((tm, D_MODEL), row),
        pl.BlockSpec((1, D_MODEL), full),
        pl.BlockSpec((D_MODEL, W_ALL), full),
        pl.BlockSpec((1, LANE), full),
        pl.BlockSpec((1, MLA_Q_RANK), full),
        pl.BlockSpec((1, MLA_KV_RANK), full),
        pl.BlockSpec((MLA_Q_RANK, 2 * MLA_QK_W), full),
        pl.BlockSpec((MLA_KV_RANK, 2 * MLA_WIDTH), full),
        pl.BlockSpec((tm, MLA_PAIR_W), tab),
        pl.BlockSpec((tm, MLA_PAIR_W), tab),
    ]
    widths = [(FOX_WIDTH, F32), (FOX_WIDTH, F32), (N_FOX_HEADS, F32), (MLA_KV_RANK, F32), (MLA_ROPE_DIM, F32),
              (DIFF_WIDTH, F32), (DIFF_WIDTH, F32),
              (3 * FOX_WIDTH, BF16), (3 * DIFF_WIDTH, BF16), (MLA_QK_W, BF16), (MLA_QK_W, BF16), (MLA_WIDTH, BF16),
              (LANE, F32)]
    out_shape = ([jax.ShapeDtypeStruct((depth, n, w), dt) for w, dt in widths[:N_ROW_OUTPUTS]]
                 + [jax.ShapeDtypeStruct((n, w), dt) for w, dt in widths[N_ROW_OUTPUTS:]])
    out_specs = ([pl.BlockSpec((None, tm, w), lambda i: (layer, i, 0)) for w, _ in widths[:N_ROW_OUTPUTS]]
                 + [pl.BlockSpec((tm, w), row) for w, _ in widths[N_ROW_OUTPUTS:]])
    if vt_seq is not None:
        nblk = vt_seq // tm
        for width in (FOX_WIDTH, MLA_WIDTH, DIFF_WIDTH):
            shape = (n // vt_seq, width // LANE, nblk, 2 * (LANE // 2 + ONES_ROWS), tm)
            out_shape.append(jax.ShapeDtypeStruct(shape, BF16))
            out_specs.append(pl.BlockSpec((1,) + shape[1:2] + (1,) + shape[3:],
                                          lambda i: (i // nblk, 0, i % nblk, 0, 0)))
    args = [h, lw["g_mix_pre"], lw["w_all"], lw["b_f"], lw["g_q"], lw["g_kv"], lw["w_uq2"], lw["w_ukv"], cos, sin]
    aliases = {}
    if prev_rows is not None:
        aliases = {len(args) + k: k for k in range(N_ROW_OUTPUTS)}
        in_specs = in_specs + [pl.BlockSpec(memory_space=pl.ANY)] * N_ROW_OUTPUTS
        args = args + list(prev_rows)
    return pl.pallas_call(
        functools.partial(_proj_kernel, n_alias=len(aliases), emit_vt=vt_seq is not None), grid=(n // tm,),
        in_specs=in_specs,
        out_specs=out_specs, out_shape=out_shape, input_output_aliases=aliases,
        compiler_params=_cparams(("parallel",)), name="proj",
    )(*args)


def _kvup_kernel(ckv_ref, kr_ref, wukv_ref, place_ref, k_ref, v_ref):
    kv = _dot(ckv_ref[...].astype(BF16), wukv_ref[...])
    v_ref[...] = kv[:, MLA_WIDTH:].astype(BF16)
    krp = _dot(kr_ref[...].astype(BF16), place_ref[...]).astype(BF16)
    for p in range(N_MLA_PAIRS):
        lo = p * MLA_PAIR_W
        k_ref[:, lo:lo + LANE] = kv[:, p * LANE:(p + 1) * LANE].astype(BF16)
        k_ref[:, lo + LANE:lo + 2 * LANE] = krp


def _kvup_call(ckv_all, krope_all, layer, w_ukv, tm):
    n = ckv_all.shape[1]
    row = lambda i: (i, 0)
    lrow = lambda i: (layer, i, 0)
    full = lambda i: (0, 0)
    eye = jnp.eye(MLA_ROPE_DIM, dtype=BF16)
    place = jnp.concatenate([eye, eye, jnp.zeros((MLA_ROPE_DIM, LANE - 2 * MLA_ROPE_DIM), BF16)], axis=1)
    return pl.pallas_call(
        _kvup_kernel, grid=(n // tm,),
        in_specs=[pl.BlockSpec((None, tm, MLA_KV_RANK), lrow), pl.BlockSpec((None, tm, MLA_ROPE_DIM), lrow),
                  pl.BlockSpec((MLA_KV_RANK, 2 * MLA_WIDTH), full), pl.BlockSpec((MLA_ROPE_DIM, LANE), full)],
        out_specs=[pl.BlockSpec((tm, MLA_QK_W), row), pl.BlockSpec((tm, MLA_WIDTH), row)],
        out_shape=[jax.ShapeDtypeStruct((n, MLA_QK_W), BF16), jax.ShapeDtypeStruct((n, MLA_WIDTH), BF16)],
        compiler_params=_cparams(("parallel",)), name="kvup",
    )(ckv_all, krope_all, w_ukv, place)


AUG_STRIDE = 8


def _keep_bf16_bits(x):
    bits = lax.bitcast_convert_type(x, jnp.uint32) & jnp.uint32(0xFFFF0000)
    return lax.bitcast_convert_type(bits, F32)


def _split3(x):
    x1 = _keep_bf16_bits(x)
    r = x - x1
    x2 = _keep_bf16_bits(r)
    return x1, x2, r - x2


def _place3(x1, x2, x3):
    lane = lax.broadcasted_iota(jnp.int32, (1, LANE), 1)
    return jnp.where(lane < AUG_STRIDE, x1,
                     jnp.where(lane < 2 * AUG_STRIDE, pltpu.roll(x2, AUG_STRIDE, 1),
                               pltpu.roll(x3, 2 * AUG_STRIDE, 1))).astype(BF16)


def _aug_selector(h, rows):
    lane = lax.broadcasted_iota(jnp.int32, (rows, LANE), 1)
    hit = (lane == h) | (lane == AUG_STRIDE + h) | (lane == 2 * AUG_STRIDE + h)
    return jnp.where(hit, 1.0, 0.0).astype(BF16)


def _forget_bias_kernel(x_ref, tri_ref, o_ref, carry_sc, *, tb):
    @pl.when(pl.program_id(1) == 0)
    def _():
        carry_sc[...] = jnp.zeros_like(carry_sc)

    terms = jnp.concatenate(_split3(x_ref[0]), axis=1).astype(BF16)
    sums = _dot(tri_ref[...], terms)
    cs = carry_sc[0:1, :] + sums[:, :LANE] + sums[:, LANE:2 * LANE] + sums[:, 2 * LANE:]
    carry_sc[...] = jnp.broadcast_to(cs[tb - 1:tb, :], carry_sc.shape)
    o_ref[0] = _place3(*_split3(cs * (-LOG2E)))


def _forget_bias_call(logf):
    b, t, _ = logf.shape
    tb = max(d for d in range(LANE, 4 * LANE + 1, LANE) if t % d == 0)
    x = logf
    tri = jnp.tril(jnp.ones((tb, tb), BF16))
    blk = pl.BlockSpec((1, tb, LANE), lambda i, j: (i, j, 0))
    return pl.pallas_call(
        functools.partial(_forget_bias_kernel, tb=tb), grid=(b, t // tb),
        in_specs=[blk, pl.BlockSpec((tb, tb), lambda i, j: (0, 0))], out_specs=blk,
        out_shape=jax.ShapeDtypeStruct((b, t, LANE), BF16),
        scratch_shapes=[pltpu.VMEM((8, LANE), F32)],
        compiler_params=_cparams(("parallel", "arbitrary")), name="forget_bias",
    )(x, tri)


def _alibi_bias_block(t_k):
    slopes = 2.0 ** (-8.0 * np.arange(1, N_DIFF_HEADS + 1) / N_DIFF_HEADS)
    b = jnp.zeros((t_k, LANE), F32).at[:, :N_DIFF_HEADS].set(
        jnp.arange(t_k, dtype=F32)[:, None] * jnp.asarray(slopes * LOG2E, F32)[None, :])
    x1, x2, x3 = _split3(b)
    lane = jnp.arange(LANE)[None, :]
    placed = jnp.where(lane < AUG_STRIDE, x1, jnp.where(lane < 2 * AUG_STRIDE, jnp.roll(x2, AUG_STRIDE, 1),
                                                        jnp.roll(x3, 2 * AUG_STRIDE, 1)))
    return placed.astype(BF16)[None]


ROW_CHUNK = CHUNK
ONES_ROWS = 16


def _masked_chunks(s_ref, c, mask, tk, tq):
    chunks = [(r0, "all" if mask is None else mask[0](r0, 0, tq)) for r0 in range(0, tk, ROW_CHUNK)]

    def logits(r0, vis):
        x = s_ref[c, r0:r0 + ROW_CHUNK, :]
        return x if vis == "all" else mask[1](r0, 0, x)
    return chunks, logits


def _softmax_max(s_ref, m_ref, c, mask, tk, tq):
    chunks, logits = _masked_chunks(s_ref, c, mask, tk, tq)
    mrun = jnp.full((8, tq), NEG_INF, F32)
    for r0, vis in chunks:
        if vis != "none":
            mrun = jnp.maximum(mrun, jnp.max(logits(r0, vis).reshape(ROW_CHUNK // 8, 8, tq), axis=0))
    m_old = m_ref[c]
    return m_old, jnp.maximum(m_old, jnp.max(mrun, axis=0, keepdims=True))


def _softmax_exp(s_ref, p_ref, m_ref, alpha_ref, c, mask, tk, tq, m_old, m_new):
    chunks, logits = _masked_chunks(s_ref, c, mask, tk, tq)
    for r0, vis in chunks:
        if vis == "none":
            p_ref[c, r0:r0 + ROW_CHUNK, :] = jnp.zeros((ROW_CHUNK, tq), BF16)
        else:
            p_ref[c, r0:r0 + ROW_CHUNK, :] = jnp.exp2(logits(r0, vis) - m_new).astype(BF16)
    m_ref[c] = m_new
    alpha_ref[c] = jnp.exp2(m_old - m_new)


def _apply_values(p_ref, alpha_ref, acc_ref, c, vt):
    acc_ref[c] = alpha_ref[c] * acc_ref[c] + _dot(vt, p_ref[c])


def _causal_mask(qoff):
    def visibility(r0, c0, sw):
        if r0 + ROW_CHUNK - 1 <= c0 + qoff:
            return "all"
        return "none" if r0 > c0 + sw - 1 + qoff else "some"

    def apply(r0, c0, x):
        kk, qq = _key_query_iota(*x.shape)
        return jnp.where(kk - qq <= c0 + qoff - r0, x, NEG_INF)
    return visibility, apply


def _chunk_visibility(r0, c0, sw, qoff):
    if r0 // CHUNK <= (c0 + qoff) // CHUNK:
        return "all"
    return "none" if r0 // CHUNK > (c0 + sw - 1 + qoff) // CHUNK else "some"


def _chunk_mask(r0, c0, sw, qoff):
    qq = lax.broadcasted_iota(jnp.int32, (1, sw), 1)
    return r0 // CHUNK <= (qq + (c0 + qoff)) // CHUNK


def _init_states(m_ref, acc_ref):
    m_ref[...] = jnp.full(m_ref.shape, NEG_INF, F32)
    acc_ref[...] = jnp.zeros(acc_ref.shape, F32)


def _normalized(acc_ref, c, dv):
    return acc_ref[c, :dv, :] / acc_ref[c, dv:dv + 1, :]


def _key_query_iota(tk, tq):
    return (lax.broadcasted_iota(jnp.int32, (tk, tq), 0), lax.broadcasted_iota(jnp.int32, (tk, tq), 1))


def _store_pair(o_ref, o0, o1, row_scale=None):
    out = jnp.concatenate([o0, o1], axis=0).T
    if row_scale is not None:
        out = out * row_scale
    o_ref[0] = out.astype(BF16)


def _block_start(j, size):
    return j * size if isinstance(j, int) else pl.multiple_of(j * size, size)


N_BUF = 3


def _attend(i, blocked, chains, qk_fn, max_fn, exp_fn, pv_fn):
    def step(qk, pv, buf, diag):
        if qk is not None:
            all_chains(qk_fn, *qk)
        if pv is not None:
            all_chains(pv_fn, *pv)
        for c in range(chains):
            exp_fn(buf, c, diag, *max_fn(buf, c, diag))

    def all_chains(fn, *args):
        for c in range(chains):
            fn(*args, c)

    if not blocked:
        all_chains(qk_fn, 0, 0)
        step(None, None, 0, True)
        all_chains(pv_fn, 0, 0)
        return
    last = N_BUF - 1
    lead = i % N_BUF

    def single(j, carry):
        all_chains(qk_fn, last, j)
        step(None, None, last, False)
        all_chains(pv_fn, last, j)
        return carry

    lax.fori_loop(0, lead, single, 0)
    all_chains(qk_fn, 0, lead)
    all_chains(qk_fn, 1, jnp.minimum(lead + 1, i))

    def rotate(t, carry):
        b0 = lead + N_BUF * t
        for u in range(N_BUF):
            step(((u + 2) % N_BUF, jnp.minimum(b0 + u + 2, i)), (u - 1, b0 + u - 1) if u > 0 else None, u, False)
        all_chains(pv_fn, last, b0 + last)
        return carry

    lax.fori_loop(0, (i - lead) // N_BUF, rotate, 0)
    step(None, None, 0, True)
    all_chains(pv_fn, 0, i)


def _vt_rows(vt_ref, j, e, dv):
    return vt_ref[0, 0, j, e * (dv + ONES_ROWS):(e + 1) * (dv + ONES_ROWS), :]


def _split_scratch(scratch):
    s_all, p_all = scratch[:2]
    nbuf = s_all.shape[0]
    return ([s_all.at[b] for b in range(nbuf)], [p_all.at[b] for b in range(nbuf)]) + tuple(scratch[2:])


def _chain_fns(scratch, vt_ref, mask_of, dv, head_of, tkd, tq):
    s_sc, p_sc, m_sc, alpha_sc, acc_sc = _split_scratch(scratch)

    def max_fn(buf, c, diag):
        return _softmax_max(s_sc[buf], m_sc, c, mask_of(c) if diag else None, tkd, tq)

    def exp_fn(buf, c, diag, m_old, m_new):
        _softmax_exp(s_sc[buf], p_sc[buf], m_sc, alpha_sc, c, mask_of(c) if diag else None, tkd, tq, m_old, m_new)

    def pv_fn(buf, j, c):
        _apply_values(p_sc[buf], alpha_sc, acc_sc, c, _vt_rows(vt_ref, j, head_of(c), dv))
    return max_fn, exp_fn, pv_fn


def _fox_kernel(q_ref, k_ref, aug_ref, vt_ref, o_ref, *scratch, tq, tkd, qoff, blocked):
    s_sc, p_sc, m_sc, alpha_sc, acc_sc = _split_scratch(scratch)
    hp, i = pl.program_id(1), pl.program_id(2)
    _init_states(m_sc, acc_sc)
    q = q_ref[0]
    lane = lax.broadcasted_iota(jnp.int32, (1, LANE), 1)
    qs = [jnp.concatenate([jnp.where((lane >= e * HEAD_DIM) & (lane < (e + 1) * HEAD_DIM), q, jnp.zeros_like(q)),
                           _aug_selector(2 * hp + e, tq)], axis=1) for e in range(2)]

    def qk_fn(buf, j, e):
        rows = pl.ds(_block_start(j, tq), tkd)
        k = jnp.concatenate([k_ref[0, rows, :], aug_ref[0, rows, :]], axis=1)
        s_sc[buf][e] = _dot_nt(k, qs[e])

    _attend(i, blocked, 2, qk_fn, *_chain_fns(scratch, vt_ref, lambda e: _causal_mask(qoff), HEAD_DIM,
                                              lambda e: e, tkd, tq))
    _store_pair(o_ref, *(_normalized(acc_sc, e, HEAD_DIM) for e in range(2)))


def _mla_kernel(q_ref, k_ref, vt_ref, o_ref, *scratch, tq, tkd, qoff, blocked):
    s_sc, p_sc, m_sc, alpha_sc, acc_sc = _split_scratch(scratch)
    i = pl.program_id(2)
    _init_states(m_sc, acc_sc)
    q = q_ref[0]
    lane2 = lax.broadcasted_iota(jnp.int32, (1, MLA_PAIR_W), 1)
    sel = []
    for e in range(2):
        nope = (lane2 >= e * MLA_NOPE_DIM) & (lane2 < (e + 1) * MLA_NOPE_DIM)
        rope = (lane2 >= LANE + e * MLA_ROPE_DIM) & (lane2 < LANE + (e + 1) * MLA_ROPE_DIM)
        sel.append(jnp.where(nope | rope, q, jnp.zeros_like(q)))

    def qk_fn(buf, j, e):
        s_sc[buf][e] = _dot_nt(k_ref[0, pl.ds(_block_start(j, tq), tkd), :], sel[e])

    mask = (lambda r0, c0, sw: _chunk_visibility(r0, c0, sw, qoff),
            lambda r0, c0, x: jnp.where(_chunk_mask(r0, c0, x.shape[1], qoff), x, NEG_INF))
    _attend(i, blocked, 2, qk_fn, *_chain_fns(scratch, vt_ref, lambda e: mask, MLA_V_DIM, lambda e: e, tkd, tq))
    _store_pair(o_ref, *(_normalized(acc_sc, e, MLA_V_DIM) for e in range(2)))


def _diff_kernel(q_ref, k_ref, aug_ref, vt_ref, slope_ref, lam_ref, subln_ref, o_ref, *scratch,
                 tq, tkd, qoff, blocked, lam_init):
    s_sc, p_sc, m_sc, alpha_sc, acc_sc = _split_scratch(scratch)
    hp, i = pl.program_id(1), pl.program_id(2)
    _init_states(m_sc, acc_sc)
    q = q_ref[0]
    lane = lax.broadcasted_iota(jnp.int32, (1, LANE), 1)
    sel = [jnp.concatenate([jnp.where((lane >= (2 * e + t) * DIFF_QK_DIM) & (lane < (2 * e + t + 1) * DIFF_QK_DIM),
                                      q, jnp.zeros_like(q)), _aug_selector(2 * hp + e, tq)], axis=1)
           for e in range(2) for t in range(2)]
    slopes = [slope_ref[0, e:e + 1, 0:1] * LOG2E for e in range(2)]

    def qk_fn(buf, j, c):
        rows = pl.ds(_block_start(j, tq), tkd)
        k = jnp.concatenate([k_ref[0, rows, :], aug_ref[0, rows, :]], axis=1)
        s_sc[buf][c] = _dot_nt(k, sel[c])

    def mask_of(c):
        def visibility(r0, c0, sw):
            vis = _chunk_visibility(r0, c0, sw, qoff)
            return "some" if vis == "all" and r0 + ROW_CHUNK - 1 > c0 + qoff else vis

        def apply(r0, c0, x):
            kk, qq = _key_query_iota(*x.shape)
            ahead = jnp.maximum(kk - qq + (r0 - c0 - qoff), 0).astype(F32)
            return jnp.where(_chunk_mask(r0, c0, x.shape[1], qoff), x - (2.0 * slopes[c // 2]) * ahead, NEG_INF)
        return visibility, apply

    _attend(i, blocked, 4, qk_fn, *_chain_fns(scratch, vt_ref, mask_of, DIFF_V_DIM, lambda c: c // 2, tkd, tq))
    res = [_normalized(acc_sc, c, DIFF_V_DIM) for c in range(4)]
    lq = lam_ref[...]
    lam = (jnp.exp(jnp.sum(lq[0:1] * lq[1:2], axis=-1, keepdims=True))
           - jnp.exp(jnp.sum(lq[2:3] * lq[3:4], axis=-1, keepdims=True)) + lam_init)
    outs = []
    for e in range(2):
        o = res[2 * e] - lam * res[2 * e + 1]
        outs.append(o * lax.rsqrt(jnp.mean(o * o, axis=0, keepdims=True) + RMS_EPS))
    _store_pair(o_ref, outs[0], outs[1], subln_ref[...] * (1.0 - lam_init))


def _attn_call(kernel, name, q, q_cb, k, k_cb, aug, vt, extras, extra_specs, *, npairs, qw, tq, tkd, qoff, blocked,
               chains, **kw):
    b, t_q = q.shape[0], q.shape[1]
    t_k = k.shape[1]
    acc_rows = LANE // 2 + ONES_ROWS
    nbuf = N_BUF if blocked else 1
    scratch = [pltpu.VMEM((nbuf, chains, tkd, tq), F32), pltpu.VMEM((nbuf, chains, tkd, tq), BF16),
               pltpu.VMEM((chains, 1, tq), F32), pltpu.VMEM((chains, 1, tq), F32),
               pltpu.VMEM((chains, acc_rows, tq), F32)]
    in_specs = [pl.BlockSpec((1, tq, qw), lambda bb, hp, i: (bb, i, q_cb + hp)),
                pl.BlockSpec((1, t_k, qw), lambda bb, hp, i: (bb, 0, k_cb + hp))]
    args = [q, k]
    if aug is not None:
        per_batch = aug.shape[0] > 1
        in_specs.append(pl.BlockSpec((1, t_k, LANE), lambda bb, hp, i: (bb if per_batch else 0, 0, 0)))
        args.append(aug)
    in_specs.append(pl.BlockSpec((1, 1) + vt.shape[2:], lambda bb, hp, i: (bb, hp, 0, 0, 0)))
    return pl.pallas_call(
        functools.partial(kernel, tq=tq, tkd=tkd, qoff=qoff, blocked=blocked, **kw),
        grid=(b, npairs, t_q // tq), in_specs=in_specs + extra_specs,
        out_specs=pl.BlockSpec((1, tq, LANE), lambda bb, hp, i: (bb, i, hp)),
        out_shape=jax.ShapeDtypeStruct((b, t_q, npairs * LANE), BF16), scratch_shapes=scratch,
        compiler_params=_cparams(("parallel", "parallel", "arbitrary")), name=name,
    )(*args, vt, *extras)


def _values_t(v, tkb):
    b, t, w = v.shape
    dv = LANE // 2
    vt = v.reshape(b, t // tkb, tkb, w // LANE, 2, dv).transpose(0, 3, 1, 4, 5, 2)
    ones = jnp.ones(vt.shape[:4] + (ONES_ROWS, tkb), v.dtype)
    return jnp.concatenate([vt, ones], axis=4).reshape(b, w // LANE, t // tkb, 2 * (dv + ONES_ROWS), tkb)


def _values_t_cached(cached, new, tk):
    b, p, hh, dv = cached.shape
    t = new.shape[1]
    vt = jnp.concatenate([jnp.transpose(cached, (0, 2, 3, 1)).astype(BF16),
                          jnp.transpose(new.reshape(b, t, hh, dv), (0, 2, 3, 1))], axis=-1)
    vt = jnp.pad(vt, ((0, 0), (0, 0), (0, 0), (0, tk - p - t)))
    ones = jnp.ones((b, hh, ONES_ROWS, tk), BF16)
    return jnp.concatenate([vt, ones], axis=2).reshape(b, hh // 2, 1, 2 * (dv + ONES_ROWS), tk)


FFN_HALVES = 4


def _ffn_kernel(oa_ref, ob_ref, oc_ref, h_ref, wout_ref, gmix_ref,
                gpre_ref, wg_ref, wv_ref, cw_ref, cb_ref, wd_ref, gpost_ref, left_ref,
                p_ref, gple_ref, wgate_ref, wproj_ref, gplepost_ref,
                o_ref, st_ref, h1_sc, xn_sc, acc_sc, carry_sc, *, tm, tf, nsb, seq_blocks):
    i = pl.program_id(0)
    f = pl.program_id(1)
    nf = pl.num_programs(1)
    tb = tm // nsb

    @pl.when(f == 0)
    def _():
        y = (_dot(oa_ref[...], wout_ref[0:FOX_WIDTH, :])
             + _dot(ob_ref[...], wout_ref[FOX_WIDTH:FOX_WIDTH + MLA_WIDTH, :])
             + _dot(oc_ref[...], wout_ref[FOX_WIDTH + MLA_WIDTH:, :]))
        h1 = h_ref[...] + _rms(y, gmix_ref[...])
        h1_sc[...] = h1
        xn_sc[...] = _rms(h1, gpre_ref[...]).astype(BF16)
        acc_sc[...] = jnp.zeros_like(acc_sc)

    xn = xn_sc[...]
    th = tf // FFN_HALVES
    halves = [slice(a * th, (a + 1) * th) for a in range(FFN_HALVES)]
    ups = [(_dot(xn, wg_ref[:, cols]), _dot(xn, wv_ref[:, cols])) for cols in halves]
    rin = lax.broadcasted_iota(jnp.int32, (tm, 1), 0) & (tb - 1)

    def spread(rows):
        return jnp.broadcast_to(rows, (nsb, tb, th)).reshape(tm, th)

    for cols, (gate, val) in zip(halves, ups):
        left = left_ref[:, :, cols]
        if seq_blocks > 1:
            left = jnp.where(i % seq_blocks == 0, left, carry_sc[f, 0:CONV_WIDTH - 1, cols][None])
            carry_sc[f, 0:CONV_WIDTH - 1, cols] = gate[tm - (CONV_WIDTH - 1):, :]
        st_ref[f, pl.ds((i // seq_blocks) * nsb, nsb), :, cols] = (
            gate.reshape(nsb, tb, th)[:, tb - (CONV_WIDTH - 1):, :])
        l0 = spread(left[:, 0:1, :])
        l1 = spread(left[:, 1:2, :])
        g1 = jnp.where(rin == 0, l1, pltpu.roll(gate, 1, 0))
        g2 = jnp.where(rin == 0, l0, jnp.where(rin == 1, l1, pltpu.roll(gate, 2, 0)))
        conv = cw_ref[0:1, cols] * g2 + cw_ref[1:2, cols] * g1 + cw_ref[2:3, cols] * gate + cb_ref[:, cols]
        gelu = 0.5 * conv * (1.0 + jnp.tanh(math.sqrt(2.0 / math.pi) * (conv + 0.044715 * (conv * conv * conv))))
        acc_sc[...] += _dot((gelu * val).astype(BF16), wd_ref[cols, :])

    @pl.when(f == nf - 1)
    def _():
        h2 = h1_sc[...] + _rms(acc_sc[...], gpost_ref[...])
        gate = jax.nn.sigmoid(_dot(_rms(h2, gple_ref[...]).astype(BF16), wgate_ref[...]))
        proj = _dot(p_ref[...].astype(BF16), wproj_ref[...])
        o_ref[...] = h2 + _rms(proj * gate, gplepost_ref[...])


def _ffn_call(oa, ob, oc, h, p, lw, left, seq_len, tm, tf):
    n = h.shape[0]
    p_all, p_layer = p
    nseq = left.shape[0]
    nf = D_FF // tf
    if seq_len >= tm:
        nsb, seq_blocks = 1, seq_len // tm
    else:
        nsb, seq_blocks = tm // seq_len, 1
    assert (tm // nsb) & (tm // nsb - 1) == 0
    row = lambda i, f: (i, 0)
    full = lambda i, f: (0, 0)
    out, state = pl.pallas_call(
        functools.partial(_ffn_kernel, tm=tm, tf=tf, nsb=nsb, seq_blocks=seq_blocks),
        grid=(n // tm, nf),
        in_specs=[pl.BlockSpec((tm, FOX_WIDTH), row), pl.BlockSpec((tm, MLA_WIDTH), row),
                  pl.BlockSpec((tm, DIFF_WIDTH), row), pl.BlockSpec((tm, D_MODEL), row),
                  pl.BlockSpec((D_MODEL, D_MODEL), full), pl.BlockSpec((1, D_MODEL), full),
                  pl.BlockSpec((1, D_MODEL), full),
                  pl.BlockSpec((D_MODEL, tf), lambda i, f: (0, f)),
                  pl.BlockSpec((D_MODEL, tf), lambda i, f: (0, nf + f)),
                  pl.BlockSpec((CONV_WIDTH, tf), lambda i, f: (0, f)),
                  pl.BlockSpec((1, tf), lambda i, f: (0, f)),
                  pl.BlockSpec((tf, D_MODEL), lambda i, f: (f, 0)),
                  pl.BlockSpec((1, D_MODEL), full),
                  pl.BlockSpec((nsb, CONV_WIDTH - 1, tf), lambda i, f: (i // seq_blocks, 0, f)),
                  pl.BlockSpec((None, tm, PLE_DIM), lambda i, f: (p_layer, i, 0)), pl.BlockSpec((1, D_MODEL), full),
                  pl.BlockSpec((D_MODEL, D_MODEL), full), pl.BlockSpec((PLE_DIM, D_MODEL), full),
                  pl.BlockSpec((1, D_MODEL), full)],
        out_specs=[pl.BlockSpec((tm, D_MODEL), row),
                   pl.BlockSpec((nf, nseq, CONV_WIDTH - 1, tf), lambda i, f: (0, 0, 0, 0))],
        out_shape=[jax.ShapeDtypeStruct((n, D_MODEL), F32),
                   jax.ShapeDtypeStruct((nf, nseq, CONV_WIDTH - 1, tf), F32)],
        scratch_shapes=[pltpu.VMEM((tm, D_MODEL), F32), pltpu.VMEM((tm, D_MODEL), BF16),
                        pltpu.VMEM((tm, D_MODEL), F32), pltpu.VMEM((nf, 8, tf), F32)],
        compiler_params=_cparams(("arbitrary", "arbitrary")), name="ffn",
    )(oa, ob, oc, h, lw["w_out"], lw["g_mix_post"],
      lw["g_ffn_pre"], lw["w_up"], lw["w_up"], lw["conv_w"], lw["conv_b"], lw["w_down"], lw["g_ffn_post"], left,
      p_all, lw["g_ple_pre"], lw["w_ple_gate"], lw["w_ple_proj"], lw["g_ple_post"])
    return out, state.transpose(1, 2, 0, 3).reshape(nseq, CONV_WIDTH - 1, D_FF)


def _swap_halves(w):
    half = MLA_ROPE_DIM // 2
    return jnp.concatenate([w[..., half:], w[..., :half]], axis=-1)


def _pack_layer(w_in, b_forget, mla_q_norm, w_mla_uq, mla_kv_norm, w_mla_uk, w_mla_uv, lams, diff_subln, w_out,
                norm_mix_pre, norm_mix_post, norm_ffn_pre, norm_ffn_post, norm_ple_pre, norm_ple_post,
                w_ffn_up, ffn_conv_w, ffn_conv_b, w_ffn_down, w_ple_gate, w_ple_proj):
    zeros = lambda r, c: jnp.zeros((r, c), F32)
    w_kr = w_in[:, OFF_MLA_KR:OFF_DIFF_Q]
    w_all = jnp.concatenate([
        w_in[:, OFF_FOX_Q:OFF_FOX_F], w_in[:, OFF_DIFF_Q:IN_WIDTH],
        w_in[:, OFF_MLA_CQ:OFF_MLA_CKV], w_in[:, OFF_MLA_CKV:OFF_MLA_KR],
        w_kr, w_kr, zeros(D_MODEL, LANE - 2 * MLA_ROPE_DIM),
        _swap_halves(w_kr), _swap_halves(w_kr), zeros(D_MODEL, LANE - 2 * MLA_ROPE_DIM),
        w_in[:, OFF_FOX_F:OFF_MLA_CQ], zeros(D_MODEL, LANE - N_FOX_HEADS)], axis=1).astype(BF16)
    wq = w_mla_uq.reshape(MLA_Q_RANK, N_MLA_HEADS, MLA_NOPE_DIM + MLA_ROPE_DIM)
    plain, swapped = [], []
    pad = zeros(MLA_Q_RANK, MLA_PAIR_W - 2 * (MLA_NOPE_DIM + MLA_ROPE_DIM))
    for p in range(N_MLA_PAIRS):
        a, b = 2 * p, 2 * p + 1
        plain += [wq[:, a, :MLA_NOPE_DIM], wq[:, b, :MLA_NOPE_DIM], wq[:, a, MLA_NOPE_DIM:], wq[:, b, MLA_NOPE_DIM:], pad]
        swapped += [zeros(MLA_Q_RANK, 2 * MLA_NOPE_DIM), _swap_halves(wq[:, a, MLA_NOPE_DIM:]),
                    _swap_halves(wq[:, b, MLA_NOPE_DIM:]), pad]
    return dict(
        w_all=w_all,
        b_f=jnp.pad(b_forget, (0, LANE - N_FOX_HEADS)).reshape(1, LANE),
        g_q=mla_q_norm.reshape(1, -1), g_kv=mla_kv_norm.reshape(1, -1),
        w_uq2=jnp.concatenate(plain + swapped, axis=1).astype(BF16),
        w_ukv=jnp.concatenate([w_mla_uk, w_mla_uv], axis=1).astype(BF16),
        lams=lams, subln=jnp.tile(diff_subln, 2).reshape(1, LANE),
        w_out=w_out.astype(BF16),
        g_mix_pre=norm_mix_pre.reshape(1, -1), g_mix_post=norm_mix_post.reshape(1, -1),
        g_ffn_pre=norm_ffn_pre.reshape(1, -1), g_ffn_post=norm_ffn_post.reshape(1, -1),
        g_ple_pre=norm_ple_pre.reshape(1, -1), g_ple_post=norm_ple_post.reshape(1, -1),
        w_up=w_ffn_up.astype(BF16), conv_w=ffn_conv_w, conv_b=ffn_conv_b.reshape(1, -1),
        w_down=w_ffn_down.astype(BF16), w_ple_gate=w_ple_gate.astype(BF16), w_ple_proj=w_ple_proj.astype(BF16))


def _rope_tables(pos):
    half = MLA_ROPE_DIM // 2
    inv_freq = ROPE_THETA ** (-jnp.arange(half, dtype=F32) / half)
    ang = pos.astype(F32)[:, None] * inv_freq[None, :]
    cos, sin = jnp.cos(ang), jnp.sin(ang)
    t = pos.shape[0]
    pad = jnp.zeros((t, MLA_PAIR_W - LANE - 2 * MLA_ROPE_DIM), F32)
    cos_t = jnp.concatenate([jnp.ones((t, LANE), F32), cos, cos, cos, cos, pad], axis=1)
    sin_t = jnp.concatenate([jnp.zeros((t, LANE), F32), -sin, sin, -sin, sin, pad], axis=1)
    return cos_t, sin_t


def _alibi_slopes():
    s = 2.0 ** (-8.0 * np.arange(1, N_DIFF_HEADS + 1) / N_DIFF_HEADS)
    return jnp.asarray(np.broadcast_to(s.reshape(N_DIFF_HEADS // 2, 2, 1), (N_DIFF_HEADS // 2, 2, LANE)), dtype=F32)


def _tile_rows(n, pref):
    t = min(n, pref)
    assert n % t == 0
    return t


def _layer(h, p, lw, cos, sin, cache, conv_left, lam_init, prev_rows, *, batch, seq, past, tq, depth):
    n = h.shape[0]
    tm = _tile_rows(n, TOKEN_TILE)
    layer = p[1]
    emit_vt = cache is None and tm == tq
    outs = _proj_call(h, lw, cos, sin, tm, layer, depth, prev_rows, seq if emit_vt else None)
    rows = tuple(outs[:N_ROW_OUTPUTS])
    fox_bf, diff_bf, mla_q, mla_k, mla_v, logf = outs[N_ROW_OUTPUTS:N_ROW_OUTPUTS + 6]
    b3 = lambda a: a.reshape(batch, seq, a.shape[-1])
    diff_extras = [_alibi_slopes(), lw["lams"], lw["subln"]]
    diff_specs = [pl.BlockSpec((1, 2, LANE), lambda bb, hp, i: (hp, 0, 0)),
                  pl.BlockSpec((4, DIFF_QK_DIM), lambda bb, hp, i: (0, 0)),
                  pl.BlockSpec((1, LANE), lambda bb, hp, i: (0, 0))]
    fb, db = b3(fox_bf), b3(diff_bf)
    if cache is None:
        common = dict(tq=tq, tkd=tq, qoff=0, blocked=True)
        if emit_vt:
            vt_fox, vt_mla, vt_diff = outs[N_ROW_OUTPUTS + 6:]
        else:
            vt_fox, vt_mla, vt_diff = (_values_t(fb[..., 2 * FOX_WIDTH:], tq), _values_t(b3(mla_v), tq),
                                       _values_t(db[..., 2 * DIFF_WIDTH:], tq))
        o_a = _attn_call(_fox_kernel, "fox_attn", fb, 0, fb, 3, _forget_bias_call(b3(logf)), vt_fox, [], [],
                         npairs=N_FOX_HEADS // 2, qw=LANE, chains=2, **common)
        o_b = _attn_call(_mla_kernel, "mla_attn", b3(mla_q), 0, b3(mla_k), 0, None, vt_mla, [], [],
                         npairs=N_MLA_PAIRS, qw=MLA_PAIR_W, chains=2, **common)
        o_c = _attn_call(_diff_kernel, "diff_attn", db, 0, db, 2, _alibi_bias_block(seq), vt_diff,
                         diff_extras, diff_specs,
                         npairs=N_DIFF_HEADS // 2, qw=LANE, chains=4, lam_init=lam_init, **common)
    else:
        c_fox_k, c_fox_v, c_logf, c_ckv, c_krope, c_diff_k, c_diff_v = cache
        tk = -(-(past + seq) // LANE) * LANE
        padk = lambda a: jnp.pad(a, ((0, 0), (0, tk - past - seq), (0, 0)))
        cat = lambda old, new: padk(jnp.concatenate([old.astype(new.dtype), new], axis=1))
        flat = lambda a: lax.optimization_barrier(a.reshape(batch, past, -1))
        padq = lambda a: jnp.pad(a, ((0, 0), (0, tq - seq), (0, 0)))
        kf = cat(flat(c_fox_k), fb[..., FOX_WIDTH:2 * FOX_WIDTH])
        vf = _values_t_cached(c_fox_v, fb[..., 2 * FOX_WIDTH:], tk)
        kd = cat(flat(c_diff_k), db[..., DIFF_WIDTH:2 * DIFF_WIDTH])
        vd = _values_t_cached(c_diff_v, db[..., 2 * DIFF_WIDTH:], tk)
        km_c, v_c = _kvup_call(c_ckv, c_krope, layer, lw["w_ukv"], _tile_rows(batch * past, CACHE_TILE))
        km = cat(km_c.reshape(batch, past, MLA_QK_W), b3(mla_k))
        vm = cat(v_c.reshape(batch, past, MLA_WIDTH), b3(mla_v))
        common = dict(tq=tq, tkd=tk, qoff=past, blocked=False)
        o_a = _attn_call(_fox_kernel, "fox_attn_s", padq(fb[..., :FOX_WIDTH]), 0, kf, 0,
                         _forget_bias_call(cat(jnp.pad(c_logf, ((0, 0), (0, 0), (0, LANE - c_logf.shape[-1]))),
                                               b3(logf))), vf, [], [],
                         npairs=N_FOX_HEADS // 2, qw=LANE, chains=2, **common)
        o_b = _attn_call(_mla_kernel, "mla_attn_s", padq(b3(mla_q)), 0, km, 0, None, _values_t(vm, tk), [], [],
                         npairs=N_MLA_PAIRS, qw=MLA_PAIR_W, chains=2, **common)
        o_c = _attn_call(_diff_kernel, "diff_attn_s", padq(db[..., :DIFF_WIDTH]), 0, kd, 0, _alibi_bias_block(tk),
                         vd, diff_extras, diff_specs,
                         npairs=N_DIFF_HEADS // 2, qw=LANE, chains=4, lam_init=lam_init, **common)
        o_a, o_b, o_c = o_a[:, :seq], o_b[:, :seq], o_c[:, :seq]
    f2 = lambda a: a.reshape(n, a.shape[-1])
    h, conv_state = _ffn_call(f2(o_a), f2(o_b), f2(o_c), h, p, lw, conv_left, seq, tm, FFN_STEP)
    return h, rows, conv_state


def kernel(x_prompt, x_sample, cache_fox_k, cache_fox_v, cache_fox_logf, cache_mla_ckv, cache_mla_krope, cache_diff_k, cache_diff_v, state_ffn_conv, p_prompt, p_sample, w_in, b_forget, mla_q_norm, w_mla_uq, mla_kv_norm, w_mla_uk, w_mla_uv, diff_lambda_q1, diff_lambda_k1, diff_lambda_q2, diff_lambda_k2, diff_subln, w_out, norm_mix_pre, norm_mix_post, norm_ffn_pre, norm_ffn_post, norm_ple_pre, norm_ple_post, w_ffn_up, ffn_conv_w, ffn_conv_b, w_ffn_down, w_ple_gate, w_ple_proj):
    bp, sp, _ = x_prompt.shape
    bs, ts, _ = x_sample.shape
    depth, _, past = cache_fox_k.shape[:3]
    assert past % CHUNK == 0 and ts <= CHUNK and sp % LANE == 0
    tq = _tile_rows(sp, Q_BLOCK)
    tms = _tile_rows(bs * ts, TOKEN_TILE)

    cos_p, sin_p = _rope_tables(jnp.arange(sp))
    cos_s, sin_s = _rope_tables(past + jnp.arange(ts))
    cos_s, sin_s = jnp.tile(cos_s, (tms // ts, 1)), jnp.tile(sin_s, (tms // ts, 1))

    hp = x_prompt.reshape(bp * sp, D_MODEL)
    hs = x_sample.reshape(bs * ts, D_MODEL)
    rows_p, rows_s, conv_p, conv_s = None, None, [], []
    for l in range(depth):
        lams = jnp.stack([diff_lambda_q1[l], diff_lambda_k1[l], diff_lambda_q2[l], diff_lambda_k2[l]], axis=0)
        lw = _pack_layer(w_in[l], b_forget[l], mla_q_norm[l], w_mla_uq[l], mla_kv_norm[l], w_mla_uk[l], w_mla_uv[l],
                         lams, diff_subln[l], w_out[l], norm_mix_pre[l], norm_mix_post[l], norm_ffn_pre[l],
                         norm_ffn_post[l], norm_ple_pre[l], norm_ple_post[l], w_ffn_up[l], ffn_conv_w[l],
                         ffn_conv_b[l], w_ffn_down[l], w_ple_gate[l], w_ple_proj[l])
        lam_init = 0.8 - 0.6 * math.exp(-0.3 * l)
        hp, rows_p, cp = _layer(hp, (p_prompt.reshape(depth, bp * sp, PLE_DIM), l), lw, cos_p, sin_p, None,
                                jnp.zeros((bp, CONV_WIDTH - 1, D_FF), F32), lam_init, rows_p,
                                batch=bp, seq=sp, past=0, tq=tq, depth=depth)
        cache_l = (cache_fox_k[l], cache_fox_v[l], cache_fox_logf[l],
                   cache_mla_ckv.reshape(depth, bs * past, MLA_KV_RANK),
                   cache_mla_krope.reshape(depth, bs * past, MLA_ROPE_DIM), cache_diff_k[l], cache_diff_v[l])
        hs, rows_s, cs = _layer(hs, (p_sample.reshape(depth, bs * ts, PLE_DIM), l), lw, cos_s, sin_s, cache_l,
                                state_ffn_conv[l], lam_init, rows_s,
                                batch=bs, seq=ts, past=past, tq=LANE, depth=depth)
        conv_p.append(cp)
        conv_s.append(cs)

    def stack_rows(rows, batch, seq):
        fk, fv, lf, ckv, kr, dk, dv = rows
        lead = (depth, batch, seq)
        return (fk.reshape(lead + (N_FOX_HEADS, HEAD_DIM)), fv.reshape(lead + (N_FOX_HEADS, HEAD_DIM)),
                lf.reshape(lead + (N_FOX_HEADS,)), ckv.reshape(lead + (MLA_KV_RANK,)),
                kr.reshape(lead + (MLA_ROPE_DIM,)), dk.reshape(lead + (N_DIFF_HEADS, 2 * DIFF_QK_DIM)),
                dv.reshape(lead + (N_DIFF_HEADS, DIFF_V_DIM)))

    out_p = stack_rows(rows_p, bp, sp)
    out_s = stack_rows(rows_s, bs, ts)
    return ((hp.reshape(bp, sp, D_MODEL), hs.reshape(bs, ts, D_MODEL)) + out_p + (jnp.stack(conv_p, axis=0),)
            + out_s + (jnp.stack(conv_s, axis=0),))
```
